```python
import jax
import jax.numpy as jnp
from jax import lax
import numpy as np

D_MODEL = 1024
BATCH = 4
SEQ = 4096
DEPTH = 2

GRID_W = 64
CTX_LEN = 256
N_MOD = 9
D_FF = 2816
RMS_EPS = 1e-6
A_HEADS = 8
A_KV_HEADS = 2
A_HEAD_DIM = 64
WINDOW = 128
BLOCK = WINDOW
ROPE_BASE = 10000.0
B_HEADS = 4
B_DK = 64
B_DV = 128
B_GATE_RANK = 16
B_GATE_NORM = 16.0
B_CHUNK = 64
POOL_WINDOWS = (2, 4, 8, 16)
POOL_GROUP = D_MODEL // len(POOL_WINDOWS)
A_Q = A_HEADS * A_HEAD_DIM
A_KV = A_KV_HEADS * A_HEAD_DIM
B_QK = B_HEADS * B_DK
B_V = B_HEADS * B_DV
PROJ_SIZES = (A_Q, A_KV, A_KV, B_QK, B_QK, B_V, B_V, 2 * B_GATE_RANK)
PROJ_DIM = A_Q + 2 * A_KV + 2 * B_QK + 2 * B_V + 2 * B_GATE_RANK
MIX_OUT = A_Q + B_V

kernel_name = "hybrid_swa_gla_pool_prefix_dit"


def rmsnorm(x, g):
    x32 = x.astype(jnp.float32)
    y = x32 * lax.rsqrt(jnp.mean(x32 * x32, axis=-1, keepdims=True) + RMS_EPS)
    return y.astype(x.dtype) * g


def adaln(cond, w, b):
    mm = jax.nn.silu(cond) @ w + b
    mm = mm.reshape(mm.shape[:-1] + (N_MOD, mm.shape[-1] // N_MOD))
    return [mm[..., i, None, :] for i in range(N_MOD)]


def modulate(z, g, shift, scale):
    return rmsnorm(z, g) * (1.0 + scale) + shift


def swiglu(h, wi, wo):
    a, u = jnp.split(h @ wi, 2, axis=-1)
    return (jax.nn.silu(a) * u) @ wo


def _rotate(x, pos):
    n = x.shape[-1] // 2
    freqs = ROPE_BASE ** (-jnp.arange(n, dtype=jnp.float32) / n)
    ang = pos[:, None] * freqs
    cos = jnp.cos(ang)[:, None, :].astype(x.dtype)
    sin = jnp.sin(ang)[:, None, :].astype(x.dtype)
    x1, x2 = x[..., :n], x[..., n:]
    return jnp.concatenate([x1 * cos - x2 * sin, x2 * cos + x1 * sin], axis=-1)


def axial_rope(x, rows, cols):
    half = x.shape[-1] // 2
    return jnp.concatenate([_rotate(x[..., :half], rows), _rotate(x[..., half:], cols)], axis=-1)


def window_attention(q, k, v, kc, vc, sink):
    B, T, Hq, d = q.shape
    G = k.shape[2]
    R = Hq // G
    nb = T // BLOCK
    L = kc.shape[1]
    scale = d ** -0.5
    qb = q.reshape(B, nb, BLOCK, G, R, d)

    def band(a):
        ap = jnp.pad(a, ((0, 0), (BLOCK, BLOCK), (0, 0), (0, 0))).reshape(B, nb + 2, BLOCK, G, d)
        return jnp.concatenate([ap[:, :nb], ap[:, 1:nb + 1], ap[:, 2:]], axis=2)

    kb, vb = band(k), band(v)
    s_band = jnp.einsum('bnigrd,bnjgd->bgrnij', qb, kb).astype(jnp.float32) * scale
    s_ctx = jnp.einsum('bnigrd,blgd->bgrnil', qb, kc).astype(jnp.float32) * scale
    qpos = jnp.arange(nb)[:, None, None] * BLOCK + jnp.arange(BLOCK)[None, :, None]
    kpos = jnp.arange(nb)[:, None, None] * BLOCK - BLOCK + jnp.arange(3 * BLOCK)[None, None, :]
    valid = (kpos >= 0) & (kpos < T) & (jnp.abs(kpos - qpos) <= WINDOW)
    s_band = jnp.where(valid, s_band, -jnp.inf)
    sink_l = jnp.broadcast_to(sink.astype(jnp.float32).reshape(1, G, R, 1, 1, 1), s_band.shape[:-1] + (1,))
    p = jax.nn.softmax(jnp.concatenate([s_band, s_ctx, sink_l], axis=-1), axis=-1)
    nk = 3 * BLOCK
    o = (jnp.einsum('bgrnij,bnjgd->bnigrd', p[..., :nk].astype(vb.dtype), vb)
         + jnp.einsum('bgrnil,blgd->bnigrd', p[..., nk:nk + L].astype(vc.dtype), vc))
    return o.reshape(B, T, Hq * d)


def context_attention(qc, kc, vc, sink):
    B, L, Hq, d = qc.shape
    G = kc.shape[2]
    R = Hq // G
    s = jnp.einsum('blgrd,bmgd->bgrlm', qc.reshape(B, L, G, R, d), kc).astype(jnp.float32) * d ** -0.5
    sink_l = jnp.broadcast_to(sink.astype(jnp.float32).reshape(1, G, R, 1, 1), s.shape[:-1] + (1,))
    p = jax.nn.softmax(jnp.concatenate([s, sink_l], axis=-1), axis=-1)
    o = jnp.einsum('bgrlm,bmgd->blgrd', p[..., :L].astype(vc.dtype), vc)
    return o.reshape(B, L, Hq * d)


def gla_chunked(q, k, v, log_a, s0):
    B, T, H, dk = q.shape
    C = B_CHUNK
    n = T // C
    f32 = jnp.float32

    def chunks(a):
        return a.astype(f32).reshape(B, n, C, H, a.shape[-1])

    qc_ = chunks(q) * dk ** -0.5
    kc_ = chunks(k)
    vc_ = chunks(v)
    g = jnp.cumsum(chunks(log_a), axis=2)
    g_last = g[:, :, -1:]
    q_t = qc_ * jnp.exp(g)
    k_t = kc_ * jnp.exp(-g)
    k_end = kc_ * jnp.exp(g_last - g)
    lower = jnp.tril(jnp.ones((C, C), dtype=bool))
    att = jnp.where(lower, jnp.einsum('bnihd,bnjhd->bnhij', q_t, k_t), 0.0)
    o = jnp.einsum('bnhij,bnjhv->bnihv', att, vc_)
    d_state = jnp.einsum('bnjhd,bnjhv->bnhdv', k_end, vc_)
    decay = jnp.exp(g_last[:, :, 0])

    def step(S, inp):
        dec, ds = inp
        return dec[..., None] * S + ds, S

    s_final, s_prev = lax.scan(step, s0, (jnp.moveaxis(decay, 1, 0), jnp.moveaxis(d_state, 1, 0)))
    s_prev = jnp.moveaxis(s_prev, 0, 1)
    o = o + jnp.einsum('bnihd,bnhdv->bnihv', q_t, s_prev)
    return o.reshape(B, T, H, v.shape[-1]).astype(q.dtype), s_final


def bidir_gla(q, k, v, la_f, la_b, qc, kc, vc, lac_f, lac_b):
    flip = lambda a: a[:, ::-1]
    B = q.shape[0]
    zeros = jnp.zeros((B, B_HEADS, B_DK, B_DV), jnp.float32)
    oc_f, sc_f = gla_chunked(qc, kc, vc, lac_f, zeros)
    oc_b, sc_b = gla_chunked(flip(qc), flip(kc), flip(vc), flip(lac_b), zeros)
    o_f, _ = gla_chunked(q, k, v, la_f, sc_f)
    o_b, _ = gla_chunked(flip(q), flip(k), flip(v), flip(la_b), sc_b)
    return o_f + flip(o_b), oc_f + flip(oc_b)


def gla_output(o, r, gla_g):
    B, T = o.shape[:2]
    return rmsnorm(o, gla_g).reshape(B, T, B_V) * jax.nn.silu(r)


def mixer_ab(h, hc, rows, cols, need_ctx_out, w_in, w_a2_f, b_a_f, w_a2_b, b_a_b, sink, gla_g, w_out):
    split_points = [int(s) for s in np.cumsum(PROJ_SIZES)[:-1]]

    def project(z):
        Bz, Tz = z.shape[:2]
        qa, ka, va, qb, kb, vb, rb, zg = jnp.split(z @ w_in, split_points, axis=-1)
        la_f = jax.nn.log_sigmoid((zg[..., :B_GATE_RANK] @ w_a2_f + b_a_f).astype(jnp.float32)) / B_GATE_NORM
        la_b = jax.nn.log_sigmoid((zg[..., B_GATE_RANK:] @ w_a2_b + b_a_b).astype(jnp.float32)) / B_GATE_NORM
        return (qa.reshape(Bz, Tz, A_HEADS, A_HEAD_DIM),
                ka.reshape(Bz, Tz, A_KV_HEADS, A_HEAD_DIM),
                va.reshape(Bz, Tz, A_KV_HEADS, A_HEAD_DIM),
                qb.reshape(Bz, Tz, B_HEADS, B_DK),
                kb.reshape(Bz, Tz, B_HEADS, B_DK),
                vb.reshape(Bz, Tz, B_HEADS, B_DV),
                rb,
                la_f.reshape(Bz, Tz, B_HEADS, B_DK),
                la_b.reshape(Bz, Tz, B_HEADS, B_DK))

    qa, ka, va, qb, kb, vb, rb, la_f, la_b = project(h)
    cqa, cka, cva, cqb, ckb, cvb, crb, cla_f, cla_b = project(hc)
    o_a = window_attention(axial_rope(qa, rows, cols), axial_rope(ka, rows, cols), va, cka, cva, sink)
    o_b, oc_b = bidir_gla(qb, kb, vb, la_f, la_b, cqb, ckb, cvb, cla_f, cla_b)
    y = jnp.concatenate([o_a, gla_output(o_b, rb, gla_g)], axis=-1) @ w_out
    yc = None
    if need_ctx_out:
        oc_a = context_attention(cqa, cka, cva, sink)
        yc = jnp.concatenate([oc_a, gla_output(oc_b, crb, gla_g)], axis=-1) @ w_out
    return y, yc


def pool_mixer(h, w_pool, pool_scale):
    B, T, D = h.shape
    ng = len(POOL_WINDOWS)
    hg = h.astype(jnp.float32).reshape(B, T, ng, POOL_GROUP)
    prefix = jnp.pad(jnp.cumsum(hg, axis=1), ((0, 0), (1, 0), (0, 0), (0, 0)))
    t = jnp.arange(T)
    means = []
    for gi, w in enumerate(POOL_WINDOWS):
        lo = jnp.maximum(t - w // 2, 0)
        hi = jnp.minimum(t + (w - w // 2), T)
        total = prefix[:, hi, gi] - prefix[:, lo, gi]
        means.append(total / (hi - lo).astype(jnp.float32)[:, None])
    pooled = (jnp.stack(means, axis=2) - hg).astype(h.dtype)
    y = jnp.einsum('btgc,gce->btge', pooled, w_pool)
    return y.reshape(B, T, D) * pool_scale


def setup_inputs(seed: int = 0) -> dict:
    key = jax.random.key(seed)
    ks = jax.random.split(key, 24)
    D = D_MODEL
    ne = (DEPTH + 1) // 2
    no = DEPTH // 2

    def nrm(k, shape, scale=1.0):
        return jax.random.normal(k, shape, jnp.float32) * scale

    return {
        "x": nrm(ks[0], (BATCH, SEQ, D)),
        "c": nrm(ks[1], (BATCH, D)),
        "ctx": nrm(ks[2], (BATCH, CTX_LEN, D)),
        "c_ctx": nrm(ks[3], (D,)),
        "w_mod": nrm(ks[4], (DEPTH, D, N_MOD * D), 0.5 * D ** -0.5),
        "b_mod": nrm(ks[5], (DEPTH, N_MOD * D), 0.01),
        "norm_g": 1.0 + nrm(ks[6], (DEPTH, 3, D), 0.05),
        "ffn1_wi": nrm(ks[7], (DEPTH, D, 2 * D_FF), D ** -0.5),
        "ffn1_wo": nrm(ks[8], (DEPTH, D_FF, D), D_FF ** -0.5),
        "ffn2_wi": nrm(ks[9], (DEPTH, D, 2 * D_FF), D ** -0.5),
        "ffn2_wo": nrm(ks[10], (DEPTH, D_FF, D), D_FF ** -0.5),
        "w_in": nrm(ks[11], (ne, D, PROJ_DIM), D ** -0.5),
        "w_a2_f": nrm(ks[12], (ne, B_GATE_RANK, B_QK), B_GATE_RANK ** -0.5),
        "b_a_f": nrm(ks[13], (ne, B_QK), 0.1),
        "w_a2_b": nrm(ks[14], (ne, B_GATE_RANK, B_QK), B_GATE_RANK ** -0.5),
        "b_a_b": nrm(ks[15], (ne, B_QK), 0.1),
        "sink": nrm(ks[16], (ne, A_HEADS), 1.0),
        "gla_g": 1.0 + nrm(ks[17], (ne, B_DV), 0.05),
        "w_out": nrm(ks[18], (ne, MIX_OUT, D), MIX_OUT ** -0.5),
        "w_pool": nrm(ks[19], (no, len(POOL_WINDOWS), POOL_GROUP, POOL_GROUP), POOL_GROUP ** -0.5),
        "pool_scale": 1.0 + nrm(ks[20], (no, D), 0.1),
        "final_g": 1.0 + nrm(ks[21], (D,), 0.05),
    }


def reference(x, c, ctx, c_ctx, w_mod, b_mod, norm_g, ffn1_wi, ffn1_wo, ffn2_wi, ffn2_wo,
              w_in, w_a2_f, b_a_f, w_a2_b, b_a_b, sink, gla_g, w_out, w_pool, pool_scale, final_g):
    T = x.shape[1]
    ROWS = T // GRID_W
    rows = jnp.repeat(jnp.arange(ROWS, dtype=jnp.float32), GRID_W)
    cols = jnp.tile(jnp.arange(GRID_W, dtype=jnp.float32), ROWS)

    for l in range(DEPTH):
        even = l % 2 == 0
        ctx_out = any(j % 2 == 0 for j in range(l + 1, DEPTH))
        ctx_in = even or ctx_out
        m = adaln(c, w_mod[l], b_mod[l])
        mc = adaln(c_ctx, w_mod[l], b_mod[l]) if ctx_in else None

        x = x + 0.5 * m[2] * swiglu(modulate(x, norm_g[l, 0], m[0], m[1]), ffn1_wi[l], ffn1_wo[l])
        if ctx_in:
            ctx = ctx + 0.5 * mc[2] * swiglu(modulate(ctx, norm_g[l, 0], mc[0], mc[1]), ffn1_wi[l], ffn1_wo[l])

        h = modulate(x, norm_g[l, 1], m[3], m[4])
        if even:
            e = l // 2
            hc = modulate(ctx, norm_g[l, 1], mc[3], mc[4])
            y, yc = mixer_ab(h, hc, rows, cols, ctx_out, w_in[e], w_a2_f[e], b_a_f[e], w_a2_b[e], b_a_b[e],
                             sink[e], gla_g[e], w_out[e])
        else:
            o = l // 2
            y = pool_mixer(h, w_pool[o], pool_scale[o])
            yc = pool_mixer(modulate(ctx, norm_g[l, 1], mc[3], mc[4]), w_pool[o], pool_scale[o]) if ctx_out else None
        x = x + m[5] * y
        if ctx_out:
            ctx = ctx + mc[5] * yc

        x = x + 0.5 * m[8] * swiglu(modulate(x, norm_g[l, 2], m[6], m[7]), ffn2_wi[l], ffn2_wo[l])
        if ctx_out:
            ctx = ctx + 0.5 * mc[8] * swiglu(modulate(ctx, norm_g[l, 2], mc[6], mc[7]), ffn2_wi[l], ffn2_wo[l])

    return rmsnorm(x, final_g)
```

```python
import functools

import jax
import jax.numpy as jnp
import numpy as np
from jax import lax
from jax.experimental import pallas as pl
from jax.experimental.pallas import tpu as pltpu

F32 = jnp.float32
BF16 = jnp.bfloat16

D_MODEL = 1024
N_MOD = 9
D_FF = 2816
RMS_EPS = 1e-6
GRID_W = 64
A_HEADS = 8
A_KV_HEADS = 2
A_HEAD_DIM = 64
WINDOW = 128
ROPE_BASE = 10000.0
B_HEADS = 4
B_DK = 64
B_DV = 128
B_GATE_RANK = 16
B_GATE_NORM = 16.0
B_CHUNK = 64
POOL_WINDOWS = (2, 4, 8, 16)
POOL_GROUP = D_MODEL // len(POOL_WINDOWS)
A_Q = A_HEADS * A_HEAD_DIM
A_KV = A_KV_HEADS * A_HEAD_DIM
B_QK = B_HEADS * B_DK
B_V = B_HEADS * B_DV
PROJ_MAIN = A_Q + 2 * A_KV + 2 * B_QK + 2 * B_V

LANES = 128
COND_ROWS = 8
VMEM_LIMIT = 56 * 1024 * 1024

TM_FFN = 256
TM_PROJ = 256
TM_OUT = 512
TM_POOL = 256
TB_GLA = 256
HALO = 8


def _cparams(*sem):
    return pltpu.CompilerParams(dimension_semantics=sem, vmem_limit_bytes=VMEM_LIMIT)


def _const_spec(shape):
    nd = len(shape)
    return pl.BlockSpec(shape, lambda *_: (0,) * nd, pipeline_mode=pl.Buffered(1))


def _modulate(x, g, shift, scale):
    y = x * lax.rsqrt(jnp.mean(x * x, axis=-1, keepdims=True) + RMS_EPS)
    return (y * g) * (1.0 + scale) + shift


def _silu(a):
    return a * jax.nn.sigmoid(a)


def _adaln_kernel(c_ref, w_ref, b_ref, o_ref):
    s = _silu(c_ref[...]).astype(BF16)
    o_ref[...] = jnp.dot(s, w_ref[...].astype(BF16), preferred_element_type=F32) + b_ref[...]


def _adaln(cond, w_mod, b_mod):
    depth, d, n = w_mod.shape
    tn = 1024
    return pl.pallas_call(
        _adaln_kernel,
        grid=(depth, n // tn),
        in_specs=[
            pl.BlockSpec((COND_ROWS, d), lambda l, j: (0, 0)),
            pl.BlockSpec((None, d, tn), lambda l, j: (l, 0, j)),
            pl.BlockSpec((None, 1, tn), lambda l, j: (l, 0, j)),
        ],
        out_specs=pl.BlockSpec((None, COND_ROWS, tn), lambda l, j: (l, 0, j)),
        out_shape=jax.ShapeDtypeStruct((depth, COND_ROWS, n), F32),
        compiler_params=_cparams("parallel", "parallel"),
        name="adaln",
    )(cond, w_mod, b_mod.reshape(depth, 1, n))


def _ffn_kernel(x_ref, mod_ref, g_ref, wi_ref, wo_ref, *rest, mi, final):
    if final:
        fg_ref, o_ref = rest
    else:
        (o_ref,) = rest
    x = x_ref[...]
    h = _modulate(x, g_ref[...], mod_ref[mi:mi + 1, :], mod_ref[mi + 1:mi + 2, :]).astype(BF16)
    hh = jnp.dot(h, wi_ref[...], preferred_element_type=F32)
    act = (_silu(hh[:, :D_FF]) * hh[:, D_FF:]).astype(BF16)
    y = jnp.dot(act, wo_ref[...], preferred_element_type=F32)
    o = x + (0.5 * mod_ref[mi + 2:mi + 3, :]) * y
    if final:
        o = (o * lax.rsqrt(jnp.mean(o * o, axis=-1, keepdims=True) + RMS_EPS)) * fg_ref[...]
    o_ref[...] = o


def _ffn(x, mods, row_of_batch, g, wi, wo, mi, final_g=None):
    b, t, d = x.shape
    tm = min(TM_FFN, t)
    final = final_g is not None
    in_specs = [
        pl.BlockSpec((None, tm, d), lambda bi, i: (bi, i, 0)),
        pl.BlockSpec((None, N_MOD, d), lambda bi, i: (row_of_batch(bi), 0, 0)),
        _const_spec((1, d)),
        _const_spec(wi.shape),
        _const_spec(wo.shape),
    ]
    args = [x, mods, g.reshape(1, d), wi, wo]
    if final:
        in_specs.append(_const_spec((1, d)))
        args.append(final_g.reshape(1, d))
    return pl.pallas_call(
        functools.partial(_ffn_kernel, mi=mi, final=final),
        grid=(b, t // tm),
        in_specs=in_specs,
        out_specs=pl.BlockSpec((None, tm, d), lambda bi, i: (bi, i, 0)),
        out_shape=jax.ShapeDtypeStruct((b, t, d), F32),
        compiler_params=_cparams("parallel", "parallel"),
        name="ffn_final" if final else "ffn",
    )(*args)


def _swap16(x):
    lane = lax.broadcasted_iota(jnp.int32, x.shape, 1)
    return jnp.where((lane & 16) == 0, pltpu.roll(x, LANES - 16, 1), pltpu.roll(x, 16, 1))


def _dup_halves(x):
    lane = lax.broadcasted_iota(jnp.int32, x.shape, 1)
    r = pltpu.roll(x, LANES // 2, 1)
    lo = lane < LANES // 2
    return jnp.where(lo, x, r), jnp.where(lo, r, x)


def _proj_kernel(x_ref, mod_ref, g_ref, w_ref, wz_ref, w2_ref, b2_ref, cos_ref, sin_ref,
                 qa_ref, kd_ref, vd_ref, qb_ref, kb_ref, vb_ref, rb_ref, la_ref):
    h = _modulate(x_ref[...], g_ref[...], mod_ref[3:4, :], mod_ref[4:5, :]).astype(BF16)
    z = jnp.dot(h, w_ref[...], preferred_element_type=F32)
    cos = cos_ref[...]
    sin = sin_ref[...]

    def rope(v):
        return v * cos + _swap16(v) * sin

    scale = A_HEAD_DIM ** -0.5
    for p in range(A_Q // LANES):
        q = z[:, p * LANES:(p + 1) * LANES]
        qa_ref[:, p * LANES:(p + 1) * LANES] = (rope(q) * scale).astype(BF16)
    k0, k1 = _dup_halves(rope(z[:, A_Q:A_Q + A_KV]))
    kd_ref[:, :LANES] = k0.astype(BF16)
    kd_ref[:, LANES:] = k1.astype(BF16)
    v0, v1 = _dup_halves(z[:, A_Q + A_KV:A_Q + 2 * A_KV])
    vd_ref[:, :LANES] = v0.astype(BF16)
    vd_ref[:, LANES:] = v1.astype(BF16)
    o = A_Q + 2 * A_KV
    qb_ref[...] = z[:, o:o + B_QK]
    kb_ref[...] = z[:, o + B_QK:o + 2 * B_QK]
    vb_ref[...] = z[:, o + 2 * B_QK:o + 2 * B_QK + B_V].astype(BF16)
    rb_ref[...] = z[:, o + 2 * B_QK + B_V:o + 2 * B_QK + 2 * B_V]
    zg = jnp.dot(h, wz_ref[...], preferred_element_type=F32).astype(BF16)
    pre = jnp.dot(zg, w2_ref[...], preferred_element_type=F32) + b2_ref[...]
    la_ref[...] = (jnp.minimum(pre, 0.0) - jnp.log1p(jnp.exp(-jnp.abs(pre)))) / B_GATE_NORM


def _proj(x, mods, row_of_batch, g, w_main, w_zg, w2, b2, cos, sin):
    b, t, d = x.shape
    tm = min(TM_PROJ, t)
    widths = [(A_Q, BF16), (2 * A_KV, BF16), (2 * A_KV, BF16), (B_QK, F32), (B_QK, F32),
              (B_V, BF16), (B_V, F32), (2 * B_QK, F32)]
    return pl.pallas_call(
        _proj_kernel,
        grid=(b, t // tm),
        in_specs=[
            pl.BlockSpec((None, tm, d), lambda bi, i: (bi, i, 0)),
            pl.BlockSpec((None, N_MOD, d), lambda bi, i: (row_of_batch(bi), 0, 0)),
            _const_spec((1, d)),
            _const_spec(w_main.shape),
            _const_spec(w_zg.shape),
            _const_spec(w2.shape),
            _const_spec(b2.shape),
            pl.BlockSpec((tm, LANES), lambda bi, i: (i, 0)),
            pl.BlockSpec((tm, LANES), lambda bi, i: (i, 0)),
        ],
        out_specs=[pl.BlockSpec((None, tm, w), lambda bi, i: (bi, i, 0)) for w, _ in widths],
        out_shape=[jax.ShapeDtypeStruct((b, t, w), dt) for w, dt in widths],
        compiler_params=_cparams("parallel", "parallel"),
        name="mixer_proj",
    )(x, mods, g.reshape(1, d), w_main, w_zg, w2, b2, cos, sin)


def _attn_kernel(sink_ref, q_ref, kp_ref, kc_ref, kn_ref, vp_ref, vc_ref, vn_ref, ck_ref, cv_ref, o_ref, *, nb):
    n = pl.program_id(1)
    blk = WINDOW
    row = lax.broadcasted_iota(jnp.int32, (blk, blk), 0)
    col = lax.broadcasted_iota(jnp.int32, (blk, blk), 1)
    ninf = jnp.float32(-jnp.inf)
    bias_prev = jnp.where((col >= row) & (n > 0), 0.0, ninf)
    bias_next = jnp.where((col <= row) & (n < nb - 1), 0.0, ninf)
    lctx = ck_ref.shape[0]
    bias = jnp.concatenate([bias_prev, jnp.zeros((blk, blk), F32), bias_next, jnp.zeros((blk, lctx), F32)], axis=1)
    lane = lax.broadcasted_iota(jnp.int32, (blk, LANES), 1)
    lo = lane < LANES // 2
    rep = A_HEADS // A_KV_HEADS
    for g in range(A_KV_HEADS):
        gs = slice(g * LANES, (g + 1) * LANES)
        kd = jnp.concatenate([kp_ref[:, gs], kc_ref[:, gs], kn_ref[:, gs], ck_ref[:, gs]], axis=0)
        vd = jnp.concatenate([vp_ref[:, gs], vc_ref[:, gs], vn_ref[:, gs], cv_ref[:, gs]], axis=0)
        qs = []
        for r in range(rep):
            hd = g * rep + r
            pair = q_ref[:, (hd // 2) * LANES:(hd // 2 + 1) * LANES]
            keep = lo if hd % 2 == 0 else jnp.logical_not(lo)
            qs.append(jnp.where(keep, pair, jnp.zeros_like(pair)))
        s = lax.dot_general(jnp.concatenate(qs, axis=0), kd, (((1,), (1,)), ((), ())),
                            preferred_element_type=F32)
        ps, dens = [], []
        for r in range(rep):
            sk = sink_ref[g * rep + r]
            sr = s[r * blk:(r + 1) * blk, :] + bias
            m = jnp.maximum(jnp.max(sr, axis=-1, keepdims=True), sk)
            p = jnp.exp(sr - m)
            dens.append(jnp.sum(p, axis=-1, keepdims=True) + jnp.exp(sk - m))
            ps.append(p.astype(BF16))
        o = jnp.dot(jnp.concatenate(ps, axis=0), vd, preferred_element_type=F32)
        outs = [o[r * blk:(r + 1) * blk, :] / dens[r] for r in range(rep)]
        for pp in range(rep // 2):
            pair_idx = (g * rep) // 2 + pp
            o_ref[:, pair_idx * LANES:(pair_idx + 1) * LANES] = jnp.where(
                lo, outs[2 * pp], outs[2 * pp + 1]).astype(BF16)


def _attention(sink, qa, kd, vd, ckd, cvd):
    b, t, _ = qa.shape
    nb = t // WINDOW
    lctx = ckd.shape[1]
    kv_w = 2 * A_KV

    def blk(off):
        return pl.BlockSpec((None, WINDOW, kv_w), lambda bi, n: (bi, jnp.clip(n + off, 0, nb - 1), 0))

    ctx_spec = pl.BlockSpec((None, lctx, kv_w), lambda bi, n: (bi, 0, 0))
    return pl.pallas_call(
        functools.partial(_attn_kernel, nb=nb),
        grid=(b, nb),
        in_specs=[
            pl.BlockSpec(memory_space=pltpu.SMEM),
            pl.BlockSpec((None, WINDOW, A_Q), lambda bi, n: (bi, n, 0)),
            blk(-1), blk(0), blk(1), blk(-1), blk(0), blk(1), ctx_spec, ctx_spec,
        ],
        out_specs=pl.BlockSpec((None, WINDOW, A_Q), lambda bi, n: (bi, n, 0)),
        out_shape=jax.ShapeDtypeStruct((b, t, A_Q), BF16),
        compiler_params=_cparams("parallel", "parallel"),
        name="window_attn",
    )(sink, qa, kd, kd, kd, vd, vd, vd, ckd, cvd)


def _gla_chunk(q, k, v, la, st, tri, reverse, want_out):
    c = B_CHUNK
    hi = la.astype(BF16)
    lo = (la - hi.astype(F32)).astype(BF16)
    g = jnp.dot(tri, hi, preferred_element_type=F32) + jnp.dot(tri, lo, preferred_element_type=F32)
    g_last = g[0:1, :] if reverse else g[c - 1:c, :]
    k_end = (k * jnp.exp(g_last - g)).astype(BF16)
    o = None
    if want_out:
        q_t = (q * (B_DK ** -0.5)) * jnp.exp(g)
        k_t = (k * jnp.exp(-g)).astype(BF16)
        lane = lax.broadcasted_iota(jnp.int32, q_t.shape, 1)
        qstack = jnp.concatenate(
            [jnp.where((lane >= h * B_DK) & (lane < (h + 1) * B_DK), q_t, 0.0) for h in range(B_HEADS)],
            axis=0).astype(BF16)
        att = lax.dot_general(qstack, k_t, (((1,), (1,)), ((), ())), preferred_element_type=F32)
        i = lax.broadcasted_iota(jnp.int32, att.shape, 0) % c
        j = lax.broadcasted_iota(jnp.int32, att.shape, 1)
        att = jnp.where((j >= i) if reverse else (j <= i), att, 0.0).astype(BF16)
        r = (jnp.dot(att, v, preferred_element_type=F32)
             + lax.dot_general(qstack, st.astype(BF16), (((1,), (1,)), ((), ())), preferred_element_type=F32))
        o = jnp.concatenate([r[h * c:(h + 1) * c, h * B_DV:(h + 1) * B_DV] for h in range(B_HEADS)], axis=1)
    d_state = lax.dot_general(v, k_end, (((0,), (0,)), ((), ())), preferred_element_type=F32)
    return o, st * jnp.exp(g_last) + d_state


def _tri(reverse):
    i = lax.broadcasted_iota(jnp.int32, (B_CHUNK, B_CHUNK), 0)
    j = lax.broadcasted_iota(jnp.int32, (B_CHUNK, B_CHUNK), 1)
    return jnp.where((j >= i) if reverse else (j <= i), 1.0, 0.0).astype(BF16)


def _gla_ctx_kernel(k_ref, v_ref, la_ref, st_ref, *, reverse):
    nc = k_ref.shape[0] // B_CHUNK
    tri = _tri(reverse)
    st = jnp.zeros(st_ref.shape, F32)
    for ci in (reversed(range(nc)) if reverse else range(nc)):
        rows = slice(ci * B_CHUNK, (ci + 1) * B_CHUNK)
        _, st = _gla_chunk(None, k_ref[rows, :], v_ref[rows, :], la_ref[rows, :], st, tri, reverse, False)
    st_ref[...] = st


def _gla_ctx_state(kb, vb, la, reverse):
    b, l, _ = kb.shape
    d = 1 if reverse else 0
    return pl.pallas_call(
        functools.partial(_gla_ctx_kernel, reverse=reverse),
        grid=(b,),
        in_specs=[
            pl.BlockSpec((None, l, B_QK), lambda bi: (bi, 0, 0)),
            pl.BlockSpec((None, l, B_V), lambda bi: (bi, 0, 0)),
            pl.BlockSpec((None, l, B_QK), lambda bi: (bi, 0, d)),
        ],
        out_specs=pl.BlockSpec((None, B_V, B_QK), lambda bi: (bi, 0, 0)),
        out_shape=jax.ShapeDtypeStruct((b, B_V, B_QK), F32),
        compiler_params=_cparams("parallel"),
        name="gla_ctx_state",
    )(kb, vb, la)


def _gla_kernel(q_ref, k_ref, v_ref, la_ref, st0_ref, *rest, reverse):
    if reverse:
        of_ref, r_ref, gg_ref, o_ref, st_ref = rest
    else:
        o_ref, st_ref = rest

    @pl.when(pl.program_id(1) == 0)
    def _():
        st_ref[...] = st0_ref[...]

    nc = q_ref.shape[0] // B_CHUNK
    tri = _tri(reverse)
    st = st_ref[...]
    for ci in (reversed(range(nc)) if reverse else range(nc)):
        rows = slice(ci * B_CHUNK, (ci + 1) * B_CHUNK)
        o, st = _gla_chunk(q_ref[rows, :], k_ref[rows, :], v_ref[rows, :], la_ref[rows, :], st, tri, reverse, True)
        if reverse:
            o = o + of_ref[rows, :]
            outs = []
            for h in range(B_HEADS):
                oh = o[:, h * B_DV:(h + 1) * B_DV]
                oh = (oh * lax.rsqrt(jnp.mean(oh * oh, axis=-1, keepdims=True) + RMS_EPS)) * gg_ref[...]
                outs.append(oh * _silu(r_ref[rows, h * B_DV:(h + 1) * B_DV]))
            o_ref[rows, :] = jnp.concatenate(outs, axis=1).astype(BF16)
        else:
            o_ref[rows, :] = o
    st_ref[...] = st


def _gla(qb, kb, vb, la, st0, reverse, o_fwd=None, rb=None, gla_g=None):
    b, t, _ = qb.shape
    tb = TB_GLA
    nt = t // tb
    d = 1 if reverse else 0

    def tok(w, lane_blk=0):
        if reverse:
            return pl.BlockSpec((None, tb, w), lambda bi, j: (bi, nt - 1 - j, lane_blk))
        return pl.BlockSpec((None, tb, w), lambda bi, j: (bi, j, lane_blk))

    in_specs = [tok(B_QK), tok(B_QK), tok(B_V), tok(B_QK, d),
                pl.BlockSpec((None, B_V, B_QK), lambda bi, j: (bi, 0, 0))]
    args = [qb, kb, vb, la, st0]
    if reverse:
        in_specs += [tok(B_V), tok(B_V), _const_spec((1, B_DV))]
        args += [o_fwd, rb, gla_g.reshape(1, B_DV)]
    return pl.pallas_call(
        functools.partial(_gla_kernel, reverse=reverse),
        grid=(b, nt),
        in_specs=in_specs,
        out_specs=tok(B_V),
        out_shape=jax.ShapeDtypeStruct((b, t, B_V), BF16 if reverse else F32),
        scratch_shapes=[pltpu.VMEM((B_V, B_QK), F32)],
        compiler_params=_cparams("parallel", "arbitrary"),
        name="gla_bwd" if reverse else "gla_fwd",
    )(*args)


def _mix_out_kernel(x_ref, mod_ref, oa_ref, ob_ref, w_ref, o_ref):
    y = (jnp.dot(oa_ref[...], w_ref[:A_Q, :], preferred_element_type=F32)
         + jnp.dot(ob_ref[...], w_ref[A_Q:, :], preferred_element_type=F32))
    o_ref[...] = x_ref[...] + mod_ref[5:6, :] * y


def _mix_out(x, mods, oa, ob, w_out):
    b, t, d = x.shape
    tm = TM_OUT

    def tok(w):
        return pl.BlockSpec((None, tm, w), lambda bi, i: (bi, i, 0))

    return pl.pallas_call(
        _mix_out_kernel,
        grid=(b, t // tm),
        in_specs=[tok(d), pl.BlockSpec((None, N_MOD, d), lambda bi, i: (bi, 0, 0)), tok(A_Q), tok(B_V),
                  _const_spec(w_out.shape)],
        out_specs=tok(d),
        out_shape=jax.ShapeDtypeStruct((b, t, d), F32),
        compiler_params=_cparams("parallel", "parallel"),
        name="mixer_out",
    )(x, mods, oa, ob, w_out)


def _shift_rows(e, d):
    n = e.shape[0]
    return pltpu.roll(e, (-d) % n, 0)


def _pool_kernel(x_ref, xp_ref, xn_ref, mod_ref, g_ref, w_ref, ps_ref, o_ref, *, seq):
    i = pl.program_id(1)
    tm = x_ref.shape[0]
    g = g_ref[...]
    shift = mod_ref[3:4, :]
    scale = mod_ref[4:5, :]
    x = x_ref[...]
    h = _modulate(x, g, shift, scale)
    hp = jnp.where(i > 0, _modulate(xp_ref[...], g, shift, scale), 0.0)
    hn = jnp.where(i < pl.num_programs(1) - 1, _modulate(xn_ref[...], g, shift, scale), 0.0)
    t = i * tm + lax.broadcasted_iota(jnp.int32, (tm, 1), 0)
    ys = []
    for gi, w in enumerate(POOL_WINDOWS):
        ls = slice(gi * POOL_GROUP, (gi + 1) * POOL_GROUP)
        hg = h[:, ls]
        p = jnp.concatenate([hp[:, ls], hg, hn[:, ls]], axis=0)
        p = _shift_rows(p, -1) + p
        step = 1
        while 2 * step < w:
            p = _shift_rows(p, -step) + _shift_rows(p, step)
            step *= 2
        total = p[HALO:HALO + tm, :]
        cnt = (jnp.minimum(t + w // 2, seq) - jnp.maximum(t - w // 2, 0)).astype(F32)
        pooled = (total / cnt - hg).astype(BF16)
        ys.append(jnp.dot(pooled, w_ref[gi], preferred_element_type=F32))
    y = jnp.concatenate(ys, axis=1) * ps_ref[...]
    o_ref[...] = x + mod_ref[5:6, :] * y


def _pool(x, mods, g, w_pool, pool_scale):
    b, t, d = x.shape
    tm = TM_POOL
    hb = tm // HALO
    last = t // HALO - 1
    return pl.pallas_call(
        functools.partial(_pool_kernel, seq=t),
        grid=(b, t // tm),
        in_specs=[
            pl.BlockSpec((None, tm, d), lambda bi, i: (bi, i, 0)),
            pl.BlockSpec((None, HALO, d), lambda bi, i: (bi, jnp.maximum(i * hb - 1, 0), 0)),
            pl.BlockSpec((None, HALO, d), lambda bi, i: (bi, jnp.minimum((i + 1) * hb, last), 0)),
            pl.BlockSpec((None, N_MOD, d), lambda bi, i: (bi, 0, 0)),
            _const_spec((1, d)),
            _const_spec(w_pool.shape),
            _const_spec((1, d)),
        ],
        out_specs=pl.BlockSpec((None, tm, d), lambda bi, i: (bi, i, 0)),
        out_shape=jax.ShapeDtypeStruct((b, t, d), F32),
        compiler_params=_cparams("parallel", "parallel"),
        name="pool_mixer",
    )(x, x, x, mods, g.reshape(1, d), w_pool, pool_scale.reshape(1, d))


def _rope_tables(t):
    n = A_HEAD_DIM // 4
    freqs = ROPE_BASE ** (-jnp.arange(n, dtype=F32) / n)
    pos = jnp.arange(t)
    rows = (pos // GRID_W).astype(F32)
    cols = (pos % GRID_W).astype(F32)
    ang_r = rows[:, None] * freqs
    ang_c = cols[:, None] * freqs
    cos = jnp.concatenate([jnp.cos(ang_r), jnp.cos(ang_r), jnp.cos(ang_c), jnp.cos(ang_c)], axis=1)
    sin = jnp.concatenate([-jnp.sin(ang_r), jnp.sin(ang_r), -jnp.sin(ang_c), jnp.sin(ang_c)], axis=1)
    return jnp.tile(cos, (1, 2)), jnp.tile(sin, (1, 2))


def kernel(x, c, ctx, c_ctx, w_mod, b_mod, norm_g, ffn1_wi, ffn1_wo, ffn2_wi, ffn2_wo,
           w_in, w_a2_f, b_a_f, w_a2_b, b_a_b, sink, gla_g, w_out, w_pool, pool_scale, final_g):
    b, t, d = x.shape
    lctx = ctx.shape[1]
    ctx_row = b

    cond = jnp.zeros((COND_ROWS, d), F32).at[:b].set(c).at[ctx_row].set(c_ctx)
    mods = _adaln(cond, w_mod, b_mod).reshape(w_mod.shape[0], COND_ROWS, N_MOD, d)
    by_batch = lambda bi: bi
    ctx_rows = lambda bi: ctx_row

    wi1 = ffn1_wi.astype(BF16)
    wo1 = ffn1_wo.astype(BF16)
    wi2 = ffn2_wi.astype(BF16)
    wo2 = ffn2_wo.astype(BF16)

    x = _ffn(x, mods[0], by_batch, norm_g[0, 0], wi1[0], wo1[0], 0)
    ctx = _ffn(ctx, mods[0], ctx_rows, norm_g[0, 0], wi1[0], wo1[0], 0)

    w_main = w_in[0][:, :PROJ_MAIN].astype(BF16)
    w_zg = jnp.zeros((d, LANES), F32).at[:, :2 * B_GATE_RANK].set(w_in[0][:, PROJ_MAIN:]).astype(BF16)
    w2 = jnp.zeros((LANES, 2 * B_QK), F32)
    w2 = w2.at[:B_GATE_RANK, :B_QK].set(w_a2_f[0]).at[B_GATE_RANK:2 * B_GATE_RANK, B_QK:].set(w_a2_b[0]).astype(BF16)
    b2 = jnp.concatenate([b_a_f[0], b_a_b[0]]).reshape(1, 2 * B_QK)
    cos, sin = _rope_tables(t)
    qa, kd, vd, qb, kb, vb, rb, la = _proj(x, mods[0], by_batch, norm_g[0, 1], w_main, w_zg, w2, b2, cos, sin)
    _, ckd, cvd, _, ckb, cvb, _, cla = _proj(ctx, mods[0], ctx_rows, norm_g[0, 1], w_main, w_zg, w2, b2,
                                              jnp.ones((lctx, LANES), F32), jnp.zeros((lctx, LANES), F32))

    o_a = _attention(sink[0], qa, kd, vd, ckd, cvd)
    st_f = _gla_ctx_state(ckb, cvb, cla, False)
    st_b = _gla_ctx_state(ckb, cvb, cla, True)
    o_f = _gla(qb, kb, vb, la, st_f, False)
    o_g = _gla(qb, kb, vb, la, st_b, True, o_f, rb, gla_g[0])
    x = _mix_out(x, mods[0], o_a, o_g, w_out[0].astype(BF16))
    x = _ffn(x, mods[0], by_batch, norm_g[0, 2], wi2[0], wo2[0], 6)

    x = _ffn(x, mods[1], by_batch, norm_g[1, 0], wi1[1], wo1[1], 0)
    x = _pool(x, mods[1], norm_g[1, 1], w_pool[0].astype(BF16), pool_scale[0])
    x = _ffn(x, mods[1], by_batch, norm_g[1, 2], wi2[1], wo2[1], 6, final_g=final_g)
    return x
```

```python
import functools

import jax
import jax.numpy as jnp
import numpy as np
from jax import lax
from jax.experimental import pallas as pl
from jax.experimental.pallas import tpu as pltpu

F32 = jnp.float32
BF16 = jnp.bfloat16

D_MODEL = 1024
N_MOD = 9
D_FF = 2816
RMS_EPS = 1e-6
GRID_W = 64
A_HEADS = 8
A_KV_HEADS = 2
A_HEAD_DIM = 64
WINDOW = 128
ROPE_BASE = 10000.0
B_HEADS = 4
B_DK = 64
B_DV = 128
B_GATE_RANK = 16
B_GATE_NORM = 16.0
B_CHUNK = 64
POOL_WINDOWS = (2, 4, 8, 16)
POOL_GROUP = D_MODEL // len(POOL_WINDOWS)
A_Q = A_HEADS * A_HEAD_DIM
A_KV = A_KV_HEADS * A_HEAD_DIM
B_QK = B_HEADS * B_DK
B_V = B_HEADS * B_DV
PROJ_MAIN = A_Q + 2 * A_KV + 2 * B_QK + 2 * B_V
KV_DUP = 4 * A_KV

LANES = 128
COND_ROWS = 8
VMEM_LIMIT = 56 * 1024 * 1024

TM_FFN = 512
SUB_FFN = 256
TM_PROJ = 256
TM_OUT = 512
TM_POOL = 256
TB_GLA = 512
HALO = 8


def _cparams(*sem):
    return pltpu.CompilerParams(dimension_semantics=sem, vmem_limit_bytes=VMEM_LIMIT)


def _const_spec(shape):
    nd = len(shape)
    return pl.BlockSpec(shape, lambda *_: (0,) * nd, pipeline_mode=pl.Buffered(1))


def _modulate(x, g, shift, scale):
    y = x * lax.rsqrt(jnp.mean(x * x, axis=-1, keepdims=True) + RMS_EPS)
    return (y * g) * (1.0 + scale) + shift


def _silu(a):
    return a * jax.nn.sigmoid(a)


def _adaln_kernel(c_ref, w_ref, b_ref, o_ref):
    s = _silu(c_ref[...]).astype(BF16)
    o_ref[...] = jnp.dot(s, w_ref[...].astype(BF16), preferred_element_type=F32) + b_ref[...]


def _adaln(cond, w_mod, b_mod):
    depth, d, n = w_mod.shape
    tn = 1024
    return pl.pallas_call(
        _adaln_kernel,
        grid=(depth, n // tn),
        in_specs=[
            pl.BlockSpec((COND_ROWS, d), lambda l, j: (0, 0)),
            pl.BlockSpec((None, d, tn), lambda l, j: (l, 0, j)),
            pl.BlockSpec((None, 1, tn), lambda l, j: (l, 0, j)),
        ],
        out_specs=pl.BlockSpec((None, COND_ROWS, tn), lambda l, j: (l, 0, j)),
        out_shape=jax.ShapeDtypeStruct((depth, COND_ROWS, n), F32),
        compiler_params=_cparams("parallel", "parallel"),
        name="adaln",
    )(cond, w_mod, b_mod.reshape(depth, 1, n))


def _ffn_kernel(x_ref, mod_ref, g_ref, wi_ref, wo_ref, *rest, mi, final):
    if final:
        fg_ref, o_ref = rest
    else:
        (o_ref,) = rest
    sub = min(SUB_FFN, x_ref.shape[0])
    nsub = x_ref.shape[0] // sub

    def normed(s):
        xs = x_ref[s * sub:(s + 1) * sub, :]
        return _modulate(xs, g_ref[...], mod_ref[mi:mi + 1, :], mod_ref[mi + 1:mi + 2, :]).astype(BF16)

    h_next = normed(0)
    for s in range(nsub):
        rows = slice(s * sub, (s + 1) * sub)
        hh = jnp.dot(h_next, wi_ref[...], preferred_element_type=F32)
        if s + 1 < nsub:
            h_next = normed(s + 1)
        act = (_silu(hh[:, :D_FF]) * hh[:, D_FF:]).astype(BF16)
        y = jnp.dot(act, wo_ref[...], preferred_element_type=F32)
        x = x_ref[rows, :]
        o = x + (0.5 * mod_ref[mi + 2:mi + 3, :]) * y
        if final:
            o = (o * lax.rsqrt(jnp.mean(o * o, axis=-1, keepdims=True) + RMS_EPS)) * fg_ref[...]
        o_ref[rows, :] = o


def _ffn(x, mods, row_of_batch, g, wi, wo, mi, final_g=None):
    b, t, d = x.shape
    tm = min(TM_FFN, t)
    final = final_g is not None
    in_specs = [
        pl.BlockSpec((None, tm, d), lambda bi, i: (bi, i, 0)),
        pl.BlockSpec((None, N_MOD, d), lambda bi, i: (row_of_batch(bi), 0, 0)),
        _const_spec((1, d)),
        _const_spec(wi.shape),
        _const_spec(wo.shape),
    ]
    args = [x, mods, g.reshape(1, d), wi, wo]
    if final:
        in_specs.append(_const_spec((1, d)))
        args.append(final_g.reshape(1, d))
    return pl.pallas_call(
        functools.partial(_ffn_kernel, mi=mi, final=final),
        grid=(b, t // tm),
        in_specs=in_specs,
        out_specs=pl.BlockSpec((None, tm, d), lambda bi, i: (bi, i, 0)),
        out_shape=jax.ShapeDtypeStruct((b, t, d), F32),
        compiler_params=_cparams("parallel", "parallel"),
        name="ffn_final" if final else "ffn",
    )(*args)


def _swap16(x):
    lane = lax.broadcasted_iota(jnp.int32, x.shape, 1)
    return jnp.where((lane & 16) == 0, pltpu.roll(x, LANES - 16, 1), pltpu.roll(x, 16, 1))


def _dup_halves(x):
    lane = lax.broadcasted_iota(jnp.int32, x.shape, 1)
    r = pltpu.roll(x, LANES // 2, 1)
    lo = lane < LANES // 2
    return jnp.where(lo, x, r), jnp.where(lo, r, x)


def _proj_kernel(x_ref, mod_ref, g_ref, w_ref, wz_ref, w2_ref, b2_ref, cos_ref, sin_ref,
                 qa_ref, kv_ref, qb_ref, kb_ref, vb_ref, rb_ref, la_ref):
    h = _modulate(x_ref[...], g_ref[...], mod_ref[3:4, :], mod_ref[4:5, :]).astype(BF16)
    z = jnp.dot(h, w_ref[...], preferred_element_type=F32)
    cos = cos_ref[...]
    sin = sin_ref[...]

    def rope(v):
        return v * cos + _swap16(v) * sin

    scale = A_HEAD_DIM ** -0.5
    for p in range(A_Q // LANES):
        q = z[:, p * LANES:(p + 1) * LANES]
        qa_ref[:, p * LANES:(p + 1) * LANES] = (rope(q) * scale).astype(BF16)
    k0, k1 = _dup_halves(rope(z[:, A_Q:A_Q + A_KV]))
    v0, v1 = _dup_halves(z[:, A_Q + A_KV:A_Q + 2 * A_KV])
    for j, piece in enumerate((k0, k1, v0, v1)):
        kv_ref[:, j * LANES:(j + 1) * LANES] = piece.astype(BF16)
    o = A_Q + 2 * A_KV
    qb_ref[...] = z[:, o:o + B_QK]
    kb_ref[...] = z[:, o + B_QK:o + 2 * B_QK]
    vb_ref[...] = z[:, o + 2 * B_QK:o + 2 * B_QK + B_V].astype(BF16)
    rb_ref[...] = z[:, o + 2 * B_QK + B_V:o + 2 * B_QK + 2 * B_V]
    zg = jnp.dot(h, wz_ref[...], preferred_element_type=F32).astype(BF16)
    pre = jnp.dot(zg, w2_ref[...], preferred_element_type=F32) + b2_ref[...]
    la_ref[...] = (jnp.minimum(pre, 0.0) - jnp.log1p(jnp.exp(-jnp.abs(pre)))) / B_GATE_NORM


def _proj(x, mods, row_of_batch, g, w_main, w_zg, w2, b2, cos, sin):
    b, t, d = x.shape
    tm = min(TM_PROJ, t)
    widths = [(A_Q, BF16), (KV_DUP, BF16), (B_QK, F32), (B_QK, F32),
              (B_V, BF16), (B_V, F32), (2 * B_QK, F32)]
    return pl.pallas_call(
        _proj_kernel,
        grid=(b, t // tm),
        in_specs=[
            pl.BlockSpec((None, tm, d), lambda bi, i: (bi, i, 0)),
            pl.BlockSpec((None, N_MOD, d), lambda bi, i: (row_of_batch(bi), 0, 0)),
            _const_spec((1, d)),
            _const_spec(w_main.shape),
            _const_spec(w_zg.shape),
            _const_spec(w2.shape),
            _const_spec(b2.shape),
            pl.BlockSpec((tm, LANES), lambda bi, i: (i, 0)),
            pl.BlockSpec((tm, LANES), lambda bi, i: (i, 0)),
        ],
        out_specs=[pl.BlockSpec((None, tm, w), lambda bi, i: (bi, i, 0)) for w, _ in widths],
        out_shape=[jax.ShapeDtypeStruct((b, t, w), dt) for w, dt in widths],
        compiler_params=_cparams("parallel", "parallel"),
        name="mixer_proj",
    )(x, mods, g.reshape(1, d), w_main, w_zg, w2, b2, cos, sin)


def _attn_kernel(sink_ref, q_ref, kvp_ref, kvc_ref, kvn_ref, ckv_ref, o_ref, *, nb):
    n = pl.program_id(1)
    blk = WINDOW
    row = lax.broadcasted_iota(jnp.int32, (blk, blk), 0)
    col = lax.broadcasted_iota(jnp.int32, (blk, blk), 1)
    ninf = jnp.float32(-jnp.inf)
    bias_prev = jnp.where((col >= row) & (n > 0), 0.0, ninf)
    bias_next = jnp.where((col <= row) & (n < nb - 1), 0.0, ninf)
    lane = lax.broadcasted_iota(jnp.int32, (blk, LANES), 1)
    lo = lane < LANES // 2
    rep = A_HEADS // A_KV_HEADS
    kd, vd = [], []
    for g in range(A_KV_HEADS):
        ks = slice(g * LANES, (g + 1) * LANES)
        vs = slice((A_KV_HEADS + g) * LANES, (A_KV_HEADS + g + 1) * LANES)
        kd.append(jnp.concatenate([r[:, ks] for r in (kvp_ref, kvc_ref, kvn_ref, ckv_ref)], axis=0))
        vd.append(jnp.concatenate([r[:, vs] for r in (kvp_ref, kvc_ref, kvn_ref, ckv_ref)], axis=0))
    nkb = kd[0].shape[0] // blk

    def scores(p):
        pair = q_ref[:, p * LANES:(p + 1) * LANES]
        zero = jnp.zeros_like(pair)
        lhs = jnp.concatenate([jnp.where(lo, pair, zero), jnp.where(lo, zero, pair)], axis=0)
        return lax.dot_general(lhs, kd[(2 * p) // rep], (((1,), (1,)), ((), ())), preferred_element_type=F32)

    def softmax(p, s):
        ps, dens = [], []
        for r in range(2):
            sk = sink_ref[2 * p + r]
            cb = [s[r * blk:(r + 1) * blk, j * blk:(j + 1) * blk] for j in range(nkb)]
            cb[0] = cb[0] + bias_prev
            cb[2] = cb[2] + bias_next
            mx = cb[0]
            for c in cb[1:]:
                mx = jnp.maximum(mx, c)
            m = jnp.maximum(jnp.max(mx, axis=-1, keepdims=True), sk)
            es = [jnp.exp(c - m) for c in cb]
            tot = es[0]
            for e in es[1:]:
                tot = tot + e
            dens.append(jnp.sum(tot, axis=-1, keepdims=True) + jnp.exp(sk - m))
            ps.append(jnp.concatenate([e.astype(BF16) for e in es], axis=1))
        return jnp.concatenate(ps, axis=0), dens

    def weighted_values(p, probs, dens):
        o = jnp.dot(probs, vd[(2 * p) // rep], preferred_element_type=F32)
        out = jnp.where(lo, o[:blk, :] / dens[0], o[blk:, :] / dens[1])
        o_ref[:, p * LANES:(p + 1) * LANES] = out.astype(BF16)

    npairs = A_HEADS // 2
    s_next = scores(0)
    for p in range(npairs):
        s_cur = s_next
        if p + 1 < npairs:
            s_next = scores(p + 1)
        probs, dens = softmax(p, s_cur)
        weighted_values(p, probs, dens)


def _attention(sink, qa, kv, ckv):
    b, t, _ = qa.shape
    nb = t // WINDOW
    lctx = ckv.shape[1]

    def blk(off):
        return pl.BlockSpec((None, WINDOW, KV_DUP), lambda bi, n: (bi, jnp.clip(n + off, 0, nb - 1), 0))

    ctx_spec = pl.BlockSpec((None, lctx, KV_DUP), lambda bi, n: (bi, 0, 0))
    return pl.pallas_call(
        functools.partial(_attn_kernel, nb=nb),
        grid=(b, nb),
        in_specs=[
            pl.BlockSpec(memory_space=pltpu.SMEM),
            pl.BlockSpec((None, WINDOW, A_Q), lambda bi, n: (bi, n, 0)),
            blk(-1), blk(0), blk(1), ctx_spec,
        ],
        out_specs=pl.BlockSpec((None, WINDOW, A_Q), lambda bi, n: (bi, n, 0)),
        out_shape=jax.ShapeDtypeStruct((b, t, A_Q), BF16),
        compiler_params=_cparams("parallel", "parallel"),
        name="window_attn",
    )(sink, qa, kv, kv, kv, ckv)


B_PAIRS = B_HEADS // 2
PAIR_QK = 2 * B_DK
PAIR_V = 2 * B_DV


class _GlaMasks:
    def __init__(self, reverse):
        c = B_CHUNK
        i = lax.broadcasted_iota(jnp.int32, (c, c), 0)
        j = lax.broadcasted_iota(jnp.int32, (c, c), 1)
        self.tri = jnp.where((j >= i) if reverse else (j <= i), 1.0, 0.0).astype(BF16)
        lane_qk = lax.broadcasted_iota(jnp.int32, (c, B_QK), 1)
        self.k_head = [(lane_qk >= h * B_DK) & (lane_qk < (h + 1) * B_DK) for h in range(B_HEADS)]
        lane_v = lax.broadcasted_iota(jnp.int32, (c, B_V), 1)
        self.v_head = [(lane_v >= h * B_DV) & (lane_v < (h + 1) * B_DV) for h in range(B_HEADS)]
        ai = lax.broadcasted_iota(jnp.int32, (c, B_HEADS * c), 0)
        aj = lax.broadcasted_iota(jnp.int32, (c, B_HEADS * c), 1) % c
        self.causal = (aj >= ai) if reverse else (aj <= ai)
        sr = lax.broadcasted_iota(jnp.int32, (PAIR_V, PAIR_QK), 0) // B_DV
        sl = lax.broadcasted_iota(jnp.int32, (PAIR_V, PAIR_QK), 1) // B_DK
        self.state_diag = sr == sl


def _gla_block(q_ref, k_ref, v_ref, la_ref, st, mk, reverse, want_out):
    c = B_CHUNK
    nc = k_ref.shape[0] // c
    order = list(reversed(range(nc))) if reverse else list(range(nc))
    rows = {ci: slice(ci * c, (ci + 1) * c) for ci in order}
    tb = (((1,), (1,)), ((), ()))
    ta = (((0,), (0,)), ((), ()))

    g = {}
    for ci in order:
        la = la_ref[rows[ci], :]
        hi = la.astype(BF16)
        lo = (la - hi.astype(F32)).astype(BF16)
        g[ci] = jnp.dot(mk.tri, hi, preferred_element_type=F32) + jnp.dot(mk.tri, lo, preferred_element_type=F32)

    dec, k_end, q_t, kstack = {}, {}, {}, {}
    for ci in order:
        k = k_ref[rows[ci], :]
        g_last = g[ci][0:1, :] if reverse else g[ci][c - 1:c, :]
        dec[ci] = jnp.exp(g_last)
        k_end[ci] = (k * jnp.exp(g_last - g[ci])).astype(BF16)
        if want_out:
            q_t[ci] = ((q_ref[rows[ci], :] * (B_DK ** -0.5)) * jnp.exp(g[ci])).astype(BF16)
            k_t = k * jnp.exp(-g[ci])
            kstack[ci] = jnp.concatenate([jnp.where(m, k_t, 0.0) for m in mk.k_head], axis=0).astype(BF16)

    ds = {}
    for ci in order:
        v = v_ref[rows[ci], :]
        ds[ci] = [jnp.where(mk.state_diag,
                            lax.dot_general(v[:, p * PAIR_V:(p + 1) * PAIR_V],
                                            k_end[ci][:, p * PAIR_QK:(p + 1) * PAIR_QK], ta,
                                            preferred_element_type=F32), 0.0) for p in range(B_PAIRS)]

    st_prev = {}
    for ci in order:
        st_prev[ci] = st
        st = [st[p] * dec[ci][:, p * PAIR_QK:(p + 1) * PAIR_QK] + ds[ci][p] for p in range(B_PAIRS)]
    if not want_out:
        return None, st

    att = {}
    for ci in order:
        a = lax.dot_general(q_t[ci], kstack[ci], tb, preferred_element_type=F32)
        att[ci] = jnp.where(mk.causal, a, 0.0).astype(BF16)
    o = {}
    for ci in order:
        v = v_ref[rows[ci], :]
        vstack = jnp.concatenate([jnp.where(m, v, jnp.zeros_like(v)) for m in mk.v_head], axis=0)
        o[ci] = jnp.dot(att[ci], vstack, preferred_element_type=F32) + jnp.concatenate(
            [lax.dot_general(q_t[ci][:, p * PAIR_QK:(p + 1) * PAIR_QK], st_prev[ci][p].astype(BF16), tb,
                             preferred_element_type=F32) for p in range(B_PAIRS)], axis=1)
    return o, st


def _gla_ctx_kernel(k_ref, v_ref, la_ref, st_ref, *, reverse):
    mk = _GlaMasks(reverse)
    st = [jnp.zeros((PAIR_V, PAIR_QK), F32) for _ in range(B_PAIRS)]
    _, st = _gla_block(None, k_ref, v_ref, la_ref, st, mk, reverse, False)
    for p in range(B_PAIRS):
        st_ref[p] = st[p]


def _gla_ctx_state(kb, vb, la, reverse):
    b, l, _ = kb.shape
    d = 1 if reverse else 0
    return pl.pallas_call(
        functools.partial(_gla_ctx_kernel, reverse=reverse),
        grid=(b,),
        in_specs=[
            pl.BlockSpec((None, l, B_QK), lambda bi: (bi, 0, 0)),
            pl.BlockSpec((None, l, B_V), lambda bi: (bi, 0, 0)),
            pl.BlockSpec((None, l, B_QK), lambda bi: (bi, 0, d)),
        ],
        out_specs=pl.BlockSpec((None, B_PAIRS, PAIR_V, PAIR_QK), lambda bi: (bi, 0, 0, 0)),
        out_shape=jax.ShapeDtypeStruct((b, B_PAIRS, PAIR_V, PAIR_QK), F32),
        compiler_params=_cparams("parallel"),
        name="gla_ctx_state",
    )(kb, vb, la)


def _gla_kernel(q_ref, k_ref, v_ref, la_ref, st0_ref, *rest, reverse):
    if reverse:
        of_ref, r_ref, gg_ref, o_ref, st_ref = rest
    else:
        o_ref, st_ref = rest

    @pl.when(pl.program_id(1) == 0)
    def _():
        st_ref[...] = st0_ref[...]

    mk = _GlaMasks(reverse)
    outs_by_chunk, st = _gla_block(q_ref, k_ref, v_ref, la_ref, [st_ref[p] for p in range(B_PAIRS)],
                                   mk, reverse, True)
    for ci, o in outs_by_chunk.items():
        rows = slice(ci * B_CHUNK, (ci + 1) * B_CHUNK)
        if reverse:
            o = o + of_ref[rows, :]
            outs = []
            for h in range(B_HEADS):
                oh = o[:, h * B_DV:(h + 1) * B_DV]
                oh = (oh * lax.rsqrt(jnp.mean(oh * oh, axis=-1, keepdims=True) + RMS_EPS)) * gg_ref[...]
                outs.append(oh * _silu(r_ref[rows, h * B_DV:(h + 1) * B_DV]))
            o_ref[rows, :] = jnp.concatenate(outs, axis=1).astype(BF16)
        else:
            o_ref[rows, :] = o
    for p in range(B_PAIRS):
        st_ref[p] = st[p]


def _gla(qb, kb, vb, la, st0, reverse, o_fwd=None, rb=None, gla_g=None):
    b, t, _ = qb.shape
    tb = TB_GLA
    nt = t // tb
    d = 1 if reverse else 0

    def tok(w, lane_blk=0):
        if reverse:
            return pl.BlockSpec((None, tb, w), lambda bi, j: (bi, nt - 1 - j, lane_blk))
        return pl.BlockSpec((None, tb, w), lambda bi, j: (bi, j, lane_blk))

    in_specs = [tok(B_QK), tok(B_QK), tok(B_V), tok(B_QK, d),
                pl.BlockSpec((None, B_PAIRS, PAIR_V, PAIR_QK), lambda bi, j: (bi, 0, 0, 0))]
    args = [qb, kb, vb, la, st0]
    if reverse:
        in_specs += [tok(B_V), tok(B_V), _const_spec((1, B_DV))]
        args += [o_fwd, rb, gla_g.reshape(1, B_DV)]
    return pl.pallas_call(
        functools.partial(_gla_kernel, reverse=reverse),
        grid=(b, nt),
        in_specs=in_specs,
        out_specs=tok(B_V),
        out_shape=jax.ShapeDtypeStruct((b, t, B_V), BF16 if reverse else F32),
        scratch_shapes=[pltpu.VMEM((B_PAIRS, PAIR_V, PAIR_QK), F32)],
        compiler_params=_cparams("parallel", "arbitrary"),
        name="gla_bwd" if reverse else "gla_fwd",
    )(*args)


def _mix_out_kernel(x_ref, mod_ref, oa_ref, ob_ref, w_ref, o_ref):
    y = (jnp.dot(oa_ref[...], w_ref[:A_Q, :], preferred_element_type=F32)
         + jnp.dot(ob_ref[...], w_ref[A_Q:, :], preferred_element_type=F32))
    o_ref[...] = x_ref[...] + mod_ref[5:6, :] * y


def _mix_out(x, mods, oa, ob, w_out):
    b, t, d = x.shape
    tm = TM_OUT

    def tok(w):
        return pl.BlockSpec((None, tm, w), lambda bi, i: (bi, i, 0))

    return pl.pallas_call(
        _mix_out_kernel,
        grid=(b, t // tm),
        in_specs=[tok(d), pl.BlockSpec((None, N_MOD, d), lambda bi, i: (bi, 0, 0)), tok(A_Q), tok(B_V),
                  _const_spec(w_out.shape)],
        out_specs=tok(d),
        out_shape=jax.ShapeDtypeStruct((b, t, d), F32),
        compiler_params=_cparams("parallel", "parallel"),
        name="mixer_out",
    )(x, mods, oa, ob, w_out)


def _shift_rows(e, d):
    n = e.shape[0]
    return pltpu.roll(e, (-d) % n, 0)


def _pool_kernel(x_ref, xp_ref, xn_ref, mod_ref, g_ref, w_ref, ps_ref, o_ref, *, seq):
    i = pl.program_id(1)
    tm = x_ref.shape[0]
    g = g_ref[...]
    shift = mod_ref[3:4, :]
    scale = mod_ref[4:5, :]
    x = x_ref[...]
    h = _modulate(x, g, shift, scale)
    hp = jnp.where(i > 0, _modulate(xp_ref[...], g, shift, scale), 0.0)
    hn = jnp.where(i < pl.num_programs(1) - 1, _modulate(xn_ref[...], g, shift, scale), 0.0)
    t = i * tm + lax.broadcasted_iota(jnp.int32, (tm, 1), 0)
    ys = []
    for gi, w in enumerate(POOL_WINDOWS):
        ls = slice(gi * POOL_GROUP, (gi + 1) * POOL_GROUP)
        hg = h[:, ls]
        p = jnp.concatenate([hp[:, ls], hg, hn[:, ls]], axis=0)
        p = _shift_rows(p, -1) + p
        step = 1
        while 2 * step < w:
            p = _shift_rows(p, -step) + _shift_rows(p, step)
            step *= 2
        total = p[HALO:HALO + tm, :]
        cnt = (jnp.minimum(t + w // 2, seq) - jnp.maximum(t - w // 2, 0)).astype(F32)
        pooled = (total / cnt - hg).astype(BF16)
        ys.append(jnp.dot(pooled, w_ref[gi], preferred_element_type=F32))
    y = jnp.concatenate(ys, axis=1) * ps_ref[...]
    o_ref[...] = x + mod_ref[5:6, :] * y


def _pool(x, mods, g, w_pool, pool_scale):
    b, t, d = x.shape
    tm = TM_POOL
    hb = tm // HALO
    last = t // HALO - 1
    return pl.pallas_call(
        functools.partial(_pool_kernel, seq=t),
        grid=(b, t // tm),
        in_specs=[
            pl.BlockSpec((None, tm, d), lambda bi, i: (bi, i, 0)),
            pl.BlockSpec((None, HALO, d), lambda bi, i: (bi, jnp.maximum(i * hb - 1, 0), 0)),
            pl.BlockSpec((None, HALO, d), lambda bi, i: (bi, jnp.minimum((i + 1) * hb, last), 0)),
            pl.BlockSpec((None, N_MOD, d), lambda bi, i: (bi, 0, 0)),
            _const_spec((1, d)),
            _const_spec(w_pool.shape),
            _const_spec((1, d)),
        ],
        out_specs=pl.BlockSpec((None, tm, d), lambda bi, i: (bi, i, 0)),
        out_shape=jax.ShapeDtypeStruct((b, t, d), F32),
        compiler_params=_cparams("parallel", "parallel"),
        name="pool_mixer",
    )(x, x, x, mods, g.reshape(1, d), w_pool, pool_scale.reshape(1, d))


def _rope_tables(t):
    n = A_HEAD_DIM // 4
    freqs = ROPE_BASE ** (-jnp.arange(n, dtype=F32) / n)
    pos = jnp.arange(t)
    rows = (pos // GRID_W).astype(F32)
    cols = (pos % GRID_W).astype(F32)
    ang_r = rows[:, None] * freqs
    ang_c = cols[:, None] * freqs
    cos = jnp.concatenate([jnp.cos(ang_r), jnp.cos(ang_r), jnp.cos(ang_c), jnp.cos(ang_c)], axis=1)
    sin = jnp.concatenate([-jnp.sin(ang_r), jnp.sin(ang_r), -jnp.sin(ang_c), jnp.sin(ang_c)], axis=1)
    return jnp.tile(cos, (1, 2)), jnp.tile(sin, (1, 2))


def kernel(x, c, ctx, c_ctx, w_mod, b_mod, norm_g, ffn1_wi, ffn1_wo, ffn2_wi, ffn2_wo,
           w_in, w_a2_f, b_a_f, w_a2_b, b_a_b, sink, gla_g, w_out, w_pool, pool_scale, final_g):
    b, t, d = x.shape
    lctx = ctx.shape[1]
    ctx_row = b

    cond = jnp.zeros((COND_ROWS, d), F32).at[:b].set(c).at[ctx_row].set(c_ctx)
    mods = _adaln(cond, w_mod, b_mod).reshape(w_mod.shape[0], COND_ROWS, N_MOD, d)
    by_batch = lambda bi: bi
    ctx_rows = lambda bi: ctx_row

    depth = w_mod.shape[0]
    wi1 = [ffn1_wi[l].astype(BF16) for l in range(depth)]
    wo1 = [ffn1_wo[l].astype(BF16) for l in range(depth)]
    wi2 = [ffn2_wi[l].astype(BF16) for l in range(depth)]
    wo2 = [ffn2_wo[l].astype(BF16) for l in range(depth)]

    x = _ffn(x, mods[0], by_batch, norm_g[0, 0], wi1[0], wo1[0], 0)
    ctx = _ffn(ctx, mods[0], ctx_rows, norm_g[0, 0], wi1[0], wo1[0], 0)

    w_main = w_in[0][:, :PROJ_MAIN].astype(BF16)
    w_zg = jnp.zeros((d, LANES), F32).at[:, :2 * B_GATE_RANK].set(w_in[0][:, PROJ_MAIN:]).astype(BF16)
    w2 = jnp.zeros((LANES, 2 * B_QK), F32)
    w2 = w2.at[:B_GATE_RANK, :B_QK].set(w_a2_f[0]).at[B_GATE_RANK:2 * B_GATE_RANK, B_QK:].set(w_a2_b[0]).astype(BF16)
    b2 = jnp.concatenate([b_a_f[0], b_a_b[0]]).reshape(1, 2 * B_QK)
    cos, sin = _rope_tables(t)
    qa, kv, qb, kb, vb, rb, la = _proj(x, mods[0], by_batch, norm_g[0, 1], w_main, w_zg, w2, b2, cos, sin)
    _, ckv, _, ckb, cvb, _, cla = _proj(ctx, mods[0], ctx_rows, norm_g[0, 1], w_main, w_zg, w2, b2,
                                        jnp.ones((lctx, LANES), F32), jnp.zeros((lctx, LANES), F32))

    o_a = _attention(sink[0], qa, kv, ckv)
    st_f = _gla_ctx_state(ckb, cvb, cla, False)
    st_b = _gla_ctx_state(ckb, cvb, cla, True)
    o_f = _gla(qb, kb, vb, la, st_f, False)
    o_g = _gla(qb, kb, vb, la, st_b, True, o_f, rb, gla_g[0])
    x = _mix_out(x, mods[0], o_a, o_g, w_out[0].astype(BF16))
    x = _ffn(x, mods[0], by_batch, norm_g[0, 2], wi2[0], wo2[0], 6)

    x = _ffn(x, mods[1], by_batch, norm_g[1, 0], wi1[1], wo1[1], 0)
    x = _pool(x, mods[1], norm_g[1, 1], w_pool[0].astype(BF16), pool_scale[0])
    x = _ffn(x, mods[1], by_batch, norm_g[1, 2], wi2[1], wo2[1], 6, final_g=final_g)
    return x
```

```python
import functools

import jax
import jax.numpy as jnp
import numpy as np
from jax import lax
from jax.experimental import pallas as pl
from jax.experimental.pallas import tpu as pltpu

F32 = jnp.float32
BF16 = jnp.bfloat16

D_MODEL = 1024
N_MOD = 9
D_FF = 2816
RMS_EPS = 1e-6
GRID_W = 64
A_HEADS = 8
A_KV_HEADS = 2
A_HEAD_DIM = 64
WINDOW = 128
ROPE_BASE = 10000.0
B_HEADS = 4
B_DK = 64
B_DV = 128
B_GATE_RANK = 16
B_GATE_NORM = 16.0
B_CHUNK = 64
POOL_WINDOWS = (2, 4, 8, 16)
POOL_GROUP = D_MODEL // len(POOL_WINDOWS)
A_Q = A_HEADS * A_HEAD_DIM
A_KV = A_KV_HEADS * A_HEAD_DIM
B_QK = B_HEADS * B_DK
B_V = B_HEADS * B_DV
PROJ_MAIN = A_Q + 2 * A_KV + 2 * B_QK + 2 * B_V
KV_DUP = 4 * A_KV

LANES = 128
COND_ROWS = 8
VMEM_LIMIT = 56 * 1024 * 1024

TM_FFN = 512
SUB_FFN = 256
TM_PROJ = 256
TM_POOL = 256
TB_GLA = 512
HALO = 8


def _cparams(*sem):
    return pltpu.CompilerParams(dimension_semantics=sem, vmem_limit_bytes=VMEM_LIMIT)


def _const_spec(shape):
    nd = len(shape)
    return pl.BlockSpec(shape, lambda *_: (0,) * nd, pipeline_mode=pl.Buffered(1))


def _modulate(x, g, shift, scale):
    y = x * lax.rsqrt(jnp.mean(x * x, axis=-1, keepdims=True) + RMS_EPS)
    return (y * g) * (1.0 + scale) + shift


def _silu(a):
    return a * jax.nn.sigmoid(a)


def _adaln_kernel(c_ref, w_ref, b_ref, o_ref):
    s = _silu(c_ref[...]).astype(BF16)
    o_ref[...] = jnp.dot(s, w_ref[...].astype(BF16), preferred_element_type=F32) + b_ref[...]


def _adaln(cond, w_mod, b_mod):
    depth, d, n = w_mod.shape
    tn = 1024
    return pl.pallas_call(
        _adaln_kernel,
        grid=(depth, n // tn),
        in_specs=[
            pl.BlockSpec((COND_ROWS, d), lambda l, j: (0, 0)),
            pl.BlockSpec((None, d, tn), lambda l, j: (l, 0, j)),
            pl.BlockSpec((None, 1, tn), lambda l, j: (l, 0, j)),
        ],
        out_specs=pl.BlockSpec((None, COND_ROWS, tn), lambda l, j: (l, 0, j)),
        out_shape=jax.ShapeDtypeStruct((depth, COND_ROWS, n), F32),
        compiler_params=_cparams("parallel", "parallel"),
        name="adaln",
    )(cond, w_mod, b_mod.reshape(depth, 1, n))


FFN_WCHUNK = 256
FFN_NW = 2 * D_FF // FFN_WCHUNK
FFN_WO_ROWS = D_FF // FFN_NW
FFN_NCH = D_FF // FFN_WCHUNK


def _ffn_tile(x_ref, o_ref, mod_ref, g_ref, wi_s, wo_s, mi, mix, fg_ref):
    sub = min(SUB_FFN, x_ref.shape[0])
    for s in range(x_ref.shape[0] // sub):
        rows = slice(s * sub, (s + 1) * sub)
        x = x_ref[rows, :]
        if mix is not None:
            oa_ref, ob_ref, w_ref = mix
            y0 = (jnp.dot(oa_ref[rows, :], w_ref[:A_Q, :], preferred_element_type=F32)
                  + jnp.dot(ob_ref[rows, :], w_ref[A_Q:, :], preferred_element_type=F32))
            x = x + mod_ref[5:6, :] * y0
        h = _modulate(x, g_ref[...], mod_ref[mi:mi + 1, :], mod_ref[mi + 1:mi + 2, :]).astype(BF16)
        acts = []
        for c in range(FFN_NCH):
            a = jnp.dot(h, wi_s[c], preferred_element_type=F32)
            u = jnp.dot(h, wi_s[FFN_NCH + c], preferred_element_type=F32)
            acts.append((_silu(a) * u).astype(BF16))
        y = jnp.dot(jnp.concatenate(acts, axis=1), wo_s[...], preferred_element_type=F32)
        o = x + (0.5 * mod_ref[mi + 2:mi + 3, :]) * y
        if fg_ref is not None:
            o = (o * lax.rsqrt(jnp.mean(o * o, axis=-1, keepdims=True) + RMS_EPS)) * fg_ref[...]
        o_ref[rows, :] = o


def _ffn_kernel(*refs, mi, nx, has_ctx, has_mix, final):
    refs = list(refs)
    x_ref = refs.pop(0)
    c_ref = refs.pop(0) if has_ctx else None
    mod_ref, g_ref, wi_ref, wo_ref = refs[:4]
    refs = refs[4:]
    mix = tuple(refs[:3]) if has_mix else None
    refs = refs[3:] if has_mix else refs
    fg_ref = refs.pop(0) if final else None
    o_ref, wi_s, wo_s = refs
    j = pl.program_id(0)

    @pl.when(j < FFN_NW)
    def _():
        wi_s[j] = wi_ref[...].astype(BF16)
        wo_s[pl.ds(pl.multiple_of(j * FFN_WO_ROWS, FFN_WO_ROWS), FFN_WO_ROWS), :] = wo_ref[...].astype(BF16)

    @pl.when((j >= FFN_NW) & (j < FFN_NW + nx))
    def _():
        _ffn_tile(x_ref, o_ref, mod_ref, g_ref, wi_s, wo_s, mi, mix, fg_ref)

    if has_ctx:
        @pl.when(j >= FFN_NW + nx)
        def _():
            _ffn_tile(c_ref, o_ref, mod_ref, g_ref, wi_s, wo_s, mi, None, fg_ref)


def _ffn(x, n_rows, tok_per_batch, mods, g, wi, wo, layer, mi, *, ctx=None, ctx_row=None, mix=None, final_g=None):
    d = x.shape[1]
    tm = TM_FFN
    nx = n_rows // tm
    nc = 0 if ctx is None else ctx.shape[0] // tm
    tiles_per_batch = tok_per_batch // tm

    def tile(j):
        return jnp.clip(j - FFN_NW, 0, nx + nc - 1)

    def x_tile(j):
        return jnp.minimum(tile(j), nx - 1)

    def mod_row(j):
        i = tile(j)
        return i // tiles_per_batch if ctx is None else jnp.where(i < nx, i // tiles_per_batch, ctx_row)

    def wchunk(j):
        return jnp.minimum(j, FFN_NW - 1)

    in_specs = [pl.BlockSpec((tm, d), lambda j: (x_tile(j), 0))]
    args = [x]
    if ctx is not None:
        in_specs.append(pl.BlockSpec((tm, d), lambda j: (jnp.clip(j - FFN_NW - nx, 0, nc - 1), 0)))
        args.append(ctx)
    in_specs += [
        pl.BlockSpec((None, N_MOD, d), lambda j: (mod_row(j), 0, 0)),
        _const_spec((1, d)),
        pl.BlockSpec((None, d, FFN_WCHUNK), lambda j: (layer, 0, wchunk(j))),
        pl.BlockSpec((None, FFN_WO_ROWS, d), lambda j: (layer, wchunk(j), 0)),
    ]
    args += [mods, g.reshape(1, d), wi, wo]
    if mix is not None:
        oa, ob, w_out = mix
        in_specs += [pl.BlockSpec((tm, A_Q), lambda j: (x_tile(j), 0)),
                     pl.BlockSpec((tm, B_V), lambda j: (x_tile(j), 0)),
                     _const_spec(w_out.shape)]
        args += [oa, ob, w_out]
    if final_g is not None:
        in_specs.append(_const_spec((1, d)))
        args.append(final_g.reshape(1, d))
    return pl.pallas_call(
        functools.partial(_ffn_kernel, mi=mi, nx=nx, has_ctx=ctx is not None, has_mix=mix is not None,
                          final=final_g is not None),
        grid=(FFN_NW + nx + nc,),
        in_specs=in_specs,
        out_specs=pl.BlockSpec((tm, d), lambda j: (tile(j), 0)),
        out_shape=jax.ShapeDtypeStruct(((nx + nc) * tm, d), F32),
        scratch_shapes=[pltpu.VMEM((FFN_NW, d, FFN_WCHUNK), BF16), pltpu.VMEM((D_FF, d), BF16)],
        compiler_params=_cparams("arbitrary"),
        name="ffn",
    )(*args)


def _swap16(x):
    lane = lax.broadcasted_iota(jnp.int32, x.shape, 1)
    return jnp.where((lane & 16) == 0, pltpu.roll(x, LANES - 16, 1), pltpu.roll(x, 16, 1))


def _dup_halves(x):
    lane = lax.broadcasted_iota(jnp.int32, x.shape, 1)
    r = pltpu.roll(x, LANES // 2, 1)
    lo = lane < LANES // 2
    return jnp.where(lo, x, r), jnp.where(lo, r, x)


def _proj_kernel(x_ref, mod_ref, g_ref, w_ref, wz_ref, w2_ref, b2_ref, cos_ref, sin_ref,
                 qa_ref, kv_ref, qb_ref, kb_ref, vb_ref, rb_ref, la_ref):
    h = _modulate(x_ref[...], g_ref[...], mod_ref[3:4, :], mod_ref[4:5, :]).astype(BF16)
    z = jnp.dot(h, w_ref[...], preferred_element_type=F32)
    cos = cos_ref[...]
    sin = sin_ref[...]

    def rope(v):
        return v * cos + _swap16(v) * sin

    scale = A_HEAD_DIM ** -0.5
    for p in range(A_Q // LANES):
        q = z[:, p * LANES:(p + 1) * LANES]
        qa_ref[:, p * LANES:(p + 1) * LANES] = (rope(q) * scale).astype(BF16)
    k0, k1 = _dup_halves(rope(z[:, A_Q:A_Q + A_KV]))
    v0, v1 = _dup_halves(z[:, A_Q + A_KV:A_Q + 2 * A_KV])
    for j, piece in enumerate((k0, k1, v0, v1)):
        kv_ref[:, j * LANES:(j + 1) * LANES] = piece.astype(BF16)
    o = A_Q + 2 * A_KV
    qb_ref[...] = z[:, o:o + B_QK]
    kb_ref[...] = z[:, o + B_QK:o + 2 * B_QK]
    vb_ref[...] = z[:, o + 2 * B_QK:o + 2 * B_QK + B_V].astype(BF16)
    rb_ref[...] = z[:, o + 2 * B_QK + B_V:o + 2 * B_QK + 2 * B_V]
    zg = jnp.dot(h, wz_ref[...], preferred_element_type=F32).astype(BF16)
    pre = jnp.dot(zg, w2_ref[...], preferred_element_type=F32) + b2_ref[...]
    la_ref[...] = (jnp.minimum(pre, 0.0) - jnp.log1p(jnp.exp(-jnp.abs(pre)))) / B_GATE_NORM


def _proj(xc, n_tok, tok_per_batch, ctx_row, mods, g, w_main, w_zg, w2, b2, cos, sin):
    rows, d = xc.shape
    tm = TM_PROJ
    nx = n_tok // tm
    tiles_per_batch = tok_per_batch // tm
    widths = [(A_Q, BF16), (KV_DUP, BF16), (B_QK, F32), (B_QK, F32),
              (B_V, BF16), (B_V, F32), (2 * B_QK, F32)]

    def mod_row(i):
        return jnp.where(i < nx, i // tiles_per_batch, ctx_row)

    def pos_blk(i):
        return jnp.where(i < nx, i % tiles_per_batch, tiles_per_batch)

    return pl.pallas_call(
        _proj_kernel,
        grid=(rows // tm,),
        in_specs=[
            pl.BlockSpec((tm, d), lambda i: (i, 0)),
            pl.BlockSpec((None, N_MOD, d), lambda i: (mod_row(i), 0, 0)),
            _const_spec((1, d)),
            _const_spec(w_main.shape),
            _const_spec(w_zg.shape),
            _const_spec(w2.shape),
            _const_spec(b2.shape),
            pl.BlockSpec((tm, LANES), lambda i: (pos_blk(i), 0)),
            pl.BlockSpec((tm, LANES), lambda i: (pos_blk(i), 0)),
        ],
        out_specs=[pl.BlockSpec((tm, w), lambda i: (i, 0)) for w, _ in widths],
        out_shape=[jax.ShapeDtypeStruct((rows, w), dt) for w, dt in widths],
        compiler_params=_cparams("parallel"),
        name="mixer_proj",
    )(xc, mods, g.reshape(1, d), w_main, w_zg, w2, b2, cos, sin)


def _attn_kernel(sink_ref, q_ref, kvp_ref, kvc_ref, kvn_ref, ckv_ref, o_ref, *, nb):
    n = pl.program_id(1)
    blk = WINDOW
    row = lax.broadcasted_iota(jnp.int32, (blk, blk), 0)
    col = lax.broadcasted_iota(jnp.int32, (blk, blk), 1)
    ninf = jnp.float32(-jnp.inf)
    bias_prev = jnp.where((col >= row) & (n > 0), 0.0, ninf)
    bias_next = jnp.where((col <= row) & (n < nb - 1), 0.0, ninf)
    lane = lax.broadcasted_iota(jnp.int32, (blk, LANES), 1)
    lo = lane < LANES // 2
    rep = A_HEADS // A_KV_HEADS
    kd, vd = [], []
    for g in range(A_KV_HEADS):
        ks = slice(g * LANES, (g + 1) * LANES)
        vs = slice((A_KV_HEADS + g) * LANES, (A_KV_HEADS + g + 1) * LANES)
        kd.append(jnp.concatenate([r[:, ks] for r in (kvp_ref, kvc_ref, kvn_ref, ckv_ref)], axis=0))
        vd.append(jnp.concatenate([r[:, vs] for r in (kvp_ref, kvc_ref, kvn_ref, ckv_ref)], axis=0))
    nkb = kd[0].shape[0] // blk

    def scores(p):
        pair = q_ref[:, p * LANES:(p + 1) * LANES]
        zero = jnp.zeros_like(pair)
        lhs = jnp.concatenate([jnp.where(lo, pair, zero), jnp.where(lo, zero, pair)], axis=0)
        return lax.dot_general(lhs, kd[(2 * p) // rep], (((1,), (1,)), ((), ())), preferred_element_type=F32)

    def softmax(p, s):
        ps, dens = [], []
        for r in range(2):
            sk = sink_ref[2 * p + r]
            cb = [s[r * blk:(r + 1) * blk, j * blk:(j + 1) * blk] for j in range(nkb)]
            cb[0] = cb[0] + bias_prev
            cb[2] = cb[2] + bias_next
            mx = cb[0]
            for c in cb[1:]:
                mx = jnp.maximum(mx, c)
            m = jnp.maximum(jnp.max(mx, axis=-1, keepdims=True), sk)
            es = [jnp.exp(c - m) for c in cb]
            tot = es[0]
            for e in es[1:]:
                tot = tot + e
            dens.append(jnp.sum(tot, axis=-1, keepdims=True) + jnp.exp(sk - m))
            ps.append(jnp.concatenate([e.astype(BF16) for e in es], axis=1))
        return jnp.concatenate(ps, axis=0), dens

    def weighted_values(p, probs, dens):
        o = jnp.dot(probs, vd[(2 * p) // rep], preferred_element_type=F32)
        out = jnp.where(lo, o[:blk, :] / dens[0], o[blk:, :] / dens[1])
        o_ref[:, p * LANES:(p + 1) * LANES] = out.astype(BF16)

    npairs = A_HEADS // 2
    s_next = scores(0)
    for p in range(npairs):
        s_cur = s_next
        if p + 1 < npairs:
            s_next = scores(p + 1)
        probs, dens = softmax(p, s_cur)
        weighted_values(p, probs, dens)


def _attention(sink, qa, kv, b, t, lctx):
    nb = t // WINDOW
    ctx0 = (b * t) // lctx

    def blk(off):
        return pl.BlockSpec((WINDOW, KV_DUP), lambda bi, n: (bi * nb + jnp.clip(n + off, 0, nb - 1), 0))

    return pl.pallas_call(
        functools.partial(_attn_kernel, nb=nb),
        grid=(b, nb),
        in_specs=[
            pl.BlockSpec(memory_space=pltpu.SMEM),
            pl.BlockSpec((WINDOW, A_Q), lambda bi, n: (bi * nb + n, 0)),
            blk(-1), blk(0), blk(1),
            pl.BlockSpec((lctx, KV_DUP), lambda bi, n: (ctx0 + bi, 0)),
        ],
        out_specs=pl.BlockSpec((WINDOW, A_Q), lambda bi, n: (bi * nb + n, 0)),
        out_shape=jax.ShapeDtypeStruct((b * t, A_Q), BF16),
        compiler_params=_cparams("parallel", "parallel"),
        name="window_attn",
    )(sink, qa, kv, kv, kv, kv)


B_PAIRS = B_HEADS // 2
PAIR_QK = 2 * B_DK
PAIR_V = 2 * B_DV


class _GlaMasks:
    def __init__(self, reverse):
        c = B_CHUNK
        i = lax.broadcasted_iota(jnp.int32, (c, c), 0)
        j = lax.broadcasted_iota(jnp.int32, (c, c), 1)
        self.tri = jnp.where((j >= i) if reverse else (j <= i), 1.0, 0.0).astype(BF16)
        lane_qk = lax.broadcasted_iota(jnp.int32, (c, B_QK), 1)
        self.k_head = [(lane_qk >= h * B_DK) & (lane_qk < (h + 1) * B_DK) for h in range(B_HEADS)]
        lane_v = lax.broadcasted_iota(jnp.int32, (c, B_V), 1)
        self.v_head = [(lane_v >= h * B_DV) & (lane_v < (h + 1) * B_DV) for h in range(B_HEADS)]
        ai = lax.broadcasted_iota(jnp.int32, (c, B_HEADS * c), 0)
        aj = lax.broadcasted_iota(jnp.int32, (c, B_HEADS * c), 1) % c
        self.causal = (aj >= ai) if reverse else (aj <= ai)
        sr = lax.broadcasted_iota(jnp.int32, (PAIR_V, PAIR_QK), 0) // B_DV
        sl = lax.broadcasted_iota(jnp.int32, (PAIR_V, PAIR_QK), 1) // B_DK
        self.state_diag = sr == sl


def _gla_block(q_ref, k_ref, v_ref, la_ref, st, mk, reverse, want_out):
    c = B_CHUNK
    nc = k_ref.shape[0] // c
    order = list(reversed(range(nc))) if reverse else list(range(nc))
    rows = {ci: slice(ci * c, (ci + 1) * c) for ci in order}
    tb = (((1,), (1,)), ((), ()))
    ta = (((0,), (0,)), ((), ()))

    g = {}
    for ci in order:
        la = la_ref[rows[ci], :]
        hi = la.astype(BF16)
        lo = (la - hi.astype(F32)).astype(BF16)
        g[ci] = jnp.dot(mk.tri, hi, preferred_element_type=F32) + jnp.dot(mk.tri, lo, preferred_element_type=F32)

    dec, k_end, q_t, kstack = {}, {}, {}, {}
    for ci in order:
        k = k_ref[rows[ci], :]
        g_last = g[ci][0:1, :] if reverse else g[ci][c - 1:c, :]
        dec[ci] = jnp.exp(g_last)
        k_end[ci] = (k * jnp.exp(g_last - g[ci])).astype(BF16)
        if want_out:
            q_t[ci] = ((q_ref[rows[ci], :] * (B_DK ** -0.5)) * jnp.exp(g[ci])).astype(BF16)
            k_t = k * jnp.exp(-g[ci])
            kstack[ci] = jnp.concatenate([jnp.where(m, k_t, 0.0) for m in mk.k_head], axis=0).astype(BF16)

    ds = {}
    for ci in order:
        v = v_ref[rows[ci], :]
        ds[ci] = [jnp.where(mk.state_diag,
                            lax.dot_general(v[:, p * PAIR_V:(p + 1) * PAIR_V],
                                            k_end[ci][:, p * PAIR_QK:(p + 1) * PAIR_QK], ta,
                                            preferred_element_type=F32), 0.0) for p in range(B_PAIRS)]

    st_prev = {}
    for ci in order:
        st_prev[ci] = st
        st = [st[p] * dec[ci][:, p * PAIR_QK:(p + 1) * PAIR_QK] + ds[ci][p] for p in range(B_PAIRS)]
    if not want_out:
        return None, st

    att = {}
    for ci in order:
        a = lax.dot_general(q_t[ci], kstack[ci], tb, preferred_element_type=F32)
        att[ci] = jnp.where(mk.causal, a, 0.0).astype(BF16)
    o = {}
    for ci in order:
        v = v_ref[rows[ci], :]
        vstack = jnp.concatenate([jnp.where(m, v, jnp.zeros_like(v)) for m in mk.v_head], axis=0)
        o[ci] = jnp.dot(att[ci], vstack, preferred_element_type=F32) + jnp.concatenate(
            [lax.dot_general(q_t[ci][:, p * PAIR_QK:(p + 1) * PAIR_QK], st_prev[ci][p].astype(BF16), tb,
                             preferred_element_type=F32) for p in range(B_PAIRS)], axis=1)
    return o, st


def _gla_ctx_kernel(k_ref, v_ref, la_ref, st_ref, *, reverse):
    mk = _GlaMasks(reverse)
    st = [jnp.zeros((PAIR_V, PAIR_QK), F32) for _ in range(B_PAIRS)]
    _, st = _gla_block(None, k_ref, v_ref, la_ref, st, mk, reverse, False)
    for p in range(B_PAIRS):
        st_ref[p] = st[p]


def _gla_ctx_state(kb, vb, la, b, t, l, reverse):
    d = 1 if reverse else 0
    ctx0 = (b * t) // l
    return pl.pallas_call(
        functools.partial(_gla_ctx_kernel, reverse=reverse),
        grid=(b,),
        in_specs=[
            pl.BlockSpec((l, B_QK), lambda bi: (ctx0 + bi, 0)),
            pl.BlockSpec((l, B_V), lambda bi: (ctx0 + bi, 0)),
            pl.BlockSpec((l, B_QK), lambda bi: (ctx0 + bi, d)),
        ],
        out_specs=pl.BlockSpec((None, B_PAIRS, PAIR_V, PAIR_QK), lambda bi: (bi, 0, 0, 0)),
        out_shape=jax.ShapeDtypeStruct((b, B_PAIRS, PAIR_V, PAIR_QK), F32),
        compiler_params=_cparams("parallel"),
        name="gla_ctx_state",
    )(kb, vb, la)


def _gla_kernel(q_ref, k_ref, v_ref, la_ref, st0_ref, *rest, reverse):
    if reverse:
        of_ref, r_ref, gg_ref, o_ref, st_ref = rest
    else:
        o_ref, st_ref = rest

    @pl.when(pl.program_id(1) == 0)
    def _():
        st_ref[...] = st0_ref[...]

    mk = _GlaMasks(reverse)
    outs_by_chunk, st = _gla_block(q_ref, k_ref, v_ref, la_ref, [st_ref[p] for p in range(B_PAIRS)],
                                   mk, reverse, True)
    for ci, o in outs_by_chunk.items():
        rows = slice(ci * B_CHUNK, (ci + 1) * B_CHUNK)
        if reverse:
            o = o + of_ref[rows, :]
            outs = []
            for h in range(B_HEADS):
                oh = o[:, h * B_DV:(h + 1) * B_DV]
                oh = (oh * lax.rsqrt(jnp.mean(oh * oh, axis=-1, keepdims=True) + RMS_EPS)) * gg_ref[...]
                outs.append(oh * _silu(r_ref[rows, h * B_DV:(h + 1) * B_DV]))
            o_ref[rows, :] = jnp.concatenate(outs, axis=1).astype(BF16)
        else:
            o_ref[rows, :] = o
    for p in range(B_PAIRS):
        st_ref[p] = st[p]


def _gla(qb, kb, vb, la, st0, b, t, reverse, o_fwd=None, rb=None, gla_g=None):
    tb = TB_GLA
    nt = t // tb
    d = 1 if reverse else 0

    def tok(w, lane_blk=0):
        if reverse:
            return pl.BlockSpec((tb, w), lambda bi, j: (bi * nt + nt - 1 - j, lane_blk))
        return pl.BlockSpec((tb, w), lambda bi, j: (bi * nt + j, lane_blk))

    in_specs = [tok(B_QK), tok(B_QK), tok(B_V), tok(B_QK, d),
                pl.BlockSpec((None, B_PAIRS, PAIR_V, PAIR_QK), lambda bi, j: (bi, 0, 0, 0))]
    args = [qb, kb, vb, la, st0]
    if reverse:
        in_specs += [tok(B_V), tok(B_V), _const_spec((1, B_DV))]
        args += [o_fwd, rb, gla_g.reshape(1, B_DV)]
    return pl.pallas_call(
        functools.partial(_gla_kernel, reverse=reverse),
        grid=(b, nt),
        in_specs=in_specs,
        out_specs=tok(B_V),
        out_shape=jax.ShapeDtypeStruct((b * t, B_V), BF16 if reverse else F32),
        scratch_shapes=[pltpu.VMEM((B_PAIRS, PAIR_V, PAIR_QK), F32)],
        compiler_params=_cparams("parallel", "arbitrary"),
        name="gla_bwd" if reverse else "gla_fwd",
    )(*args)


def _shift_rows(e, d):
    n = e.shape[0]
    return pltpu.roll(e, (-d) % n, 0)


def _pool_kernel(x_ref, xp_ref, xn_ref, mod_ref, g_ref, w_ref, ps_ref, o_ref, *, seq):
    i = pl.program_id(1)
    tm = x_ref.shape[0]
    g = g_ref[...]
    shift = mod_ref[3:4, :]
    scale = mod_ref[4:5, :]
    x = x_ref[...]
    h = _modulate(x, g, shift, scale)
    hp = jnp.where(i > 0, _modulate(xp_ref[...], g, shift, scale), 0.0)
    hn = jnp.where(i < pl.num_programs(1) - 1, _modulate(xn_ref[...], g, shift, scale), 0.0)
    t = i * tm + lax.broadcasted_iota(jnp.int32, (tm, 1), 0)
    ys = []
    for gi, w in enumerate(POOL_WINDOWS):
        ls = slice(gi * POOL_GROUP, (gi + 1) * POOL_GROUP)
        hg = h[:, ls]
        p = jnp.concatenate([hp[:, ls], hg, hn[:, ls]], axis=0)
        p = _shift_rows(p, -1) + p
        step = 1
        while 2 * step < w:
            p = _shift_rows(p, -step) + _shift_rows(p, step)
            step *= 2
        total = p[HALO:HALO + tm, :]
        cnt = (jnp.minimum(t + w // 2, seq) - jnp.maximum(t - w // 2, 0)).astype(F32)
        pooled = (total / cnt - hg).astype(BF16)
        ys.append(jnp.dot(pooled, w_ref[gi], preferred_element_type=F32))
    y = jnp.concatenate(ys, axis=1) * ps_ref[...]
    o_ref[...] = x + mod_ref[5:6, :] * y


def _pool(x, mods, g, w_pool, pool_scale):
    b, t, d = x.shape
    tm = TM_POOL
    hb = tm // HALO
    last = t // HALO - 1
    return pl.pallas_call(
        functools.partial(_pool_kernel, seq=t),
        grid=(b, t // tm),
        in_specs=[
            pl.BlockSpec((None, tm, d), lambda bi, i: (bi, i, 0)),
            pl.BlockSpec((None, HALO, d), lambda bi, i: (bi, jnp.maximum(i * hb - 1, 0), 0)),
            pl.BlockSpec((None, HALO, d), lambda bi, i: (bi, jnp.minimum((i + 1) * hb, last), 0)),
            pl.BlockSpec((None, N_MOD, d), lambda bi, i: (bi, 0, 0)),
            _const_spec((1, d)),
            _const_spec(w_pool.shape),
            _const_spec((1, d)),
        ],
        out_specs=pl.BlockSpec((None, tm, d), lambda bi, i: (bi, i, 0)),
        out_shape=jax.ShapeDtypeStruct((b, t, d), F32),
        compiler_params=_cparams("parallel", "parallel"),
        name="pool_mixer",
    )(x, x, x, mods, g.reshape(1, d), w_pool, pool_scale.reshape(1, d))


def _rope_tables(t, pad_rows):
    n = A_HEAD_DIM // 4
    freqs = ROPE_BASE ** (-jnp.arange(n, dtype=F32) / n)
    pos = jnp.arange(t)
    rows = (pos // GRID_W).astype(F32)
    cols = (pos % GRID_W).astype(F32)
    ang_r = rows[:, None] * freqs
    ang_c = cols[:, None] * freqs
    cos = jnp.concatenate([jnp.cos(ang_r), jnp.cos(ang_r), jnp.cos(ang_c), jnp.cos(ang_c)], axis=1)
    sin = jnp.concatenate([-jnp.sin(ang_r), jnp.sin(ang_r), -jnp.sin(ang_c), jnp.sin(ang_c)], axis=1)
    cos = jnp.concatenate([jnp.tile(cos, (1, 2)), jnp.ones((pad_rows, LANES), F32)], axis=0)
    sin = jnp.concatenate([jnp.tile(sin, (1, 2)), jnp.zeros((pad_rows, LANES), F32)], axis=0)
    return cos, sin


def kernel(x, c, ctx, c_ctx, w_mod, b_mod, norm_g, ffn1_wi, ffn1_wo, ffn2_wi, ffn2_wo,
           w_in, w_a2_f, b_a_f, w_a2_b, b_a_b, sink, gla_g, w_out, w_pool, pool_scale, final_g):
    b, t, d = x.shape
    lctx = ctx.shape[1]
    n_tok = b * t
    ctx_row = b

    cond = jnp.zeros((COND_ROWS, d), F32).at[:b].set(c).at[ctx_row].set(c_ctx)
    mods = _adaln(cond, w_mod, b_mod).reshape(w_mod.shape[0], COND_ROWS, N_MOD, d)
    x = x.reshape(n_tok, d)

    xc = _ffn(x, n_tok, t, mods[0], norm_g[0, 0], ffn1_wi, ffn1_wo, 0, 0,
              ctx=ctx.reshape(b * lctx, d), ctx_row=ctx_row)

    w_main = w_in[0][:, :PROJ_MAIN].astype(BF16)
    w_zg = jnp.zeros((d, LANES), F32).at[:, :2 * B_GATE_RANK].set(w_in[0][:, PROJ_MAIN:]).astype(BF16)
    w2 = jnp.zeros((LANES, 2 * B_QK), F32)
    w2 = w2.at[:B_GATE_RANK, :B_QK].set(w_a2_f[0]).at[B_GATE_RANK:2 * B_GATE_RANK, B_QK:].set(w_a2_b[0]).astype(BF16)
    b2 = jnp.concatenate([b_a_f[0], b_a_b[0]]).reshape(1, 2 * B_QK)
    cos, sin = _rope_tables(t, TM_PROJ)
    qa, kv, qb, kb, vb, rb, la = _proj(xc, n_tok, t, ctx_row, mods[0], norm_g[0, 1], w_main, w_zg, w2, b2, cos, sin)

    o_a = _attention(sink[0], qa, kv, b, t, lctx)
    st_f = _gla_ctx_state(kb, vb, la, b, t, lctx, False)
    st_b = _gla_ctx_state(kb, vb, la, b, t, lctx, True)
    o_f = _gla(qb, kb, vb, la, st_f, b, t, False)
    o_g = _gla(qb, kb, vb, la, st_b, b, t, True, o_f, rb, gla_g[0])
    x = _ffn(xc, n_tok, t, mods[0], norm_g[0, 2], ffn2_wi, ffn2_wo, 0, 6, mix=(o_a, o_g, w_out[0].astype(BF16)))

    x = _ffn(x, n_tok, t, mods[1], norm_g[1, 0], ffn1_wi, ffn1_wo, 1, 0)
    x = _pool(x.reshape(b, t, d), mods[1], norm_g[1, 1], w_pool[0].astype(BF16), pool_scale[0])
    x = _ffn(x.reshape(n_tok, d), n_tok, t, mods[1], norm_g[1, 2], ffn2_wi, ffn2_wo, 1, 6, final_g=final_g)
    return x.reshape(b, t, d)
```

```python
import functools

import jax
import jax.numpy as jnp
import numpy as np
from jax import lax
from jax.experimental import pallas as pl
from jax.experimental.pallas import tpu as pltpu

F32 = jnp.float32
BF16 = jnp.bfloat16

D_MODEL = 1024
N_MOD = 9
D_FF = 2816
RMS_EPS = 1e-6
GRID_W = 64
A_HEADS = 8
A_KV_HEADS = 2
A_HEAD_DIM = 64
WINDOW = 128
ROPE_BASE = 10000.0
B_HEADS = 4
B_DK = 64
B_DV = 128
B_GATE_RANK = 16
B_GATE_NORM = 16.0
B_CHUNK = 64
POOL_WINDOWS = (2, 4, 8, 16)
POOL_GROUP = D_MODEL // len(POOL_WINDOWS)
A_Q = A_HEADS * A_HEAD_DIM
A_KV = A_KV_HEADS * A_HEAD_DIM
B_QK = B_HEADS * B_DK
B_V = B_HEADS * B_DV
PROJ_MAIN = A_Q + 2 * A_KV + 2 * B_QK + 2 * B_V
KV_DUP = 4 * A_KV

LOG2E = 1.4426950408889634
LANES = 128
COND_ROWS = 8
VMEM_LIMIT = 56 * 1024 * 1024

TM_FFN = 1024
SUB_FFN = 256
TM_PROJ = 1024
SUB_PROJ = 256
TM_POOL = 512
TB_GLA = 512
HALO = 8


def _cparams(*sem):
    return pltpu.CompilerParams(dimension_semantics=sem, vmem_limit_bytes=VMEM_LIMIT)


def _const_spec(shape):
    nd = len(shape)
    return pl.BlockSpec(shape, lambda *_: (0,) * nd, pipeline_mode=pl.Buffered(1))


def _modulate(x, g, shift, scale):
    y = x * lax.rsqrt(jnp.mean(x * x, axis=-1, keepdims=True) + RMS_EPS)
    return (y * g) * (1.0 + scale) + shift


def _silu(a):
    return a * jax.nn.sigmoid(a)


def _adaln_kernel(c_ref, w_ref, b_ref, o_ref):
    s = _silu(c_ref[...]).astype(BF16)
    o_ref[...] = jnp.dot(s, w_ref[...].astype(BF16), preferred_element_type=F32) + b_ref[...]


def _adaln(cond, w_mod, b_mod):
    depth, d, n = w_mod.shape
    tn = 1024
    return pl.pallas_call(
        _adaln_kernel,
        grid=(depth, n // tn),
        in_specs=[
            pl.BlockSpec((COND_ROWS, d), lambda l, j: (0, 0)),
            pl.BlockSpec((None, d, tn), lambda l, j: (l, 0, j)),
            pl.BlockSpec((None, 1, tn), lambda l, j: (l, 0, j)),
        ],
        out_specs=pl.BlockSpec((None, COND_ROWS, tn), lambda l, j: (l, 0, j)),
        out_shape=jax.ShapeDtypeStruct((depth, COND_ROWS, n), F32),
        compiler_params=_cparams("parallel", "parallel"),
        name="adaln",
    )(cond, w_mod, b_mod.reshape(depth, 1, n))


FFN_WCHUNK = 256
FFN_NW = 2 * D_FF // FFN_WCHUNK
FFN_WO_ROWS = D_FF // FFN_NW
FFN_NCH = D_FF // FFN_WCHUNK
FFN_PROLOGUE_AT = 2


def _ffn_tile(x_ref, o_ref, mod_ref, g_ref, wi_s, wo_s, mi, mix, fg_ref):
    sub = min(SUB_FFN, x_ref.shape[0])
    nsub = x_ref.shape[0] // sub

    def prologue(s):
        rows = slice(s * sub, (s + 1) * sub)
        x = x_ref[rows, :]
        if mix is not None:
            oa_ref, ob_ref, w_ref = mix
            y0 = (jnp.dot(oa_ref[rows, :], w_ref[:A_Q, :], preferred_element_type=F32)
                  + jnp.dot(ob_ref[rows, :], w_ref[A_Q:, :], preferred_element_type=F32))
            x = x + mod_ref[5:6, :] * y0
        return x, _modulate(x, g_ref[...], mod_ref[mi:mi + 1, :], mod_ref[mi + 1:mi + 2, :]).astype(BF16)

    def act_chunk(h, c):
        a = jnp.dot(h, wi_s[c], preferred_element_type=F32)
        u = jnp.dot(h, wi_s[FFN_NCH + c], preferred_element_type=F32)
        return (_silu(a) * u).astype(BF16)

    nxt = prologue(0)
    early = []
    for s in range(nsub):
        rows = slice(s * sub, (s + 1) * sub)
        x, h = nxt
        acts, early = early, []
        for c in range(len(acts), FFN_NCH):
            if c == FFN_NCH - 1 and s + 1 < nsub:
                nxt = prologue(s + 1)
                early = [act_chunk(nxt[1], 0)]
            acts.append(act_chunk(h, c))
        y = jnp.dot(jnp.concatenate(acts, axis=1), wo_s[...], preferred_element_type=F32)
        o = x + (0.5 * mod_ref[mi + 2:mi + 3, :]) * y
        if fg_ref is not None:
            o = (o * lax.rsqrt(jnp.mean(o * o, axis=-1, keepdims=True) + RMS_EPS)) * fg_ref[...]
        o_ref[rows, :] = o


def _ffn_kernel(*refs, mi, nx, has_ctx, has_mix, final):
    refs = list(refs)
    x_ref = refs.pop(0)
    c_ref = refs.pop(0) if has_ctx else None
    mod_ref, g_ref, wi_ref, wo_ref = refs[:4]
    refs = refs[4:]
    mix = tuple(refs[:3]) if has_mix else None
    refs = refs[3:] if has_mix else refs
    fg_ref = refs.pop(0) if final else None
    o_ref, wi_s, wo_s = refs
    j = pl.program_id(0)

    @pl.when(j < FFN_NW)
    def _():
        wi_s[j] = wi_ref[...].astype(BF16)
        wo_s[pl.ds(pl.multiple_of(j * FFN_WO_ROWS, FFN_WO_ROWS), FFN_WO_ROWS), :] = wo_ref[...].astype(BF16)

    @pl.when((j >= FFN_NW) & (j < FFN_NW + nx))
    def _():
        _ffn_tile(x_ref, o_ref, mod_ref, g_ref, wi_s, wo_s, mi, mix, fg_ref)

    if has_ctx:
        @pl.when(j >= FFN_NW + nx)
        def _():
            _ffn_tile(c_ref, o_ref, mod_ref, g_ref, wi_s, wo_s, mi, None, fg_ref)


def _ffn(x, n_rows, tok_per_batch, mods, g, wi, wo, layer, mi, *, ctx=None, ctx_row=None, mix=None, final_g=None):
    d = x.shape[1]
    tm = TM_FFN
    nx = n_rows // tm
    nc = 0 if ctx is None else ctx.shape[0] // tm
    tiles_per_batch = tok_per_batch // tm

    def tile(j):
        return jnp.clip(j - FFN_NW, 0, nx + nc - 1)

    def x_tile(j):
        return jnp.minimum(tile(j), nx - 1)

    def mod_row(j):
        i = tile(j)
        return i // tiles_per_batch if ctx is None else jnp.where(i < nx, i // tiles_per_batch, ctx_row)

    def wchunk(j):
        return jnp.minimum(j, FFN_NW - 1)

    in_specs = [pl.BlockSpec((tm, d), lambda j: (x_tile(j), 0))]
    args = [x]
    if ctx is not None:
        in_specs.append(pl.BlockSpec((tm, d), lambda j: (jnp.clip(j - FFN_NW - nx, 0, nc - 1), 0)))
        args.append(ctx)
    in_specs += [
        pl.BlockSpec((None, N_MOD, d), lambda j: (mod_row(j), 0, 0)),
        _const_spec((1, d)),
        pl.BlockSpec((None, d, FFN_WCHUNK), lambda j: (layer, 0, wchunk(j))),
        pl.BlockSpec((None, FFN_WO_ROWS, d), lambda j: (layer, wchunk(j), 0)),
    ]
    args += [mods, g.reshape(1, d), wi, wo]
    if mix is not None:
        oa, ob, w_out = mix
        in_specs += [pl.BlockSpec((tm, A_Q), lambda j: (x_tile(j), 0)),
                     pl.BlockSpec((tm, B_V), lambda j: (x_tile(j), 0)),
                     _const_spec(w_out.shape)]
        args += [oa, ob, w_out]
    if final_g is not None:
        in_specs.append(_const_spec((1, d)))
        args.append(final_g.reshape(1, d))
    return pl.pallas_call(
        functools.partial(_ffn_kernel, mi=mi, nx=nx, has_ctx=ctx is not None, has_mix=mix is not None,
                          final=final_g is not None),
        grid=(FFN_NW + nx + nc,),
        in_specs=in_specs,
        out_specs=pl.BlockSpec((tm, d), lambda j: (tile(j), 0)),
        out_shape=jax.ShapeDtypeStruct(((nx + nc) * tm, d), F32),
        scratch_shapes=[pltpu.VMEM((FFN_NW, d, FFN_WCHUNK), BF16), pltpu.VMEM((D_FF, d), BF16)],
        compiler_params=_cparams("arbitrary"),
        name="ffn",
    )(*args)


def _swap16(x):
    lane = lax.broadcasted_iota(jnp.int32, x.shape, 1)
    return jnp.where((lane & 16) == 0, pltpu.roll(x, LANES - 16, 1), pltpu.roll(x, 16, 1))


def _dup_halves(x):
    lane = lax.broadcasted_iota(jnp.int32, x.shape, 1)
    r = pltpu.roll(x, LANES // 2, 1)
    lo = lane < LANES // 2
    return jnp.where(lo, x, r), jnp.where(lo, r, x)


def _halves_with_ones(x):
    lane = lax.broadcasted_iota(jnp.int32, x.shape, 1)
    r = pltpu.roll(x, LANES // 2, 1)
    lo = lane < LANES // 2
    return jnp.where(lo, x, 1.0), jnp.where(lo, r, 1.0)


def _proj_kernel(x_ref, mod_ref, g_ref, w_ref, wz_ref, w2_ref, b2_ref, cos_ref, sin_ref,
                 qa_ref, kv_ref, qb_ref, kb_ref, vb_ref, rb_ref, la_ref):
    sub = min(SUB_PROJ, x_ref.shape[0])
    nsub = x_ref.shape[0] // sub
    scale = A_HEAD_DIM ** -0.5 * LOG2E
    o_b = A_Q + 2 * A_KV

    def prologue(s):
        rows = slice(s * sub, (s + 1) * sub)
        return _modulate(x_ref[rows, :], g_ref[...], mod_ref[3:4, :], mod_ref[4:5, :]).astype(BF16)

    def cols(h, lo, hi):
        return jnp.dot(h, w_ref[:, lo:hi], preferred_element_type=F32)

    def rope(v, rows):
        return v * cos_ref[rows, :] + _swap16(v) * sin_ref[rows, :]

    def attn_q(h, rows):
        z = cols(h, 0, A_Q)
        for p in range(A_Q // LANES):
            q = z[:, p * LANES:(p + 1) * LANES]
            qa_ref[rows, p * LANES:(p + 1) * LANES] = (rope(q, rows) * scale).astype(BF16)

    def attn_kv(h, rows):
        z = cols(h, A_Q, A_Q + 2 * A_KV)
        k0, k1 = _dup_halves(rope(z[:, :A_KV], rows))
        v0, v1 = _halves_with_ones(z[:, A_KV:])
        for j, piece in enumerate((k0, k1, v0, v1)):
            kv_ref[rows, j * LANES:(j + 1) * LANES] = piece.astype(BF16)

    def gla_qk(h, rows):
        z = cols(h, o_b, o_b + 2 * B_QK)
        qb_ref[rows, :] = z[:, :B_QK]
        kb_ref[rows, :] = z[:, B_QK:]

    def gla_v(h, rows):
        vb_ref[rows, :] = cols(h, o_b + 2 * B_QK, o_b + 2 * B_QK + B_V).astype(BF16)

    def gla_r(h, rows):
        rb_ref[rows, :] = cols(h, o_b + 2 * B_QK + B_V, o_b + 2 * B_QK + 2 * B_V)

    def gla_gate(h, rows):
        zg = jnp.dot(h, wz_ref[...], preferred_element_type=F32).astype(BF16)
        pre = jnp.dot(zg, w2_ref[...], preferred_element_type=F32) + b2_ref[...]
        la_ref[rows, :] = (jnp.minimum(pre, 0.0) - jnp.log1p(jnp.exp(-jnp.abs(pre)))) / B_GATE_NORM

    stages = (gla_gate, attn_q, attn_kv, gla_qk, gla_v, gla_r)
    h_next = prologue(0)
    first_done = False
    for s in range(nsub):
        rows = slice(s * sub, (s + 1) * sub)
        h = h_next
        begin, first_done = (1 if first_done else 0), False
        for k in range(begin, len(stages)):
            if k == len(stages) - 1 and s + 1 < nsub:
                h_next = prologue(s + 1)
                stages[0](h_next, slice((s + 1) * sub, (s + 2) * sub))
                first_done = True
            stages[k](h, rows)


def _proj(xc, n_tok, tok_per_batch, ctx_row, mods, g, w_main, w_zg, w2, b2, cos, sin):
    rows, d = xc.shape
    tm = TM_PROJ
    nx = n_tok // tm
    tiles_per_batch = tok_per_batch // tm
    widths = [(A_Q, BF16), (KV_DUP, BF16), (B_QK, F32), (B_QK, F32),
              (B_V, BF16), (B_V, F32), (2 * B_QK, F32)]

    def mod_row(i):
        return jnp.where(i < nx, i // tiles_per_batch, ctx_row)

    def pos_blk(i):
        return jnp.where(i < nx, i % tiles_per_batch, tiles_per_batch)

    return pl.pallas_call(
        _proj_kernel,
        grid=(rows // tm,),
        in_specs=[
            pl.BlockSpec((tm, d), lambda i: (i, 0)),
            pl.BlockSpec((None, N_MOD, d), lambda i: (mod_row(i), 0, 0)),
            _const_spec((1, d)),
            _const_spec(w_main.shape),
            _const_spec(w_zg.shape),
            _const_spec(w2.shape),
            _const_spec(b2.shape),
            pl.BlockSpec((tm, LANES), lambda i: (pos_blk(i), 0)),
            pl.BlockSpec((tm, LANES), lambda i: (pos_blk(i), 0)),
        ],
        out_specs=[pl.BlockSpec((tm, w), lambda i: (i, 0)) for w, _ in widths],
        out_shape=[jax.ShapeDtypeStruct((rows, w), dt) for w, dt in widths],
        compiler_params=_cparams("parallel"),
        name="mixer_proj",
    )(xc, mods, g.reshape(1, d), w_main, w_zg, w2, b2, cos, sin)


ATTN_QBLOCKS = 4


def _attn_kernel(sink_ref, q_ref, kvp_ref, kvc_ref, kvn_ref, ckv_ref, o_ref, *, nsteps):
    n = pl.program_id(1)
    blk = WINDOW
    row = lax.broadcasted_iota(jnp.int32, (blk, blk), 0)
    col = lax.broadcasted_iota(jnp.int32, (blk, blk), 1)
    ninf = jnp.float32(-jnp.inf)
    lane = lax.broadcasted_iota(jnp.int32, (blk, LANES), 1)
    lo = lane < LANES // 2
    rep = A_HEADS // A_KV_HEADS

    def key_block(i, lanes):
        if i == 0:
            return kvp_ref[:, lanes]
        if i == ATTN_QBLOCKS + 1:
            return kvn_ref[:, lanes]
        return kvc_ref[(i - 1) * blk:i * blk, lanes]

    def band(qb, lanes):
        return jnp.concatenate([key_block(qb + j, lanes) for j in range(3)] + [ckv_ref[:, lanes]], axis=0)

    kd = [[band(qb, slice(g * LANES, (g + 1) * LANES)) for g in range(A_KV_HEADS)] for qb in range(ATTN_QBLOCKS)]
    vd = [[band(qb, slice((A_KV_HEADS + g) * LANES, (A_KV_HEADS + g + 1) * LANES)) for g in range(A_KV_HEADS)]
          for qb in range(ATTN_QBLOCKS)]
    nkb = kd[0][0].shape[0] // blk
    bias_prev = [jnp.where((col >= row) & ((n > 0) if qb == 0 else True), 0.0, ninf) for qb in range(ATTN_QBLOCKS)]
    bias_next = [jnp.where((col <= row) & ((n < nsteps - 1) if qb == ATTN_QBLOCKS - 1 else True), 0.0, ninf)
                 for qb in range(ATTN_QBLOCKS)]

    def scores(qb, p):
        pair = q_ref[qb * blk:(qb + 1) * blk, p * LANES:(p + 1) * LANES]
        zero = jnp.zeros_like(pair)
        lhs = jnp.concatenate([jnp.where(lo, pair, zero), jnp.where(lo, zero, pair)], axis=0)
        return lax.dot_general(lhs, kd[qb][(2 * p) // rep], (((1,), (1,)), ((), ())), preferred_element_type=F32)

    def softmax(qb, p, s):
        ps, sinks = [], []
        for r in range(2):
            sk = sink_ref[2 * p + r] * LOG2E
            cb = [s[r * blk:(r + 1) * blk, j * blk:(j + 1) * blk] for j in range(nkb)]
            cb[0] = cb[0] + bias_prev[qb]
            cb[2] = cb[2] + bias_next[qb]
            mx = cb[0]
            for c in cb[1:]:
                mx = jnp.maximum(mx, c)
            m = jnp.maximum(jnp.max(mx, axis=-1, keepdims=True), sk)
            ps.append(jnp.concatenate([jnp.exp2((c - m).astype(BF16)) for c in cb], axis=1))
            sinks.append(jnp.exp2(sk - m))
        return jnp.concatenate(ps, axis=0), sinks

    def weighted_values(qb, p, probs, sinks):
        o = jnp.dot(probs, vd[qb][(2 * p) // rep], preferred_element_type=F32)
        o_even, o_odd = o[:blk, :], o[blk:, :]
        num = jnp.where(lo, o_even, pltpu.roll(o_odd, LANES // 2, 1))
        den = jnp.where(lo, pltpu.roll(o_even, LANES // 2, 1) + sinks[0], o_odd + sinks[1])
        o_ref[qb * blk:(qb + 1) * blk, p * LANES:(p + 1) * LANES] = (num / den).astype(BF16)

    units = [(qb, p) for qb in range(ATTN_QBLOCKS) for p in range(A_HEADS // 2)]
    s_next = scores(*units[0])
    for i, unit in enumerate(units):
        s_cur = s_next
        if i + 1 < len(units):
            s_next = scores(*units[i + 1])
        probs, sinks = softmax(*unit, s_cur)
        weighted_values(*unit, probs, sinks)


def _attention(sink, qa, kv, b, t, lctx):
    nb = t // WINDOW
    tq = ATTN_QBLOCKS * WINDOW
    nsteps = t // tq
    ctx0 = (b * t) // lctx

    def edge(off):
        return pl.BlockSpec((WINDOW, KV_DUP),
                            lambda bi, n: (bi * nb + jnp.clip(n * ATTN_QBLOCKS + off, 0, nb - 1), 0))

    return pl.pallas_call(
        functools.partial(_attn_kernel, nsteps=nsteps),
        grid=(b, nsteps),
        in_specs=[
            pl.BlockSpec(memory_space=pltpu.SMEM),
            pl.BlockSpec((tq, A_Q), lambda bi, n: (bi * nsteps + n, 0)),
            edge(-1),
            pl.BlockSpec((tq, KV_DUP), lambda bi, n: (bi * nsteps + n, 0)),
            edge(ATTN_QBLOCKS),
            pl.BlockSpec((lctx, KV_DUP), lambda bi, n: (ctx0 + bi, 0)),
        ],
        out_specs=pl.BlockSpec((tq, A_Q), lambda bi, n: (bi * nsteps + n, 0)),
        out_shape=jax.ShapeDtypeStruct((b * t, A_Q), BF16),
        compiler_params=_cparams("parallel", "parallel"),
        name="window_attn",
    )(sink, qa, kv, kv, kv, kv)


B_PAIRS = B_HEADS // 2
PAIR_QK = 2 * B_DK
PAIR_V = 2 * B_DV


class _GlaMasks:
    def __init__(self, reverse):
        c = B_CHUNK
        i = lax.broadcasted_iota(jnp.int32, (c, c), 0)
        j = lax.broadcasted_iota(jnp.int32, (c, c), 1)
        self.tri = jnp.where((j >= i) if reverse else (j <= i), 1.0, 0.0).astype(BF16)
        lane_qk = lax.broadcasted_iota(jnp.int32, (c, B_QK), 1)
        self.k_head = [(lane_qk >= h * B_DK) & (lane_qk < (h + 1) * B_DK) for h in range(B_HEADS)]
        lane_v = lax.broadcasted_iota(jnp.int32, (c, B_V), 1)
        self.v_head = [(lane_v >= h * B_DV) & (lane_v < (h + 1) * B_DV) for h in range(B_HEADS)]
        ai = lax.broadcasted_iota(jnp.int32, (c, B_HEADS * c), 0)
        aj = lax.broadcasted_iota(jnp.int32, (c, B_HEADS * c), 1) % c
        self.causal = (aj >= ai) if reverse else (aj <= ai)
        sr = lax.broadcasted_iota(jnp.int32, (PAIR_V, PAIR_QK), 0) // B_DV
        sl = lax.broadcasted_iota(jnp.int32, (PAIR_V, PAIR_QK), 1) // B_DK
        self.state_diag = sr == sl


def _gla_block(q_ref, k_ref, v_ref, la_ref, st, mk, reverse, want_out):
    c = B_CHUNK
    nc = k_ref.shape[0] // c
    order = list(reversed(range(nc))) if reverse else list(range(nc))
    rows = {ci: slice(ci * c, (ci + 1) * c) for ci in order}
    tb = (((1,), (1,)), ((), ()))
    ta = (((0,), (0,)), ((), ()))

    g = {}
    for ci in order:
        la = la_ref[rows[ci], :]
        hi = la.astype(BF16)
        lo = (la - hi.astype(F32)).astype(BF16)
        g[ci] = jnp.dot(mk.tri, hi, preferred_element_type=F32) + jnp.dot(mk.tri, lo, preferred_element_type=F32)

    dec, k_end, q_t, kstack = {}, {}, {}, {}
    for ci in order:
        k = k_ref[rows[ci], :]
        g_last = g[ci][0:1, :] if reverse else g[ci][c - 1:c, :]
        dec[ci] = jnp.exp(g_last)
        k_end[ci] = (k * jnp.exp(g_last - g[ci])).astype(BF16)
        if want_out:
            q_t[ci] = ((q_ref[rows[ci], :] * (B_DK ** -0.5)) * jnp.exp(g[ci])).astype(BF16)
            k_t = k * jnp.exp(-g[ci])
            kstack[ci] = jnp.concatenate([jnp.where(m, k_t, 0.0) for m in mk.k_head], axis=0).astype(BF16)

    ds = {}
    for ci in order:
        v = v_ref[rows[ci], :]
        ds[ci] = [jnp.where(mk.state_diag,
                            lax.dot_general(v[:, p * PAIR_V:(p + 1) * PAIR_V],
                                            k_end[ci][:, p * PAIR_QK:(p + 1) * PAIR_QK], ta,
                                            preferred_element_type=F32), 0.0) for p in range(B_PAIRS)]

    st_prev = {}
    for ci in order:
        st_prev[ci] = st
        st = [st[p] * dec[ci][:, p * PAIR_QK:(p + 1) * PAIR_QK] + ds[ci][p] for p in range(B_PAIRS)]
    if not want_out:
        return None, st

    att = {}
    for ci in order:
        a = lax.dot_general(q_t[ci], kstack[ci], tb, preferred_element_type=F32)
        att[ci] = jnp.where(mk.causal, a, 0.0).astype(BF16)
    o = {}
    for ci in order:
        v = v_ref[rows[ci], :]
        vstack = jnp.concatenate([jnp.where(m, v, jnp.zeros_like(v)) for m in mk.v_head], axis=0)
        o[ci] = jnp.dot(att[ci], vstack, preferred_element_type=F32) + jnp.concatenate(
            [lax.dot_general(q_t[ci][:, p * PAIR_QK:(p + 1) * PAIR_QK], st_prev[ci][p].astype(BF16), tb,
                             preferred_element_type=F32) for p in range(B_PAIRS)], axis=1)
    return o, st


def _gla_ctx_kernel(k_ref, v_ref, la_ref, st_ref, *, reverse):
    mk = _GlaMasks(reverse)
    st = [jnp.zeros((PAIR_V, PAIR_QK), F32) for _ in range(B_PAIRS)]
    _, st = _gla_block(None, k_ref, v_ref, la_ref, st, mk, reverse, False)
    for p in range(B_PAIRS):
        st_ref[p] = st[p]


def _gla_ctx_state(kb, vb, la, b, t, l, reverse):
    d = 1 if reverse else 0
    ctx0 = (b * t) // l
    return pl.pallas_call(
        functools.partial(_gla_ctx_kernel, reverse=reverse),
        grid=(b,),
        in_specs=[
            pl.BlockSpec((l, B_QK), lambda bi: (ctx0 + bi, 0)),
            pl.BlockSpec((l, B_V), lambda bi: (ctx0 + bi, 0)),
            pl.BlockSpec((l, B_QK), lambda bi: (ctx0 + bi, d)),
        ],
        out_specs=pl.BlockSpec((None, B_PAIRS, PAIR_V, PAIR_QK), lambda bi: (bi, 0, 0, 0)),
        out_shape=jax.ShapeDtypeStruct((b, B_PAIRS, PAIR_V, PAIR_QK), F32),
        compiler_params=_cparams("parallel"),
        name="gla_ctx_state",
    )(kb, vb, la)


def _gla_kernel(q_ref, k_ref, v_ref, la_ref, st0_ref, *rest, reverse):
    if reverse:
        of_ref, r_ref, gg_ref, o_ref, st_ref = rest
    else:
        o_ref, st_ref = rest

    @pl.when(pl.program_id(1) == 0)
    def _():
        st_ref[...] = st0_ref[...]

    mk = _GlaMasks(reverse)
    outs_by_chunk, st = _gla_block(q_ref, k_ref, v_ref, la_ref, [st_ref[p] for p in range(B_PAIRS)],
                                   mk, reverse, True)
    for ci, o in outs_by_chunk.items():
        rows = slice(ci * B_CHUNK, (ci + 1) * B_CHUNK)
        if reverse:
            o = o + of_ref[rows, :]
            outs = []
            for h in range(B_HEADS):
                oh = o[:, h * B_DV:(h + 1) * B_DV]
                oh = (oh * lax.rsqrt(jnp.mean(oh * oh, axis=-1, keepdims=True) + RMS_EPS)) * gg_ref[...]
                outs.append(oh * _silu(r_ref[rows, h * B_DV:(h + 1) * B_DV]))
            o_ref[rows, :] = jnp.concatenate(outs, axis=1).astype(BF16)
        else:
            o_ref[rows, :] = o
    for p in range(B_PAIRS):
        st_ref[p] = st[p]


def _gla(qb, kb, vb, la, st0, b, t, reverse, o_fwd=None, rb=None, gla_g=None):
    tb = TB_GLA
    nt = t // tb
    d = 1 if reverse else 0

    def tok(w, lane_blk=0):
        if reverse:
            return pl.BlockSpec((tb, w), lambda bi, j: (bi * nt + nt - 1 - j, lane_blk))
        return pl.BlockSpec((tb, w), lambda bi, j: (bi * nt + j, lane_blk))

    in_specs = [tok(B_QK), tok(B_QK), tok(B_V), tok(B_QK, d),
                pl.BlockSpec((None, B_PAIRS, PAIR_V, PAIR_QK), lambda bi, j: (bi, 0, 0, 0))]
    args = [qb, kb, vb, la, st0]
    if reverse:
        in_specs += [tok(B_V), tok(B_V), _const_spec((1, B_DV))]
        args += [o_fwd, rb, gla_g.reshape(1, B_DV)]
    return pl.pallas_call(
        functools.partial(_gla_kernel, reverse=reverse),
        grid=(b, nt),
        in_specs=in_specs,
        out_specs=tok(B_V),
        out_shape=jax.ShapeDtypeStruct((b * t, B_V), BF16 if reverse else F32),
        scratch_shapes=[pltpu.VMEM((B_PAIRS, PAIR_V, PAIR_QK), F32)],
        compiler_params=_cparams("parallel", "arbitrary"),
        name="gla_bwd" if reverse else "gla_fwd",
    )(*args)


def _shift_rows(e, d):
    n = e.shape[0]
    return pltpu.roll(e, (-d) % n, 0)


def _pool_kernel(x_ref, xp_ref, xn_ref, mod_ref, g_ref, w_ref, ps_ref, o_ref, *, seq):
    i = pl.program_id(1)
    tm = x_ref.shape[0]
    g = g_ref[...]
    shift = mod_ref[3:4, :]
    scale = mod_ref[4:5, :]
    x = x_ref[...]
    h = _modulate(x, g, shift, scale)
    hp = jnp.where(i > 0, _modulate(xp_ref[...], g, shift, scale), 0.0)
    hn = jnp.where(i < pl.num_programs(1) - 1, _modulate(xn_ref[...], g, shift, scale), 0.0)
    groups, totals = [], []
    for gi, w in enumerate(POOL_WINDOWS):
        ls = slice(gi * POOL_GROUP, (gi + 1) * POOL_GROUP)
        groups.append(h[:, ls])
        s = jnp.concatenate([hp[:, ls], h[:, ls], hn[:, ls]], axis=0)
        k = 1
        while k < w:
            s = s + _shift_rows(s, -k)
            k *= 2
        if w // 2 > 1:
            s = _shift_rows(s, w // 2 - 1)
        totals.append(s[HALO:HALO + tm, :])

    def finish(scales):
        ys = [jnp.dot((totals[gi] * scales[gi] - groups[gi]).astype(BF16), w_ref[gi], preferred_element_type=F32)
              for gi in range(len(POOL_WINDOWS))]
        y = jnp.concatenate(ys, axis=1) * ps_ref[...]
        o_ref[...] = x + mod_ref[5:6, :] * y

    clipped = (i == 0) | (i == pl.num_programs(1) - 1)

    @pl.when(clipped)
    def _():
        t = i * tm + lax.broadcasted_iota(jnp.int32, (tm, 1), 0)
        finish([1.0 / (jnp.minimum(t + w // 2, seq) - jnp.maximum(t - w // 2, 0)).astype(F32)
                for w in POOL_WINDOWS])

    @pl.when(jnp.logical_not(clipped))
    def _():
        finish([1.0 / w for w in POOL_WINDOWS])


def _pool(x, mods, g, w_pool, pool_scale):
    b, t, d = x.shape
    tm = TM_POOL
    hb = tm // HALO
    last = t // HALO - 1
    return pl.pallas_call(
        functools.partial(_pool_kernel, seq=t),
        grid=(b, t // tm),
        in_specs=[
            pl.BlockSpec((None, tm, d), lambda bi, i: (bi, i, 0)),
            pl.BlockSpec((None, HALO, d), lambda bi, i: (bi, jnp.maximum(i * hb - 1, 0), 0)),
            pl.BlockSpec((None, HALO, d), lambda bi, i: (bi, jnp.minimum((i + 1) * hb, last), 0)),
            pl.BlockSpec((None, N_MOD, d), lambda bi, i: (bi, 0, 0)),
            _const_spec((1, d)),
            _const_spec(w_pool.shape),
            _const_spec((1, d)),
        ],
        out_specs=pl.BlockSpec((None, tm, d), lambda bi, i: (bi, i, 0)),
        out_shape=jax.ShapeDtypeStruct((b, t, d), F32),
        compiler_params=_cparams("parallel", "parallel"),
        name="pool_mixer",
    )(x, x, x, mods, g.reshape(1, d), w_pool, pool_scale.reshape(1, d))


def _rope_tables(t, pad_rows):
    n = A_HEAD_DIM // 4
    freqs = ROPE_BASE ** (-jnp.arange(n, dtype=F32) / n)
    pos = jnp.arange(t)
    rows = (pos // GRID_W).astype(F32)
    cols = (pos % GRID_W).astype(F32)
    ang_r = rows[:, None] * freqs
    ang_c = cols[:, None] * freqs
    cos = jnp.concatenate([jnp.cos(ang_r), jnp.cos(ang_r), jnp.cos(ang_c), jnp.cos(ang_c)], axis=1)
    sin = jnp.concatenate([-jnp.sin(ang_r), jnp.sin(ang_r), -jnp.sin(ang_c), jnp.sin(ang_c)], axis=1)
    cos = jnp.concatenate([jnp.tile(cos, (1, 2)), jnp.ones((pad_rows, LANES), F32)], axis=0)
    sin = jnp.concatenate([jnp.tile(sin, (1, 2)), jnp.zeros((pad_rows, LANES), F32)], axis=0)
    return cos, sin


def kernel(x, c, ctx, c_ctx, w_mod, b_mod, norm_g, ffn1_wi, ffn1_wo, ffn2_wi, ffn2_wo,
           w_in, w_a2_f, b_a_f, w_a2_b, b_a_b, sink, gla_g, w_out, w_pool, pool_scale, final_g):
    b, t, d = x.shape
    lctx = ctx.shape[1]
    n_tok = b * t
    ctx_row = b

    cond = jnp.zeros((COND_ROWS, d), F32).at[:b].set(c).at[ctx_row].set(c_ctx)
    mods = _adaln(cond, w_mod, b_mod).reshape(w_mod.shape[0], COND_ROWS, N_MOD, d)
    x = x.reshape(n_tok, d)

    xc = _ffn(x, n_tok, t, mods[0], norm_g[0, 0], ffn1_wi, ffn1_wo, 0, 0,
              ctx=ctx.reshape(b * lctx, d), ctx_row=ctx_row)

    w_main = w_in[0][:, :PROJ_MAIN].astype(BF16)
    w_zg = jnp.zeros((d, LANES), F32).at[:, :2 * B_GATE_RANK].set(w_in[0][:, PROJ_MAIN:]).astype(BF16)
    w2 = jnp.zeros((LANES, 2 * B_QK), F32)
    w2 = w2.at[:B_GATE_RANK, :B_QK].set(w_a2_f[0]).at[B_GATE_RANK:2 * B_GATE_RANK, B_QK:].set(w_a2_b[0]).astype(BF16)
    b2 = jnp.concatenate([b_a_f[0], b_a_b[0]]).reshape(1, 2 * B_QK)
    cos, sin = _rope_tables(t, TM_PROJ)
    qa, kv, qb, kb, vb, rb, la = _proj(xc, n_tok, t, ctx_row, mods[0], norm_g[0, 1], w_main, w_zg, w2, b2, cos, sin)

    o_a = _attention(sink[0], qa, kv, b, t, lctx)
    st_f = _gla_ctx_state(kb, vb, la, b, t, lctx, False)
    st_b = _gla_ctx_state(kb, vb, la, b, t, lctx, True)
    o_f = _gla(qb, kb, vb, la, st_f, b, t, False)
    o_g = _gla(qb, kb, vb, la, st_b, b, t, True, o_f, rb, gla_g[0])
    x = _ffn(xc, n_tok, t, mods[0], norm_g[0, 2], ffn2_wi, ffn2_wo, 0, 6, mix=(o_a, o_g, w_out[0].astype(BF16)))

    x = _ffn(x, n_tok, t, mods[1], norm_g[1, 0], ffn1_wi, ffn1_wo, 1, 0)
    x = _pool(x.reshape(b, t, d), mods[1], norm_g[1, 1], w_pool[0].astype(BF16), pool_scale[0])
    x = _ffn(x.reshape(n_tok, d), n_tok, t, mods[1], norm_g[1, 2], ffn2_wi, ffn2_wo, 1, 6, final_g=final_g)
    return x.reshape(b, t, d)
```

```python
import functools

import jax
import jax.numpy as jnp
import numpy as np
from jax import lax
from jax.experimental import pallas as pl
from jax.experimental.pallas import tpu as pltpu

F32 = jnp.float32
BF16 = jnp.bfloat16

D_MODEL = 1024
N_MOD = 9
D_FF = 2816
RMS_EPS = 1e-6
GRID_W = 64
A_HEADS = 8
A_KV_HEADS = 2
A_HEAD_DIM = 64
WINDOW = 128
ROPE_BASE = 10000.0
B_HEADS = 4
B_DK = 64
B_DV = 128
B_GATE_RANK = 16
B_GATE_NORM = 16.0
B_CHUNK = 64
POOL_WINDOWS = (2, 4, 8, 16)
POOL_GROUP = D_MODEL // len(POOL_WINDOWS)
A_Q = A_HEADS * A_HEAD_DIM
A_KV = A_KV_HEADS * A_HEAD_DIM
B_QK = B_HEADS * B_DK
B_V = B_HEADS * B_DV
PROJ_MAIN = A_Q + 2 * A_KV + 2 * B_QK + 2 * B_V
KV_DUP = 4 * A_KV

LOG2E = 1.4426950408889634
LANES = 128
COND_ROWS = 8
VMEM_LIMIT = 56 * 1024 * 1024

TM_FFN = 1024
SUB_FFN = 256
TM_PROJ = 1024
SUB_PROJ = 256
TM_POOL = 512
TB_GLA = 1024
GLA_GROUP = 4
HALO = 8


def _cparams(*sem):
    return pltpu.CompilerParams(dimension_semantics=sem, vmem_limit_bytes=VMEM_LIMIT)


def _const_spec(shape):
    nd = len(shape)
    return pl.BlockSpec(shape, lambda *_: (0,) * nd, pipeline_mode=pl.Buffered(1))


def _modulate(x, g, shift, scale):
    y = x * lax.rsqrt(jnp.mean(x * x, axis=-1, keepdims=True) + RMS_EPS)
    return (y * g) * (1.0 + scale) + shift


def _silu(a):
    return a * jax.nn.sigmoid(a)


ADALN_MODS_PER_STEP = 3


def _adaln_kernel(c_ref, w_ref, b_ref, o_ref):
    d = c_ref.shape[1]
    s = _silu(c_ref[...]).astype(BF16)
    r = jnp.dot(s, w_ref[...].astype(BF16), preferred_element_type=F32) + b_ref[...]
    for k in range(ADALN_MODS_PER_STEP):
        o_ref[k] = r[:, k * d:(k + 1) * d]


def _adaln(cond, w_mod, b_mod):
    depth, d, n = w_mod.shape
    per = ADALN_MODS_PER_STEP
    tn = per * d
    return pl.pallas_call(
        _adaln_kernel,
        grid=(depth, N_MOD // per),
        in_specs=[
            pl.BlockSpec((COND_ROWS, d), lambda l, j: (0, 0)),
            pl.BlockSpec((None, d, tn), lambda l, j: (l, 0, j)),
            pl.BlockSpec((None, 1, tn), lambda l, j: (l, 0, j)),
        ],
        out_specs=pl.BlockSpec((None, per, COND_ROWS, d), lambda l, j: (l, j, 0, 0)),
        out_shape=jax.ShapeDtypeStruct((depth, N_MOD, COND_ROWS, d), F32),
        compiler_params=_cparams("parallel", "parallel"),
        name="adaln",
    )(cond, w_mod, b_mod.reshape(depth, 1, n))


def _mod_spec(layer, d):
    return pl.BlockSpec((None, N_MOD, COND_ROWS, d), lambda *_: (layer, 0, 0, 0))


def _mod(mod_ref, idx, row):
    return mod_ref[idx, pl.ds(row, 1), :]


FFN_WCHUNK = 256
FFN_NW = 2 * D_FF // FFN_WCHUNK
FFN_WO_ROWS = D_FF // FFN_NW
FFN_NCH = D_FF // FFN_WCHUNK
FFN_PROLOGUE_AT = 2


def _ffn_tile(x_ref, o_ref, mod_ref, row, g_ref, wi_s, wo_s, mi, mix, fg_ref):
    sub = min(SUB_FFN, x_ref.shape[0])
    nsub = x_ref.shape[0] // sub

    def prologue(s):
        rows = slice(s * sub, (s + 1) * sub)
        x = x_ref[rows, :]
        if mix is not None:
            oa_ref, ob_ref, w_ref = mix
            y0 = (jnp.dot(oa_ref[rows, :], w_ref[:A_Q, :], preferred_element_type=F32)
                  + jnp.dot(ob_ref[rows, :], w_ref[A_Q:, :], preferred_element_type=F32))
            x = x + _mod(mod_ref, 5, row) * y0
        return x, _modulate(x, g_ref[...], _mod(mod_ref, mi, row), _mod(mod_ref, mi + 1, row)).astype(BF16)

    def act_chunk(h, c):
        a = jnp.dot(h, wi_s[c], preferred_element_type=F32)
        u = jnp.dot(h, wi_s[FFN_NCH + c], preferred_element_type=F32)
        return (_silu(a) * u).astype(BF16)

    nxt = prologue(0)
    early = []
    for s in range(nsub):
        rows = slice(s * sub, (s + 1) * sub)
        x, h = nxt
        acts, early = early, []
        for c in range(len(acts), FFN_NCH):
            if c == FFN_NCH - 1 and s + 1 < nsub:
                nxt = prologue(s + 1)
                early = [act_chunk(nxt[1], 0)]
            acts.append(act_chunk(h, c))
        y = jnp.dot(jnp.concatenate(acts, axis=1), wo_s[...], preferred_element_type=F32)
        o = x + (0.5 * _mod(mod_ref, mi + 2, row)) * y
        if fg_ref is not None:
            o = (o * lax.rsqrt(jnp.mean(o * o, axis=-1, keepdims=True) + RMS_EPS)) * fg_ref[...]
        o_ref[rows, :] = o


def _ffn_kernel(*refs, mi, nx, tiles_per_batch, ctx_row, has_ctx, has_mix, final):
    refs = list(refs)
    x_ref = refs.pop(0)
    c_ref = refs.pop(0) if has_ctx else None
    mod_ref, g_ref, wi_ref, wo_ref = refs[:4]
    refs = refs[4:]
    mix = tuple(refs[:3]) if has_mix else None
    refs = refs[3:] if has_mix else refs
    fg_ref = refs.pop(0) if final else None
    o_ref, wi_s, wo_s = refs
    j = pl.program_id(0)

    @pl.when(j < FFN_NW)
    def _():
        wi_s[j] = wi_ref[...].astype(BF16)
        wo_s[pl.ds(pl.multiple_of(j * FFN_WO_ROWS, FFN_WO_ROWS), FFN_WO_ROWS), :] = wo_ref[...].astype(BF16)

    @pl.when((j >= FFN_NW) & (j < FFN_NW + nx))
    def _():
        _ffn_tile(x_ref, o_ref, mod_ref, (j - FFN_NW) // tiles_per_batch, g_ref, wi_s, wo_s, mi, mix, fg_ref)

    if has_ctx:
        @pl.when(j >= FFN_NW + nx)
        def _():
            _ffn_tile(c_ref, o_ref, mod_ref, ctx_row, g_ref, wi_s, wo_s, mi, None, fg_ref)


def _ffn(x, n_rows, tok_per_batch, mods, g, wi, wo, layer, mi, *, ctx=None, ctx_row=None, mix=None, final_g=None):
    d = x.shape[1]
    tm = TM_FFN
    nx = n_rows // tm
    nc = 0 if ctx is None else ctx.shape[0] // tm
    tiles_per_batch = tok_per_batch // tm

    def tile(j):
        return jnp.clip(j - FFN_NW, 0, nx + nc - 1)

    def x_tile(j):
        return jnp.minimum(tile(j), nx - 1)

    def wchunk(j):
        return jnp.minimum(j, FFN_NW - 1)

    in_specs = [pl.BlockSpec((tm, d), lambda j: (x_tile(j), 0))]
    args = [x]
    if ctx is not None:
        in_specs.append(pl.BlockSpec((tm, d), lambda j: (jnp.clip(j - FFN_NW - nx, 0, nc - 1), 0)))
        args.append(ctx)
    in_specs += [
        _mod_spec(layer, d),
        _const_spec((1, d)),
        pl.BlockSpec((None, d, FFN_WCHUNK), lambda j: (layer, 0, wchunk(j))),
        pl.BlockSpec((None, FFN_WO_ROWS, d), lambda j: (layer, wchunk(j), 0)),
    ]
    args += [mods, g.reshape(1, d), wi, wo]
    if mix is not None:
        oa, ob, w_out = mix
        in_specs += [pl.BlockSpec((tm, A_Q), lambda j: (x_tile(j), 0)),
                     pl.BlockSpec((tm, B_V), lambda j: (x_tile(j), 0)),
                     _const_spec(w_out.shape)]
        args += [oa, ob, w_out]
    if final_g is not None:
        in_specs.append(_const_spec((1, d)))
        args.append(final_g.reshape(1, d))
    return pl.pallas_call(
        functools.partial(_ffn_kernel, mi=mi, nx=nx, tiles_per_batch=tiles_per_batch, ctx_row=ctx_row,
                          has_ctx=ctx is not None, has_mix=mix is not None,
                          final=final_g is not None),
        grid=(FFN_NW + nx + nc,),
        in_specs=in_specs,
        out_specs=pl.BlockSpec((tm, d), lambda j: (tile(j), 0)),
        out_shape=jax.ShapeDtypeStruct(((nx + nc) * tm, d), F32),
        scratch_shapes=[pltpu.VMEM((FFN_NW, d, FFN_WCHUNK), BF16), pltpu.VMEM((D_FF, d), BF16)],
        compiler_params=_cparams("arbitrary"),
        name="ffn",
    )(*args)


def _swap16(x):
    lane = lax.broadcasted_iota(jnp.int32, x.shape, 1)
    return jnp.where((lane & 16) == 0, pltpu.roll(x, LANES - 16, 1), pltpu.roll(x, 16, 1))


def _dup_halves(x):
    lane = lax.broadcasted_iota(jnp.int32, x.shape, 1)
    r = pltpu.roll(x, LANES // 2, 1)
    lo = lane < LANES // 2
    return jnp.where(lo, x, r), jnp.where(lo, r, x)


def _halves_with_ones(x):
    lane = lax.broadcasted_iota(jnp.int32, x.shape, 1)
    r = pltpu.roll(x, LANES // 2, 1)
    lo = lane < LANES // 2
    return jnp.where(lo, x, 1.0), jnp.where(lo, r, 1.0)


def _proj_kernel(x_ref, mod_ref, g_ref, w_ref, wz_ref, w2_ref, b2_ref, cos_ref, sin_ref,
                 qa_ref, kv_ref, qb_ref, kb_ref, vb_ref, rb_ref, la_ref, *, nx, tiles_per_batch, ctx_row):
    i = pl.program_id(0)
    row = jnp.where(i < nx, i // tiles_per_batch, ctx_row)
    sub = min(SUB_PROJ, x_ref.shape[0])
    nsub = x_ref.shape[0] // sub
    scale = A_HEAD_DIM ** -0.5 * LOG2E
    o_b = A_Q + 2 * A_KV

    def prologue(s):
        rows = slice(s * sub, (s + 1) * sub)
        return _modulate(x_ref[rows, :], g_ref[...], _mod(mod_ref, 3, row), _mod(mod_ref, 4, row)).astype(BF16)

    def cols(h, lo, hi):
        return jnp.dot(h, w_ref[:, lo:hi], preferred_element_type=F32)

    def rope(v, rows):
        return v * cos_ref[rows, :] + _swap16(v) * sin_ref[rows, :]

    def attn_q(h, rows):
        z = cols(h, 0, A_Q)
        for p in range(A_Q // LANES):
            q = z[:, p * LANES:(p + 1) * LANES]
            qa_ref[rows, p * LANES:(p + 1) * LANES] = (rope(q, rows) * scale).astype(BF16)

    def attn_kv(h, rows):
        z = cols(h, A_Q, A_Q + 2 * A_KV)
        k0, k1 = _dup_halves(rope(z[:, :A_KV], rows))
        v0, v1 = _halves_with_ones(z[:, A_KV:])
        for j, piece in enumerate((k0, k1, v0, v1)):
            kv_ref[rows, j * LANES:(j + 1) * LANES] = piece.astype(BF16)

    def gla_qk(h, rows):
        z = cols(h, o_b, o_b + 2 * B_QK)
        qb_ref[rows, :] = z[:, :B_QK].astype(BF16)
        kb_ref[rows, :] = z[:, B_QK:].astype(BF16)

    def gla_v(h, rows):
        vb_ref[rows, :] = cols(h, o_b + 2 * B_QK, o_b + 2 * B_QK + B_V).astype(BF16)

    def gla_r(h, rows):
        rb_ref[rows, :] = cols(h, o_b + 2 * B_QK + B_V, o_b + 2 * B_QK + 2 * B_V).astype(BF16)

    def gla_gate(h, rows):
        zg = jnp.dot(h, wz_ref[...], preferred_element_type=F32).astype(BF16)
        pre = jnp.dot(zg, w2_ref[...], preferred_element_type=F32) + b2_ref[...]
        la_ref[rows, :] = (jnp.minimum(pre, 0.0) - jnp.log1p(jnp.exp(-jnp.abs(pre)))) / B_GATE_NORM

    stages = (gla_gate, attn_q, attn_kv, gla_qk, gla_v, gla_r)
    h_next = prologue(0)
    first_done = False
    for s in range(nsub):
        rows = slice(s * sub, (s + 1) * sub)
        h = h_next
        begin, first_done = (1 if first_done else 0), False
        for k in range(begin, len(stages)):
            if k == len(stages) - 1 and s + 1 < nsub:
                h_next = prologue(s + 1)
                stages[0](h_next, slice((s + 1) * sub, (s + 2) * sub))
                first_done = True
            stages[k](h, rows)


def _proj(xc, n_tok, tok_per_batch, ctx_row, mods, layer, g, w_main, w_zg, w2, b2, cos, sin):
    rows, d = xc.shape
    tm = TM_PROJ
    nx = n_tok // tm
    tiles_per_batch = tok_per_batch // tm
    widths = [(A_Q, BF16), (KV_DUP, BF16), (B_QK, BF16), (B_QK, BF16),
              (B_V, BF16), (B_V, BF16), (2 * B_QK, F32)]

    def pos_blk(i):
        return jnp.where(i < nx, i % tiles_per_batch, tiles_per_batch)

    return pl.pallas_call(
        functools.partial(_proj_kernel, nx=nx, tiles_per_batch=tiles_per_batch, ctx_row=ctx_row),
        grid=(rows // tm,),
        in_specs=[
            pl.BlockSpec((tm, d), lambda i: (i, 0)),
            _mod_spec(layer, d),
            _const_spec((1, d)),
            _const_spec(w_main.shape),
            _const_spec(w_zg.shape),
            _const_spec(w2.shape),
            _const_spec(b2.shape),
            pl.BlockSpec((tm, LANES), lambda i: (pos_blk(i), 0)),
            pl.BlockSpec((tm, LANES), lambda i: (pos_blk(i), 0)),
        ],
        out_specs=[pl.BlockSpec((tm, w), lambda i: (i, 0)) for w, _ in widths],
        out_shape=[jax.ShapeDtypeStruct((rows, w), dt) for w, dt in widths],
        compiler_params=_cparams("parallel"),
        name="mixer_proj",
    )(xc, mods, g.reshape(1, d), w_main, w_zg, w2, b2, cos, sin)


ATTN_QBLOCKS = 4


def _attn_kernel(sink_ref, q_ref, kvp_ref, kvc_ref, kvn_ref, ckv_ref, o_ref, *, nsteps):
    n = pl.program_id(1)
    blk = WINDOW
    row = lax.broadcasted_iota(jnp.int32, (blk, blk), 0)
    col = lax.broadcasted_iota(jnp.int32, (blk, blk), 1)
    ninf = jnp.float32(-jnp.inf)
    lane = lax.broadcasted_iota(jnp.int32, (blk, LANES), 1)
    lo = lane < LANES // 2
    rep = A_HEADS // A_KV_HEADS

    def key_block(i, lanes):
        if i == 0:
            return kvp_ref[:, lanes]
        if i == ATTN_QBLOCKS + 1:
            return kvn_ref[:, lanes]
        return kvc_ref[(i - 1) * blk:i * blk, lanes]

    def band(qb, lanes):
        return jnp.concatenate([key_block(qb + j, lanes) for j in range(3)] + [ckv_ref[:, lanes]], axis=0)

    kd = [[band(qb, slice(g * LANES, (g + 1) * LANES)) for g in range(A_KV_HEADS)] for qb in range(ATTN_QBLOCKS)]
    vd = [[band(qb, slice((A_KV_HEADS + g) * LANES, (A_KV_HEADS + g + 1) * LANES)) for g in range(A_KV_HEADS)]
          for qb in range(ATTN_QBLOCKS)]
    nkb = kd[0][0].shape[0] // blk
    bias_prev = [jnp.where((col >= row) & ((n > 0) if qb == 0 else True), 0.0, ninf) for qb in range(ATTN_QBLOCKS)]
    bias_next = [jnp.where((col <= row) & ((n < nsteps - 1) if qb == ATTN_QBLOCKS - 1 else True), 0.0, ninf)
                 for qb in range(ATTN_QBLOCKS)]

    def scores(qb, p):
        pair = q_ref[qb * blk:(qb + 1) * blk, p * LANES:(p + 1) * LANES]
        zero = jnp.zeros_like(pair)
        lhs = jnp.concatenate([jnp.where(lo, pair, zero), jnp.where(lo, zero, pair)], axis=0)
        return lax.dot_general(lhs, kd[qb][(2 * p) // rep], (((1,), (1,)), ((), ())), preferred_element_type=F32)

    def softmax(qb, p, s):
        ps, sinks = [], []
        for r in range(2):
            sk = sink_ref[2 * p + r] * LOG2E
            cb = [s[r * blk:(r + 1) * blk, j * blk:(j + 1) * blk] for j in range(nkb)]
            cb[0] = cb[0] + bias_prev[qb]
            cb[2] = cb[2] + bias_next[qb]
            mx = cb[0]
            for c in cb[1:]:
                mx = jnp.maximum(mx, c)
            m = jnp.maximum(jnp.max(mx, axis=-1, keepdims=True), sk)
            ps.append(jnp.concatenate([jnp.exp2((c - m).astype(BF16)) for c in cb], axis=1))
            sinks.append(jnp.exp2(sk - m))
        return jnp.concatenate(ps, axis=0), sinks

    def weighted_values(qb, p, probs, sinks):
        o = jnp.dot(probs, vd[qb][(2 * p) // rep], preferred_element_type=F32)
        o_even, o_odd = o[:blk, :], o[blk:, :]
        num = jnp.where(lo, o_even, pltpu.roll(o_odd, LANES // 2, 1))
        den = jnp.where(lo, pltpu.roll(o_even, LANES // 2, 1) + sinks[0], o_odd + sinks[1])
        o_ref[qb * blk:(qb + 1) * blk, p * LANES:(p + 1) * LANES] = (num / den).astype(BF16)

    units = [(qb, p) for qb in range(ATTN_QBLOCKS) for p in range(A_HEADS // 2)]
    s_next = scores(*units[0])
    for i, unit in enumerate(units):
        s_cur = s_next
        if i + 1 < len(units):
            s_next = scores(*units[i + 1])
        probs, sinks = softmax(*unit, s_cur)
        weighted_values(*unit, probs, sinks)


def _attention(sink, qa, kv, b, t, lctx):
    nb = t // WINDOW
    tq = ATTN_QBLOCKS * WINDOW
    nsteps = t // tq
    ctx0 = (b * t) // lctx

    def edge(off):
        return pl.BlockSpec((WINDOW, KV_DUP),
                            lambda bi, n: (bi * nb + jnp.clip(n * ATTN_QBLOCKS + off, 0, nb - 1), 0))

    return pl.pallas_call(
        functools.partial(_attn_kernel, nsteps=nsteps),
        grid=(b, nsteps),
        in_specs=[
            pl.BlockSpec(memory_space=pltpu.SMEM),
            pl.BlockSpec((tq, A_Q), lambda bi, n: (bi * nsteps + n, 0)),
            edge(-1),
            pl.BlockSpec((tq, KV_DUP), lambda bi, n: (bi * nsteps + n, 0)),
            edge(ATTN_QBLOCKS),
            pl.BlockSpec((lctx, KV_DUP), lambda bi, n: (ctx0 + bi, 0)),
        ],
        out_specs=pl.BlockSpec((tq, A_Q), lambda bi, n: (bi * nsteps + n, 0)),
        out_shape=jax.ShapeDtypeStruct((b * t, A_Q), BF16),
        compiler_params=_cparams("parallel", "parallel"),
        name="window_attn",
    )(sink, qa, kv, kv, kv, kv)


B_PAIRS = B_HEADS // 2
PAIR_QK = 2 * B_DK
PAIR_V = 2 * B_DV


class _GlaMasks:
    def __init__(self, reverse):
        c = B_CHUNK
        i = lax.broadcasted_iota(jnp.int32, (c, c), 0)
        j = lax.broadcasted_iota(jnp.int32, (c, c), 1)
        self.tri = jnp.where((j >= i) if reverse else (j <= i), 1.0, 0.0).astype(BF16)
        lane_qk = lax.broadcasted_iota(jnp.int32, (c, B_QK), 1)
        self.k_head = [(lane_qk >= h * B_DK) & (lane_qk < (h + 1) * B_DK) for h in range(B_HEADS)]
        lane_v = lax.broadcasted_iota(jnp.int32, (c, B_V), 1)
        self.v_head = [(lane_v >= h * B_DV) & (lane_v < (h + 1) * B_DV) for h in range(B_HEADS)]
        ai = lax.broadcasted_iota(jnp.int32, (c, B_HEADS * c), 0)
        aj = lax.broadcasted_iota(jnp.int32, (c, B_HEADS * c), 1) % c
        self.causal = (aj >= ai) if reverse else (aj <= ai)
        sr = lax.broadcasted_iota(jnp.int32, (PAIR_V, PAIR_QK), 0) // B_DV
        sl = lax.broadcasted_iota(jnp.int32, (PAIR_V, PAIR_QK), 1) // B_DK
        self.state_diag = sr == sl


def _gla_block(q_ref, k_ref, v_ref, la_ref, st, mk, reverse, want_out):
    c = B_CHUNK
    nc = k_ref.shape[0] // c
    order = list(reversed(range(nc))) if reverse else list(range(nc))
    rows = {ci: slice(ci * c, (ci + 1) * c) for ci in order}
    tb = (((1,), (1,)), ((), ()))
    ta = (((0,), (0,)), ((), ()))

    g = {}
    for ci in order:
        la = la_ref[rows[ci], :]
        hi = la.astype(BF16)
        lo = (la - hi.astype(F32)).astype(BF16)
        g[ci] = jnp.dot(mk.tri, hi, preferred_element_type=F32) + jnp.dot(mk.tri, lo, preferred_element_type=F32)

    o = {}
    for g0 in range(0, nc, GLA_GROUP):
        group = order[g0:g0 + GLA_GROUP]
        dec, k_end, q_t, kstack = {}, {}, {}, {}
        for ci in group:
            k = k_ref[rows[ci], :].astype(F32)
            g_last = g[ci][0:1, :] if reverse else g[ci][c - 1:c, :]
            dec[ci] = jnp.exp(g_last)
            k_end[ci] = (k * jnp.exp(g_last - g[ci])).astype(BF16)
            if want_out:
                q_t[ci] = ((q_ref[rows[ci], :].astype(F32) * (B_DK ** -0.5)) * jnp.exp(g[ci])).astype(BF16)
                k_t = k * jnp.exp(-g[ci])
                kstack[ci] = jnp.concatenate([jnp.where(m, k_t, 0.0) for m in mk.k_head], axis=0).astype(BF16)

        ds = {}
        for ci in group:
            v = v_ref[rows[ci], :]
            ds[ci] = [jnp.where(mk.state_diag,
                                lax.dot_general(v[:, p * PAIR_V:(p + 1) * PAIR_V],
                                                k_end[ci][:, p * PAIR_QK:(p + 1) * PAIR_QK], ta,
                                                preferred_element_type=F32), 0.0) for p in range(B_PAIRS)]

        st_prev = {}
        for ci in group:
            st_prev[ci] = st
            st = [st[p] * dec[ci][:, p * PAIR_QK:(p + 1) * PAIR_QK] + ds[ci][p] for p in range(B_PAIRS)]
        if not want_out:
            continue

        att = {}
        for ci in group:
            a = lax.dot_general(q_t[ci], kstack[ci], tb, preferred_element_type=F32)
            att[ci] = jnp.where(mk.causal, a, 0.0).astype(BF16)
        for ci in group:
            v = v_ref[rows[ci], :]
            vstack = jnp.concatenate([jnp.where(m, v, jnp.zeros_like(v)) for m in mk.v_head], axis=0)
            o[ci] = jnp.dot(att[ci], vstack, preferred_element_type=F32) + jnp.concatenate(
                [lax.dot_general(q_t[ci][:, p * PAIR_QK:(p + 1) * PAIR_QK], st_prev[ci][p].astype(BF16), tb,
                                 preferred_element_type=F32) for p in range(B_PAIRS)], axis=1)
    if not want_out:
        return None, st
    return o, st


def _gla_ctx_kernel(k_ref, v_ref, la_ref, st_ref, *, reverse):
    mk = _GlaMasks(reverse)
    st = [jnp.zeros((PAIR_V, PAIR_QK), F32) for _ in range(B_PAIRS)]
    _, st = _gla_block(None, k_ref, v_ref, la_ref, st, mk, reverse, False)
    for p in range(B_PAIRS):
        st_ref[p] = st[p]


def _gla_ctx_state(kb, vb, la, b, t, l, reverse):
    d = 1 if reverse else 0
    ctx0 = (b * t) // l
    return pl.pallas_call(
        functools.partial(_gla_ctx_kernel, reverse=reverse),
        grid=(b,),
        in_specs=[
            pl.BlockSpec((l, B_QK), lambda bi: (ctx0 + bi, 0)),
            pl.BlockSpec((l, B_V), lambda bi: (ctx0 + bi, 0)),
            pl.BlockSpec((l, B_QK), lambda bi: (ctx0 + bi, d)),
        ],
        out_specs=pl.BlockSpec((None, B_PAIRS, PAIR_V, PAIR_QK), lambda bi: (bi, 0, 0, 0)),
        out_shape=jax.ShapeDtypeStruct((b, B_PAIRS, PAIR_V, PAIR_QK), F32),
        compiler_params=_cparams("parallel"),
        name="gla_ctx_state",
    )(kb, vb, la)


def _gla_kernel(q_ref, k_ref, v_ref, la_ref, st0_ref, *rest, reverse):
    if reverse:
        of_ref, r_ref, gg_ref, o_ref, st_ref = rest
    else:
        o_ref, st_ref = rest

    @pl.when(pl.program_id(1) == 0)
    def _():
        st_ref[...] = st0_ref[...]

    mk = _GlaMasks(reverse)
    outs_by_chunk, st = _gla_block(q_ref, k_ref, v_ref, la_ref, [st_ref[p] for p in range(B_PAIRS)],
                                   mk, reverse, True)
    for ci, o in outs_by_chunk.items():
        rows = slice(ci * B_CHUNK, (ci + 1) * B_CHUNK)
        if reverse:
            o = o + of_ref[rows, :]
            outs = []
            for h in range(B_HEADS):
                oh = o[:, h * B_DV:(h + 1) * B_DV]
                oh = (oh * lax.rsqrt(jnp.mean(oh * oh, axis=-1, keepdims=True) + RMS_EPS)) * gg_ref[...]
                outs.append(oh * _silu(r_ref[rows, h * B_DV:(h + 1) * B_DV].astype(F32)))
            o_ref[rows, :] = jnp.concatenate(outs, axis=1).astype(BF16)
        else:
            o_ref[rows, :] = o
    for p in range(B_PAIRS):
        st_ref[p] = st[p]


def _gla(qb, kb, vb, la, st0, b, t, reverse, o_fwd=None, rb=None, gla_g=None):
    tb = TB_GLA
    nt = t // tb
    d = 1 if reverse else 0

    def tok(w, lane_blk=0):
        if reverse:
            return pl.BlockSpec((tb, w), lambda bi, j: (bi * nt + nt - 1 - j, lane_blk))
        return pl.BlockSpec((tb, w), lambda bi, j: (bi * nt + j, lane_blk))

    in_specs = [tok(B_QK), tok(B_QK), tok(B_V), tok(B_QK, d),
                pl.BlockSpec((None, B_PAIRS, PAIR_V, PAIR_QK), lambda bi, j: (bi, 0, 0, 0))]
    args = [qb, kb, vb, la, st0]
    if reverse:
        in_specs += [tok(B_V), tok(B_V), _const_spec((1, B_DV))]
        args += [o_fwd, rb, gla_g.reshape(1, B_DV)]
    return pl.pallas_call(
        functools.partial(_gla_kernel, reverse=reverse),
        grid=(b, nt),
        in_specs=in_specs,
        out_specs=tok(B_V),
        out_shape=jax.ShapeDtypeStruct((b * t, B_V), BF16 if reverse else F32),
        scratch_shapes=[pltpu.VMEM((B_PAIRS, PAIR_V, PAIR_QK), F32)],
        compiler_params=_cparams("parallel", "arbitrary"),
        name="gla_bwd" if reverse else "gla_fwd",
    )(*args)


def _shift_rows(e, d):
    n = e.shape[0]
    return pltpu.roll(e, (-d) % n, 0)


def _pool_kernel(x_ref, xp_ref, xn_ref, mod_ref, g_ref, w_ref, ps_ref, o_ref, *, seq):
    i = pl.program_id(1)
    tm = x_ref.shape[0]
    g = g_ref[...]
    row = pl.program_id(0)
    shift = _mod(mod_ref, 3, row)
    scale = _mod(mod_ref, 4, row)
    x = x_ref[...]
    h = _modulate(x, g, shift, scale)
    hp = jnp.where(i > 0, _modulate(xp_ref[...], g, shift, scale), 0.0)
    hn = jnp.where(i < pl.num_programs(1) - 1, _modulate(xn_ref[...], g, shift, scale), 0.0)
    groups, totals = [], []
    for gi, w in enumerate(POOL_WINDOWS):
        ls = slice(gi * POOL_GROUP, (gi + 1) * POOL_GROUP)
        groups.append(h[:, ls])
        s = jnp.concatenate([hp[:, ls], h[:, ls], hn[:, ls]], axis=0)
        k = 1
        while k < w:
            s = s + _shift_rows(s, -k)
            k *= 2
        if w // 2 > 1:
            s = _shift_rows(s, w // 2 - 1)
        totals.append(s[HALO:HALO + tm, :])

    def finish(scales):
        ys = [jnp.dot((totals[gi] * scales[gi] - groups[gi]).astype(BF16), w_ref[gi], preferred_element_type=F32)
              for gi in range(len(POOL_WINDOWS))]
        y = jnp.concatenate(ys, axis=1) * ps_ref[...]
        o_ref[...] = x + _mod(mod_ref, 5, row) * y

    clipped = (i == 0) | (i == pl.num_programs(1) - 1)

    @pl.when(clipped)
    def _():
        t = i * tm + lax.broadcasted_iota(jnp.int32, (tm, 1), 0)
        finish([1.0 / (jnp.minimum(t + w // 2, seq) - jnp.maximum(t - w // 2, 0)).astype(F32)
                for w in POOL_WINDOWS])

    @pl.when(jnp.logical_not(clipped))
    def _():
        finish([1.0 / w for w in POOL_WINDOWS])


def _pool(x, mods, layer, g, w_pool, pool_scale):
    b, t, d = x.shape
    tm = TM_POOL
    hb = tm // HALO
    last = t // HALO - 1
    return pl.pallas_call(
        functools.partial(_pool_kernel, seq=t),
        grid=(b, t // tm),
        in_specs=[
            pl.BlockSpec((None, tm, d), lambda bi, i: (bi, i, 0)),
            pl.BlockSpec((None, HALO, d), lambda bi, i: (bi, jnp.maximum(i * hb - 1, 0), 0)),
            pl.BlockSpec((None, HALO, d), lambda bi, i: (bi, jnp.minimum((i + 1) * hb, last), 0)),
            _mod_spec(layer, d),
            _const_spec((1, d)),
            _const_spec(w_pool.shape),
            _const_spec((1, d)),
        ],
        out_specs=pl.BlockSpec((None, tm, d), lambda bi, i: (bi, i, 0)),
        out_shape=jax.ShapeDtypeStruct((b, t, d), F32),
        compiler_params=_cparams("parallel", "parallel"),
        name="pool_mixer",
    )(x, x, x, mods, g.reshape(1, d), w_pool, pool_scale.reshape(1, d))


def _rope_tables(t, pad_rows):
    n = A_HEAD_DIM // 4
    freqs = ROPE_BASE ** (-jnp.arange(n, dtype=F32) / n)
    pos = jnp.arange(t)
    rows = (pos // GRID_W).astype(F32)
    cols = (pos % GRID_W).astype(F32)
    ang_r = rows[:, None] * freqs
    ang_c = cols[:, None] * freqs
    cos = jnp.concatenate([jnp.cos(ang_r), jnp.cos(ang_r), jnp.cos(ang_c), jnp.cos(ang_c)], axis=1)
    sin = jnp.concatenate([-jnp.sin(ang_r), jnp.sin(ang_r), -jnp.sin(ang_c), jnp.sin(ang_c)], axis=1)
    cos = jnp.concatenate([jnp.tile(cos, (1, 2)), jnp.ones((pad_rows, LANES), F32)], axis=0)
    sin = jnp.concatenate([jnp.tile(sin, (1, 2)), jnp.zeros((pad_rows, LANES), F32)], axis=0)
    return cos, sin


def kernel(x, c, ctx, c_ctx, w_mod, b_mod, norm_g, ffn1_wi, ffn1_wo, ffn2_wi, ffn2_wo,
           w_in, w_a2_f, b_a_f, w_a2_b, b_a_b, sink, gla_g, w_out, w_pool, pool_scale, final_g):
    b, t, d = x.shape
    lctx = ctx.shape[1]
    n_tok = b * t
    ctx_row = b

    cond = jnp.zeros((COND_ROWS, d), F32).at[:b].set(c).at[ctx_row].set(c_ctx)
    mods = _adaln(cond, w_mod, b_mod)
    x = x.reshape(n_tok, d)

    xc = _ffn(x, n_tok, t, mods, norm_g[0, 0], ffn1_wi, ffn1_wo, 0, 0,
              ctx=ctx.reshape(b * lctx, d), ctx_row=ctx_row)

    w_main = w_in[0][:, :PROJ_MAIN].astype(BF16)
    w_zg = jnp.zeros((d, LANES), F32).at[:, :2 * B_GATE_RANK].set(w_in[0][:, PROJ_MAIN:]).astype(BF16)
    w2 = jnp.zeros((LANES, 2 * B_QK), F32)
    w2 = w2.at[:B_GATE_RANK, :B_QK].set(w_a2_f[0]).at[B_GATE_RANK:2 * B_GATE_RANK, B_QK:].set(w_a2_b[0]).astype(BF16)
    b2 = jnp.concatenate([b_a_f[0], b_a_b[0]]).reshape(1, 2 * B_QK)
    cos, sin = _rope_tables(t, TM_PROJ)
    qa, kv, qb, kb, vb, rb, la = _proj(xc, n_tok, t, ctx_row, mods, 0, norm_g[0, 1], w_main, w_zg, w2, b2, cos, sin)

    o_a = _attention(sink[0], qa, kv, b, t, lctx)
    st_f = _gla_ctx_state(kb, vb, la, b, t, lctx, False)
    st_b = _gla_ctx_state(kb, vb, la, b, t, lctx, True)
    o_f = _gla(qb, kb, vb, la, st_f, b, t, False)
    o_g = _gla(qb, kb, vb, la, st_b, b, t, True, o_f, rb, gla_g[0])
    x = _ffn(xc, n_tok, t, mods, norm_g[0, 2], ffn2_wi, ffn2_wo, 0, 6, mix=(o_a, o_g, w_out[0].astype(BF16)))

    x = _ffn(x, n_tok, t, mods, norm_g[1, 0], ffn1_wi, ffn1_wo, 1, 0)
    x = _pool(x.reshape(b, t, d), mods, 1, norm_g[1, 1], w_pool[0].astype(BF16), pool_scale[0])
    x = _ffn(x.reshape(n_tok, d), n_tok, t, mods, norm_g[1, 2], ffn2_wi, ffn2_wo, 1, 6, final_g=final_g)
    return x.reshape(b, t, d)
```

```python
import functools

import jax
import jax.numpy as jnp
import numpy as np
from jax import lax
from jax.experimental import pallas as pl
from jax.experimental.pallas import tpu as pltpu

F32 = jnp.float32
BF16 = jnp.bfloat16

D_MODEL = 1024
N_MOD = 9
D_FF = 2816
RMS_EPS = 1e-6
GRID_W = 64
A_HEADS = 8
A_KV_HEADS = 2
A_HEAD_DIM = 64
WINDOW = 128
ROPE_BASE = 10000.0
B_HEADS = 4
B_DK = 64
B_DV = 128
B_GATE_RANK = 16
B_GATE_NORM = 16.0
B_CHUNK = 64
POOL_WINDOWS = (2, 4, 8, 16)
POOL_GROUP = D_MODEL // len(POOL_WINDOWS)
A_Q = A_HEADS * A_HEAD_DIM
A_KV = A_KV_HEADS * A_HEAD_DIM
B_QK = B_HEADS * B_DK
B_V = B_HEADS * B_DV
PROJ_MAIN = A_Q + 2 * A_KV + 2 * B_QK + 2 * B_V
PROJ_STAGE_COLS = 256
KV_DUP = 4 * A_KV

LOG2E = 1.4426950408889634
LANES = 128
COND_ROWS = 8
VMEM_LIMIT = 56 * 1024 * 1024

TM_FFN = 1024
SUB_FFN = 256
TM_PROJ = 1024
SUB_PROJ = 256
TM_POOL = 512
TB_GLA = 1024
GLA_GROUP = 4
HALO = 8


def _cparams(*sem):
    return pltpu.CompilerParams(dimension_semantics=sem, vmem_limit_bytes=VMEM_LIMIT)


def _const_spec(shape):
    nd = len(shape)
    return pl.BlockSpec(shape, lambda *_: (0,) * nd, pipeline_mode=pl.Buffered(1))


def _modulate(x, g, shift, scale):
    y = x * lax.rsqrt(jnp.mean(x * x, axis=-1, keepdims=True) + RMS_EPS)
    return (y * g) * (1.0 + scale) + shift


def _silu(a):
    return a * jax.nn.sigmoid(a)


ADALN_MODS_PER_STEP = 3


def _adaln_kernel(c_ref, w_ref, b_ref, o_ref):
    d = c_ref.shape[1]
    s = _silu(c_ref[...]).astype(BF16)
    r = jnp.dot(s, w_ref[...].astype(BF16), preferred_element_type=F32) + b_ref[...]
    for k in range(ADALN_MODS_PER_STEP):
        o_ref[k] = r[:, k * d:(k + 1) * d]


def _adaln(cond, w_mod, b_mod):
    depth, d, n = w_mod.shape
    per = ADALN_MODS_PER_STEP
    tn = per * d
    return pl.pallas_call(
        _adaln_kernel,
        grid=(depth, N_MOD // per),
        in_specs=[
            pl.BlockSpec((COND_ROWS, d), lambda l, j: (0, 0)),
            pl.BlockSpec((None, d, tn), lambda l, j: (l, 0, j)),
            pl.BlockSpec((None, 1, tn), lambda l, j: (l, 0, j)),
        ],
        out_specs=pl.BlockSpec((None, per, COND_ROWS, d), lambda l, j: (l, j, 0, 0)),
        out_shape=jax.ShapeDtypeStruct((depth, N_MOD, COND_ROWS, d), F32),
        compiler_params=_cparams("parallel", "parallel"),
        name="adaln",
    )(cond, w_mod, b_mod.reshape(depth, 1, n))


def _mod_spec(layer, d):
    return pl.BlockSpec((None, N_MOD, COND_ROWS, d), lambda *_: (layer, 0, 0, 0))


def _mod(mod_ref, idx, row):
    return mod_ref[idx, pl.ds(row, 1), :]


FFN_WCHUNK = 256
FFN_NCH = D_FF // FFN_WCHUNK
FFN_CHUNKS_PER_STEP = 2
FFN_NW = 2 * FFN_NCH // FFN_CHUNKS_PER_STEP
FFN_WO_ROWS = D_FF // FFN_NW


def _ffn_tile(x_ref, o_ref, mod_ref, row, g_ref, wi_s, wo_s, mi, mix, fg_ref):
    sub = min(SUB_FFN, x_ref.shape[0])
    nsub = x_ref.shape[0] // sub

    def prologue(s):
        rows = slice(s * sub, (s + 1) * sub)
        x = x_ref[rows, :]
        if mix is not None:
            oa_ref, ob_ref, w_ref = mix
            y0 = (jnp.dot(oa_ref[rows, :], w_ref[:A_Q, :], preferred_element_type=F32)
                  + jnp.dot(ob_ref[rows, :], w_ref[A_Q:, :], preferred_element_type=F32))
            x = x + _mod(mod_ref, 5, row) * y0
        return x, _modulate(x, g_ref[...], _mod(mod_ref, mi, row), _mod(mod_ref, mi + 1, row)).astype(BF16)

    def act_chunk(h, c):
        a = jnp.dot(h, wi_s[c], preferred_element_type=F32)
        u = jnp.dot(h, wi_s[FFN_NCH + c], preferred_element_type=F32)
        return (_silu(a) * u).astype(BF16)

    nxt = prologue(0)
    early = []
    for s in range(nsub):
        rows = slice(s * sub, (s + 1) * sub)
        x, h = nxt
        acts, early = early, []
        for c in range(len(acts), FFN_NCH):
            if c == FFN_NCH - 1 and s + 1 < nsub:
                nxt = prologue(s + 1)
                early = [act_chunk(nxt[1], 0)]
            acts.append(act_chunk(h, c))
        y = jnp.dot(jnp.concatenate(acts, axis=1), wo_s[...], preferred_element_type=F32)
        o = x + (0.5 * _mod(mod_ref, mi + 2, row)) * y
        if fg_ref is not None:
            o = (o * lax.rsqrt(jnp.mean(o * o, axis=-1, keepdims=True) + RMS_EPS)) * fg_ref[...]
        o_ref[rows, :] = o


def _ffn_kernel(*refs, mi, nx, tiles_per_batch, ctx_row, has_ctx, has_mix, final):
    refs = list(refs)
    x_ref = refs.pop(0)
    c_ref = refs.pop(0) if has_ctx else None
    mod_ref, g_ref, wi_ref, wo_ref = refs[:4]
    refs = refs[4:]
    mix = tuple(refs[:3]) if has_mix else None
    refs = refs[3:] if has_mix else refs
    fg_ref = refs.pop(0) if final else None
    o_ref, wi_s, wo_s = refs
    j = pl.program_id(0)

    @pl.when(j < FFN_NW)
    def _():
        for k in range(FFN_CHUNKS_PER_STEP):
            wi_s[j * FFN_CHUNKS_PER_STEP + k] = wi_ref[:, k * FFN_WCHUNK:(k + 1) * FFN_WCHUNK].astype(BF16)
        wo_s[pl.ds(pl.multiple_of(j * FFN_WO_ROWS, FFN_WO_ROWS), FFN_WO_ROWS), :] = wo_ref[...].astype(BF16)

    @pl.when((j >= FFN_NW) & (j < FFN_NW + nx))
    def _():
        _ffn_tile(x_ref, o_ref, mod_ref, (j - FFN_NW) // tiles_per_batch, g_ref, wi_s, wo_s, mi, mix, fg_ref)

    if has_ctx:
        @pl.when(j >= FFN_NW + nx)
        def _():
            _ffn_tile(c_ref, o_ref, mod_ref, ctx_row, g_ref, wi_s, wo_s, mi, None, fg_ref)


def _ffn(x, n_rows, tok_per_batch, mods, g, wi, wo, layer, mi, *, ctx=None, ctx_row=None, mix=None, final_g=None):
    d = x.shape[1]
    tm = TM_FFN
    nx = n_rows // tm
    nc = 0 if ctx is None else ctx.shape[0] // tm
    tiles_per_batch = tok_per_batch // tm

    def tile(j):
        return jnp.clip(j - FFN_NW, 0, nx + nc - 1)

    def x_tile(j):
        return jnp.minimum(tile(j), nx - 1)

    def wchunk(j):
        return jnp.minimum(j, FFN_NW - 1)

    in_specs = [pl.BlockSpec((tm, d), lambda j: (x_tile(j), 0))]
    args = [x]
    if ctx is not None:
        in_specs.append(pl.BlockSpec((tm, d), lambda j: (jnp.clip(j - FFN_NW - nx, 0, nc - 1), 0)))
        args.append(ctx)
    in_specs += [
        _mod_spec(layer, d),
        _const_spec((1, d)),
        pl.BlockSpec((None, d, FFN_WCHUNK * FFN_CHUNKS_PER_STEP), lambda j: (layer, 0, wchunk(j))),
        pl.BlockSpec((None, FFN_WO_ROWS, d), lambda j: (layer, wchunk(j), 0)),
    ]
    args += [mods, g.reshape(1, d), wi, wo]
    if mix is not None:
        oa, ob, w_out = mix
        in_specs += [pl.BlockSpec((tm, A_Q), lambda j: (x_tile(j), 0)),
                     pl.BlockSpec((tm, B_V), lambda j: (x_tile(j), 0)),
                     _const_spec(w_out.shape)]
        args += [oa, ob, w_out]
    if final_g is not None:
        in_specs.append(_const_spec((1, d)))
        args.append(final_g.reshape(1, d))
    return pl.pallas_call(
        functools.partial(_ffn_kernel, mi=mi, nx=nx, tiles_per_batch=tiles_per_batch, ctx_row=ctx_row,
                          has_ctx=ctx is not None, has_mix=mix is not None,
                          final=final_g is not None),
        grid=(FFN_NW + nx + nc,),
        in_specs=in_specs,
        out_specs=pl.BlockSpec((tm, d), lambda j: (tile(j), 0)),
        out_shape=jax.ShapeDtypeStruct(((nx + nc) * tm, d), F32),
        scratch_shapes=[pltpu.VMEM((2 * FFN_NCH, d, FFN_WCHUNK), BF16), pltpu.VMEM((D_FF, d), BF16)],
        compiler_params=_cparams("arbitrary"),
        name="ffn",
    )(*args)


def _swap16(x):
    lane = lax.broadcasted_iota(jnp.int32, x.shape, 1)
    return jnp.where((lane & 16) == 0, pltpu.roll(x, LANES - 16, 1), pltpu.roll(x, 16, 1))


def _dup_halves(x):
    lane = lax.broadcasted_iota(jnp.int32, x.shape, 1)
    r = pltpu.roll(x, LANES // 2, 1)
    lo = lane < LANES // 2
    return jnp.where(lo, x, r), jnp.where(lo, r, x)


def _halves_with_ones(x):
    lane = lax.broadcasted_iota(jnp.int32, x.shape, 1)
    r = pltpu.roll(x, LANES // 2, 1)
    lo = lane < LANES // 2
    return jnp.where(lo, x, 1.0), jnp.where(lo, r, 1.0)


def _proj_kernel(x_ref, mod_ref, g_ref, w_ref, w2_ref, b2_ref, cos_ref, sin_ref,
                 qa_ref, kv_ref, qb_ref, kb_ref, vb_ref, rb_ref, la_ref, w_s, wz_s, *, nx, tiles_per_batch, ctx_row):
    i = pl.program_id(0)
    row = jnp.where(i < nx, i // tiles_per_batch, ctx_row)

    @pl.when(i == 0)
    def _():
        for c0 in range(0, PROJ_MAIN, PROJ_STAGE_COLS):
            w_s[:, c0:c0 + PROJ_STAGE_COLS] = w_ref[:, c0:c0 + PROJ_STAGE_COLS].astype(BF16)
        wz_s[...] = jnp.zeros(wz_s.shape, BF16)
        wz_s[:, :2 * B_GATE_RANK] = w_ref[:, PROJ_MAIN:].astype(BF16)

    sub = min(SUB_PROJ, x_ref.shape[0])
    nsub = x_ref.shape[0] // sub
    scale = A_HEAD_DIM ** -0.5 * LOG2E
    o_b = A_Q + 2 * A_KV

    def prologue(s):
        rows = slice(s * sub, (s + 1) * sub)
        return _modulate(x_ref[rows, :], g_ref[...], _mod(mod_ref, 3, row), _mod(mod_ref, 4, row)).astype(BF16)

    def cols(h, lo, hi):
        return jnp.dot(h, w_s[:, lo:hi], preferred_element_type=F32)

    def rope(v, rows):
        return v * cos_ref[rows, :] + _swap16(v) * sin_ref[rows, :]

    def attn_q(h, rows):
        z = cols(h, 0, A_Q)
        for p in range(A_Q // LANES):
            q = z[:, p * LANES:(p + 1) * LANES]
            qa_ref[rows, p * LANES:(p + 1) * LANES] = (rope(q, rows) * scale).astype(BF16)

    def attn_kv(h, rows):
        z = cols(h, A_Q, A_Q + 2 * A_KV)
        k0, k1 = _dup_halves(rope(z[:, :A_KV], rows))
        v0, v1 = _halves_with_ones(z[:, A_KV:])
        for j, piece in enumerate((k0, k1, v0, v1)):
            kv_ref[rows, j * LANES:(j + 1) * LANES] = piece.astype(BF16)

    def gla_qk(h, rows):
        z = cols(h, o_b, o_b + 2 * B_QK)
        qb_ref[rows, :] = z[:, :B_QK].astype(BF16)
        kb_ref[rows, :] = z[:, B_QK:].astype(BF16)

    def gla_v(h, rows):
        vb_ref[rows, :] = cols(h, o_b + 2 * B_QK, o_b + 2 * B_QK + B_V).astype(BF16)

    def gla_r(h, rows):
        rb_ref[rows, :] = cols(h, o_b + 2 * B_QK + B_V, o_b + 2 * B_QK + 2 * B_V).astype(BF16)

    def gla_gate(h, rows):
        zg = jnp.dot(h, wz_s[...], preferred_element_type=F32).astype(BF16)
        pre = jnp.dot(zg, w2_ref[...], preferred_element_type=F32) + b2_ref[...]
        la_ref[rows, :] = (jnp.minimum(pre, 0.0) - jnp.log1p(jnp.exp(-jnp.abs(pre)))) / B_GATE_NORM

    stages = (gla_gate, attn_q, attn_kv, gla_qk, gla_v, gla_r)
    h_next = prologue(0)
    first_done = False
    for s in range(nsub):
        rows = slice(s * sub, (s + 1) * sub)
        h = h_next
        begin, first_done = (1 if first_done else 0), False
        for k in range(begin, len(stages)):
            if k == len(stages) - 1 and s + 1 < nsub:
                h_next = prologue(s + 1)
                stages[0](h_next, slice((s + 1) * sub, (s + 2) * sub))
                first_done = True
            stages[k](h, rows)


def _proj(xc, n_tok, tok_per_batch, ctx_row, mods, layer, g, w_in, w2, b2, cos, sin):
    rows, d = xc.shape
    tm = TM_PROJ
    nx = n_tok // tm
    tiles_per_batch = tok_per_batch // tm
    widths = [(A_Q, BF16), (KV_DUP, BF16), (B_QK, BF16), (B_QK, BF16),
              (B_V, BF16), (B_V, BF16), (2 * B_QK, F32)]

    def pos_blk(i):
        return jnp.where(i < nx, i % tiles_per_batch, tiles_per_batch)

    return pl.pallas_call(
        functools.partial(_proj_kernel, nx=nx, tiles_per_batch=tiles_per_batch, ctx_row=ctx_row),
        grid=(rows // tm,),
        in_specs=[
            pl.BlockSpec((tm, d), lambda i: (i, 0)),
            _mod_spec(layer, d),
            _const_spec((1, d)),
            pl.BlockSpec((None,) + w_in.shape[1:], lambda *_: (layer // 2, 0, 0), pipeline_mode=pl.Buffered(1)),
            _const_spec(w2.shape),
            _const_spec(b2.shape),
            pl.BlockSpec((tm, LANES), lambda i: (pos_blk(i), 0)),
            pl.BlockSpec((tm, LANES), lambda i: (pos_blk(i), 0)),
        ],
        out_specs=[pl.BlockSpec((tm, w), lambda i: (i, 0)) for w, _ in widths],
        out_shape=[jax.ShapeDtypeStruct((rows, w), dt) for w, dt in widths],
        scratch_shapes=[pltpu.VMEM((d, PROJ_MAIN), BF16), pltpu.VMEM((d, LANES), BF16)],
        compiler_params=_cparams("arbitrary"),
        name="mixer_proj",
    )(xc, mods, g.reshape(1, d), w_in, w2, b2, cos, sin)


ATTN_QBLOCKS = 8


def _attn_kernel(sink_ref, q_ref, kvp_ref, kvc_ref, kvn_ref, ckv_ref, o_ref, *, nsteps):
    n = pl.program_id(1)
    blk = WINDOW
    row = lax.broadcasted_iota(jnp.int32, (blk, blk), 0)
    col = lax.broadcasted_iota(jnp.int32, (blk, blk), 1)
    ninf = jnp.float32(-jnp.inf)
    lane = lax.broadcasted_iota(jnp.int32, (blk, LANES), 1)
    lo = lane < LANES // 2
    rep = A_HEADS // A_KV_HEADS

    def key_block(i, lanes):
        if i == 0:
            return kvp_ref[:, lanes]
        if i == ATTN_QBLOCKS + 1:
            return kvn_ref[:, lanes]
        return kvc_ref[(i - 1) * blk:i * blk, lanes]

    def band(qb, lanes):
        return jnp.concatenate([key_block(qb + j, lanes) for j in range(3)] + [ckv_ref[:, lanes]], axis=0)

    kd = [[band(qb, slice(g * LANES, (g + 1) * LANES)) for g in range(A_KV_HEADS)] for qb in range(ATTN_QBLOCKS)]
    vd = [[band(qb, slice((A_KV_HEADS + g) * LANES, (A_KV_HEADS + g + 1) * LANES)) for g in range(A_KV_HEADS)]
          for qb in range(ATTN_QBLOCKS)]
    nkb = kd[0][0].shape[0] // blk
    bias_prev = [jnp.where((col >= row) & ((n > 0) if qb == 0 else True), 0.0, ninf) for qb in range(ATTN_QBLOCKS)]
    bias_next = [jnp.where((col <= row) & ((n < nsteps - 1) if qb == ATTN_QBLOCKS - 1 else True), 0.0, ninf)
                 for qb in range(ATTN_QBLOCKS)]

    def scores(qb, p):
        pair = q_ref[qb * blk:(qb + 1) * blk, p * LANES:(p + 1) * LANES]
        zero = jnp.zeros_like(pair)
        lhs = jnp.concatenate([jnp.where(lo, pair, zero), jnp.where(lo, zero, pair)], axis=0)
        return lax.dot_general(lhs, kd[qb][(2 * p) // rep], (((1,), (1,)), ((), ())), preferred_element_type=F32)

    def softmax(qb, p, s):
        ps, sinks = [], []
        for r in range(2):
            sk = sink_ref[2 * p + r] * LOG2E
            cb = [s[r * blk:(r + 1) * blk, j * blk:(j + 1) * blk] for j in range(nkb)]
            cb[0] = cb[0] + bias_prev[qb]
            cb[2] = cb[2] + bias_next[qb]
            mx = cb[0]
            for c in cb[1:]:
                mx = jnp.maximum(mx, c)
            m = jnp.maximum(jnp.max(mx, axis=-1, keepdims=True), sk)
            ps.append(jnp.concatenate([jnp.exp2((c - m).astype(BF16)) for c in cb], axis=1))
            sinks.append(jnp.exp2(sk - m))
        return jnp.concatenate(ps, axis=0), sinks

    def weighted_values(qb, p, probs, sinks):
        o = jnp.dot(probs, vd[qb][(2 * p) // rep], preferred_element_type=F32)
        o_even, o_odd = o[:blk, :], o[blk:, :]
        num = jnp.where(lo, o_even, pltpu.roll(o_odd, LANES // 2, 1))
        den = jnp.where(lo, pltpu.roll(o_even, LANES // 2, 1) + sinks[0], o_odd + sinks[1])
        o_ref[qb * blk:(qb + 1) * blk, p * LANES:(p + 1) * LANES] = (num / den).astype(BF16)

    units = [(qb, p) for qb in range(ATTN_QBLOCKS) for p in range(A_HEADS // 2)]
    s_next = scores(*units[0])
    for i, unit in enumerate(units):
        s_cur = s_next
        if i + 1 < len(units):
            s_next = scores(*units[i + 1])
        probs, sinks = softmax(*unit, s_cur)
        weighted_values(*unit, probs, sinks)


def _attention(sink, qa, kv, b, t, lctx):
    nb = t // WINDOW
    tq = ATTN_QBLOCKS * WINDOW
    nsteps = t // tq
    ctx0 = (b * t) // lctx

    def edge(off):
        return pl.BlockSpec((WINDOW, KV_DUP),
                            lambda bi, n: (bi * nb + jnp.clip(n * ATTN_QBLOCKS + off, 0, nb - 1), 0))

    return pl.pallas_call(
        functools.partial(_attn_kernel, nsteps=nsteps),
        grid=(b, nsteps),
        in_specs=[
            pl.BlockSpec(memory_space=pltpu.SMEM),
            pl.BlockSpec((tq, A_Q), lambda bi, n: (bi * nsteps + n, 0)),
            edge(-1),
            pl.BlockSpec((tq, KV_DUP), lambda bi, n: (bi * nsteps + n, 0)),
            edge(ATTN_QBLOCKS),
            pl.BlockSpec((lctx, KV_DUP), lambda bi, n: (ctx0 + bi, 0)),
        ],
        out_specs=pl.BlockSpec((tq, A_Q), lambda bi, n: (bi * nsteps + n, 0)),
        out_shape=jax.ShapeDtypeStruct((b * t, A_Q), BF16),
        compiler_params=_cparams("parallel", "parallel"),
        name="window_attn",
    )(sink, qa, kv, kv, kv, kv)


B_PAIRS = B_HEADS // 2
PAIR_QK = 2 * B_DK
PAIR_V = 2 * B_DV


class _GlaMasks:
    def __init__(self, reverse):
        c = B_CHUNK
        i = lax.broadcasted_iota(jnp.int32, (c, c), 0)
        j = lax.broadcasted_iota(jnp.int32, (c, c), 1)
        self.tri = jnp.where((j >= i) if reverse else (j <= i), 1.0, 0.0).astype(BF16)
        lane_qk = lax.broadcasted_iota(jnp.int32, (c, B_QK), 1)
        self.k_head = [(lane_qk >= h * B_DK) & (lane_qk < (h + 1) * B_DK) for h in range(B_HEADS)]
        lane_v = lax.broadcasted_iota(jnp.int32, (c, B_V), 1)
        self.v_head = [(lane_v >= h * B_DV) & (lane_v < (h + 1) * B_DV) for h in range(B_HEADS)]
        ai = lax.broadcasted_iota(jnp.int32, (c, B_HEADS * c), 0)
        aj = lax.broadcasted_iota(jnp.int32, (c, B_HEADS * c), 1) % c
        self.causal = (aj >= ai) if reverse else (aj <= ai)
        sr = lax.broadcasted_iota(jnp.int32, (PAIR_V, PAIR_QK), 0) // B_DV
        sl = lax.broadcasted_iota(jnp.int32, (PAIR_V, PAIR_QK), 1) // B_DK
        self.state_diag = sr == sl


def _gla_block(q_ref, k_ref, v_ref, la_ref, st, mk, reverse, want_out):
    c = B_CHUNK
    nc = k_ref.shape[0] // c
    order = list(reversed(range(nc))) if reverse else list(range(nc))
    rows = {ci: slice(ci * c, (ci + 1) * c) for ci in order}
    tb = (((1,), (1,)), ((), ()))
    ta = (((0,), (0,)), ((), ()))

    g = {}
    for ci in order:
        la = la_ref[rows[ci], :]
        hi = la.astype(BF16)
        lo = (la - hi.astype(F32)).astype(BF16)
        g[ci] = jnp.dot(mk.tri, hi, preferred_element_type=F32) + jnp.dot(mk.tri, lo, preferred_element_type=F32)

    o = {}
    for g0 in range(0, nc, GLA_GROUP):
        group = order[g0:g0 + GLA_GROUP]
        dec, k_end, q_t, kstack = {}, {}, {}, {}
        for ci in group:
            k = k_ref[rows[ci], :].astype(F32)
            g_last = g[ci][0:1, :] if reverse else g[ci][c - 1:c, :]
            dec[ci] = jnp.exp(g_last)
            k_end[ci] = (k * jnp.exp(g_last - g[ci])).astype(BF16)
            if want_out:
                q_t[ci] = ((q_ref[rows[ci], :].astype(F32) * (B_DK ** -0.5)) * jnp.exp(g[ci])).astype(BF16)
                k_t = k * jnp.exp(-g[ci])
                kstack[ci] = jnp.concatenate([jnp.where(m, k_t, 0.0) for m in mk.k_head], axis=0).astype(BF16)

        ds = {}
        for ci in group:
            v = v_ref[rows[ci], :]
            ds[ci] = [jnp.where(mk.state_diag,
                                lax.dot_general(v[:, p * PAIR_V:(p + 1) * PAIR_V],
                                                k_end[ci][:, p * PAIR_QK:(p + 1) * PAIR_QK], ta,
                                                preferred_element_type=F32), 0.0) for p in range(B_PAIRS)]

        st_prev = {}
        for ci in group:
            st_prev[ci] = st
            st = [st[p] * dec[ci][:, p * PAIR_QK:(p + 1) * PAIR_QK] + ds[ci][p] for p in range(B_PAIRS)]
        if not want_out:
            continue

        att = {}
        for ci in group:
            a = lax.dot_general(q_t[ci], kstack[ci], tb, preferred_element_type=F32)
            att[ci] = jnp.where(mk.causal, a, 0.0).astype(BF16)
        for ci in group:
            v = v_ref[rows[ci], :]
            vstack = jnp.concatenate([jnp.where(m, v, jnp.zeros_like(v)) for m in mk.v_head], axis=0)
            o[ci] = jnp.dot(att[ci], vstack, preferred_element_type=F32) + jnp.concatenate(
                [lax.dot_general(q_t[ci][:, p * PAIR_QK:(p + 1) * PAIR_QK], st_prev[ci][p].astype(BF16), tb,
                                 preferred_element_type=F32) for p in range(B_PAIRS)], axis=1)
    if not want_out:
        return None, st
    return o, st


def _gla_ctx_kernel(k_ref, v_ref, la_ref, st_ref, *, reverse):
    mk = _GlaMasks(reverse)
    st = [jnp.zeros((PAIR_V, PAIR_QK), F32) for _ in range(B_PAIRS)]
    _, st = _gla_block(None, k_ref, v_ref, la_ref, st, mk, reverse, False)
    for p in range(B_PAIRS):
        st_ref[p] = st[p]


def _gla_ctx_state(kb, vb, la, b, t, l, reverse):
    d = 1 if reverse else 0
    ctx0 = (b * t) // l
    return pl.pallas_call(
        functools.partial(_gla_ctx_kernel, reverse=reverse),
        grid=(b,),
        in_specs=[
            pl.BlockSpec((l, B_QK), lambda bi: (ctx0 + bi, 0)),
            pl.BlockSpec((l, B_V), lambda bi: (ctx0 + bi, 0)),
            pl.BlockSpec((l, B_QK), lambda bi: (ctx0 + bi, d)),
        ],
        out_specs=pl.BlockSpec((None, B_PAIRS, PAIR_V, PAIR_QK), lambda bi: (bi, 0, 0, 0)),
        out_shape=jax.ShapeDtypeStruct((b, B_PAIRS, PAIR_V, PAIR_QK), F32),
        compiler_params=_cparams("parallel"),
        name="gla_ctx_state",
    )(kb, vb, la)


def _gla_kernel(q_ref, k_ref, v_ref, la_ref, st0_ref, *rest, reverse):
    if reverse:
        of_ref, r_ref, gg_ref, o_ref, st_ref = rest
    else:
        o_ref, st_ref = rest

    @pl.when(pl.program_id(1) == 0)
    def _():
        st_ref[...] = st0_ref[...]

    mk = _GlaMasks(reverse)
    outs_by_chunk, st = _gla_block(q_ref, k_ref, v_ref, la_ref, [st_ref[p] for p in range(B_PAIRS)],
                                   mk, reverse, True)
    for ci, o in outs_by_chunk.items():
        rows = slice(ci * B_CHUNK, (ci + 1) * B_CHUNK)
        if reverse:
            o = o + of_ref[rows, :]
            outs = []
            for h in range(B_HEADS):
                oh = o[:, h * B_DV:(h + 1) * B_DV]
                oh = (oh * lax.rsqrt(jnp.mean(oh * oh, axis=-1, keepdims=True) + RMS_EPS)) * gg_ref[...]
                outs.append(oh * _silu(r_ref[rows, h * B_DV:(h + 1) * B_DV].astype(F32)))
            o_ref[rows, :] = jnp.concatenate(outs, axis=1).astype(BF16)
        else:
            o_ref[rows, :] = o
    for p in range(B_PAIRS):
        st_ref[p] = st[p]


def _gla(qb, kb, vb, la, st0, b, t, reverse, o_fwd=None, rb=None, gla_g=None):
    tb = TB_GLA
    nt = t // tb
    d = 1 if reverse else 0

    def tok(w, lane_blk=0):
        if reverse:
            return pl.BlockSpec((tb, w), lambda bi, j: (bi * nt + nt - 1 - j, lane_blk))
        return pl.BlockSpec((tb, w), lambda bi, j: (bi * nt + j, lane_blk))

    in_specs = [tok(B_QK), tok(B_QK), tok(B_V), tok(B_QK, d),
                pl.BlockSpec((None, B_PAIRS, PAIR_V, PAIR_QK), lambda bi, j: (bi, 0, 0, 0))]
    args = [qb, kb, vb, la, st0]
    if reverse:
        in_specs += [tok(B_V), tok(B_V), _const_spec((1, B_DV))]
        args += [o_fwd, rb, gla_g.reshape(1, B_DV)]
    return pl.pallas_call(
        functools.partial(_gla_kernel, reverse=reverse),
        grid=(b, nt),
        in_specs=in_specs,
        out_specs=tok(B_V),
        out_shape=jax.ShapeDtypeStruct((b * t, B_V), BF16 if reverse else F32),
        scratch_shapes=[pltpu.VMEM((B_PAIRS, PAIR_V, PAIR_QK), F32)],
        compiler_params=_cparams("parallel", "arbitrary"),
        name="gla_bwd" if reverse else "gla_fwd",
    )(*args)


def _shift_rows(e, d):
    n = e.shape[0]
    return pltpu.roll(e, (-d) % n, 0)


def _pool_kernel(x_ref, xp_ref, xn_ref, mod_ref, g_ref, w_ref, ps_ref, o_ref, *, seq):
    i = pl.program_id(1)
    tm = x_ref.shape[0]
    g = g_ref[...]
    row = pl.program_id(0)
    shift = _mod(mod_ref, 3, row)
    scale = _mod(mod_ref, 4, row)
    x = x_ref[...]
    h = _modulate(x, g, shift, scale)
    hp = jnp.where(i > 0, _modulate(xp_ref[...], g, shift, scale), 0.0)
    hn = jnp.where(i < pl.num_programs(1) - 1, _modulate(xn_ref[...], g, shift, scale), 0.0)
    groups, totals = [], []
    for gi, w in enumerate(POOL_WINDOWS):
        ls = slice(gi * POOL_GROUP, (gi + 1) * POOL_GROUP)
        groups.append(h[:, ls])
        s = jnp.concatenate([hp[:, ls], h[:, ls], hn[:, ls]], axis=0)
        k = 1
        while k < w:
            s = s + _shift_rows(s, -k)
            k *= 2
        if w // 2 > 1:
            s = _shift_rows(s, w // 2 - 1)
        totals.append(s[HALO:HALO + tm, :])

    def finish(scales):
        ys = [jnp.dot((totals[gi] * scales[gi] - groups[gi]).astype(BF16), w_ref[gi], preferred_element_type=F32)
              for gi in range(len(POOL_WINDOWS))]
        y = jnp.concatenate(ys, axis=1) * ps_ref[...]
        o_ref[...] = x + _mod(mod_ref, 5, row) * y

    clipped = (i == 0) | (i == pl.num_programs(1) - 1)

    @pl.when(clipped)
    def _():
        t = i * tm + lax.broadcasted_iota(jnp.int32, (tm, 1), 0)
        finish([1.0 / (jnp.minimum(t + w // 2, seq) - jnp.maximum(t - w // 2, 0)).astype(F32)
                for w in POOL_WINDOWS])

    @pl.when(jnp.logical_not(clipped))
    def _():
        finish([1.0 / w for w in POOL_WINDOWS])


def _pool(x, mods, layer, g, w_pool, pool_scale):
    b, t, d = x.shape
    tm = TM_POOL
    hb = tm // HALO
    last = t // HALO - 1
    return pl.pallas_call(
        functools.partial(_pool_kernel, seq=t),
        grid=(b, t // tm),
        in_specs=[
            pl.BlockSpec((None, tm, d), lambda bi, i: (bi, i, 0)),
            pl.BlockSpec((None, HALO, d), lambda bi, i: (bi, jnp.maximum(i * hb - 1, 0), 0)),
            pl.BlockSpec((None, HALO, d), lambda bi, i: (bi, jnp.minimum((i + 1) * hb, last), 0)),
            _mod_spec(layer, d),
            _const_spec((1, d)),
            _const_spec(w_pool.shape),
            _const_spec((1, d)),
        ],
        out_specs=pl.BlockSpec((None, tm, d), lambda bi, i: (bi, i, 0)),
        out_shape=jax.ShapeDtypeStruct((b, t, d), F32),
        compiler_params=_cparams("parallel", "parallel"),
        name="pool_mixer",
    )(x, x, x, mods, g.reshape(1, d), w_pool, pool_scale.reshape(1, d))


def _rope_tables(t, pad_rows):
    n = A_HEAD_DIM // 4
    freqs = ROPE_BASE ** (-jnp.arange(n, dtype=F32) / n)
    pos = jnp.arange(t)
    rows = (pos // GRID_W).astype(F32)
    cols = (pos % GRID_W).astype(F32)
    ang_r = rows[:, None] * freqs
    ang_c = cols[:, None] * freqs
    cos = jnp.concatenate([jnp.cos(ang_r), jnp.cos(ang_r), jnp.cos(ang_c), jnp.cos(ang_c)], axis=1)
    sin = jnp.concatenate([-jnp.sin(ang_r), jnp.sin(ang_r), -jnp.sin(ang_c), jnp.sin(ang_c)], axis=1)
    cos = jnp.concatenate([jnp.tile(cos, (1, 2)), jnp.ones((pad_rows, LANES), F32)], axis=0)
    sin = jnp.concatenate([jnp.tile(sin, (1, 2)), jnp.zeros((pad_rows, LANES), F32)], axis=0)
    return cos, sin


def kernel(x, c, ctx, c_ctx, w_mod, b_mod, norm_g, ffn1_wi, ffn1_wo, ffn2_wi, ffn2_wo,
           w_in, w_a2_f, b_a_f, w_a2_b, b_a_b, sink, gla_g, w_out, w_pool, pool_scale, final_g):
    b, t, d = x.shape
    lctx = ctx.shape[1]
    n_tok = b * t
    ctx_row = b

    cond = jnp.zeros((COND_ROWS, d), F32).at[:b].set(c).at[ctx_row].set(c_ctx)
    mods = _adaln(cond, w_mod, b_mod)
    x = x.reshape(n_tok, d)

    xc = _ffn(x, n_tok, t, mods, norm_g[0, 0], ffn1_wi, ffn1_wo, 0, 0,
              ctx=ctx.reshape(b * lctx, d), ctx_row=ctx_row)

    w2 = jnp.zeros((LANES, 2 * B_QK), F32)
    w2 = w2.at[:B_GATE_RANK, :B_QK].set(w_a2_f[0]).at[B_GATE_RANK:2 * B_GATE_RANK, B_QK:].set(w_a2_b[0]).astype(BF16)
    b2 = jnp.concatenate([b_a_f[0], b_a_b[0]]).reshape(1, 2 * B_QK)
    cos, sin = _rope_tables(t, TM_PROJ)
    qa, kv, qb, kb, vb, rb, la = _proj(xc, n_tok, t, ctx_row, mods, 0, norm_g[0, 1], w_in, w2, b2, cos, sin)

    o_a = _attention(sink[0], qa, kv, b, t, lctx)
    st_f = _gla_ctx_state(kb, vb, la, b, t, lctx, False)
    st_b = _gla_ctx_state(kb, vb, la, b, t, lctx, True)
    o_f = _gla(qb, kb, vb, la, st_f, b, t, False)
    o_g = _gla(qb, kb, vb, la, st_b, b, t, True, o_f, rb, gla_g[0])
    x = _ffn(xc, n_tok, t, mods, norm_g[0, 2], ffn2_wi, ffn2_wo, 0, 6, mix=(o_a, o_g, w_out[0].astype(BF16)))

    x = _ffn(x, n_tok, t, mods, norm_g[1, 0], ffn1_wi, ffn1_wo, 1, 0)
    x = _pool(x.reshape(b, t, d), mods, 1, norm_g[1, 1], w_pool[0].astype(BF16), pool_scale[0])
    x = _ffn(x.reshape(n_tok, d), n_tok, t, mods, norm_g[1, 2], ffn2_wi, ffn2_wo, 1, 6, final_g=final_g)
    return x.reshape(b, t, d)
```

```python
import functools

import jax
import jax.numpy as jnp
import numpy as np
from jax import lax
from jax.experimental import pallas as pl
from jax.experimental.pallas import tpu as pltpu

F32 = jnp.float32
BF16 = jnp.bfloat16

D_MODEL = 1024
N_MOD = 9
D_FF = 2816
RMS_EPS = 1e-6
GRID_W = 64
A_HEADS = 8
A_KV_HEADS = 2
A_HEAD_DIM = 64
WINDOW = 128
ROPE_BASE = 10000.0
B_HEADS = 4
B_DK = 64
B_DV = 128
B_GATE_RANK = 16
B_GATE_NORM = 16.0
B_CHUNK = 64
POOL_WINDOWS = (2, 4, 8, 16)
POOL_GROUP = D_MODEL // len(POOL_WINDOWS)
A_Q = A_HEADS * A_HEAD_DIM
A_KV = A_KV_HEADS * A_HEAD_DIM
B_QK = B_HEADS * B_DK
B_V = B_HEADS * B_DV
PROJ_MAIN = A_Q + 2 * A_KV + 2 * B_QK + 2 * B_V
KV_DUP = 4 * A_KV

LOG2E = 1.4426950408889634
LANES = 128
COND_ROWS = 8
VMEM_LIMIT = 56 * 1024 * 1024

TM_FFN = 1024
SUB_FFN = 256
TM_FFN_PROJ = 512
TM_POOL = 512
TB_GLA = 1024
GLA_GROUP = 4
HALO = 8


def _cparams(*sem):
    return pltpu.CompilerParams(dimension_semantics=sem, vmem_limit_bytes=VMEM_LIMIT)


def _const_spec(shape):
    nd = len(shape)
    return pl.BlockSpec(shape, lambda *_: (0,) * nd, pipeline_mode=pl.Buffered(1))


def _modulate(x, g, shift, scale):
    y = x * lax.rsqrt(jnp.mean(x * x, axis=-1, keepdims=True) + RMS_EPS)
    return (y * g) * (1.0 + scale) + shift


def _silu(a):
    return a * jax.nn.sigmoid(a)


ADALN_MODS_PER_STEP = 3


def _adaln_kernel(c_ref, w_ref, b_ref, o_ref):
    d = c_ref.shape[1]
    s = _silu(c_ref[...]).astype(BF16)
    r = jnp.dot(s, w_ref[...].astype(BF16), preferred_element_type=F32) + b_ref[...]
    for k in range(ADALN_MODS_PER_STEP):
        o_ref[k] = r[:, k * d:(k + 1) * d]


def _adaln(cond, w_mod, b_mod):
    depth, d, n = w_mod.shape
    per = ADALN_MODS_PER_STEP
    tn = per * d
    return pl.pallas_call(
        _adaln_kernel,
        grid=(depth, N_MOD // per),
        in_specs=[
            pl.BlockSpec((COND_ROWS, d), lambda l, j: (0, 0)),
            pl.BlockSpec((None, d, tn), lambda l, j: (l, 0, j)),
            pl.BlockSpec((None, 1, tn), lambda l, j: (l, 0, j)),
        ],
        out_specs=pl.BlockSpec((None, per, COND_ROWS, d), lambda l, j: (l, j, 0, 0)),
        out_shape=jax.ShapeDtypeStruct((depth, N_MOD, COND_ROWS, d), F32),
        compiler_params=_cparams("parallel", "parallel"),
        name="adaln",
    )(cond, w_mod, b_mod.reshape(depth, 1, n))


def _mod_spec(layer, d):
    return pl.BlockSpec((None, N_MOD, COND_ROWS, d), lambda *_: (layer, 0, 0, 0))


def _mod(mod_ref, idx, row):
    return mod_ref[idx, pl.ds(row, 1), :]


FFN_WCHUNK = 256
FFN_NCH = D_FF // FFN_WCHUNK
FFN_CHUNKS_PER_STEP = 2
FFN_NW = 2 * FFN_NCH // FFN_CHUNKS_PER_STEP
FFN_WO_ROWS = D_FF // FFN_NW


def _ffn_tile(x_ref, o_ref, mod_ref, row, g_ref, wi_s, wo_s, mi, mix, fg_ref, proj):
    sub = min(SUB_FFN, x_ref.shape[0])
    nsub = x_ref.shape[0] // sub

    def prologue(s):
        rows = slice(s * sub, (s + 1) * sub)
        x = x_ref[rows, :]
        if mix is not None:
            oa_ref, ob_ref, w_ref = mix
            y0 = (jnp.dot(oa_ref[rows, :], w_ref[:A_Q, :], preferred_element_type=F32)
                  + jnp.dot(ob_ref[rows, :], w_ref[A_Q:, :], preferred_element_type=F32))
            x = x + _mod(mod_ref, 5, row) * y0
        return x, _modulate(x, g_ref[...], _mod(mod_ref, mi, row), _mod(mod_ref, mi + 1, row)).astype(BF16)

    def act_chunk(h, c):
        a = jnp.dot(h, wi_s[c], preferred_element_type=F32)
        u = jnp.dot(h, wi_s[FFN_NCH + c], preferred_element_type=F32)
        return (_silu(a) * u).astype(BF16)

    nxt = prologue(0)
    early = []
    pending = []
    for s in range(nsub):
        rows = slice(s * sub, (s + 1) * sub)
        x, h = nxt
        acts, early = early, []
        for c in range(len(acts), FFN_NCH):
            if c == FFN_NCH - 1 and s + 1 < nsub:
                nxt = prologue(s + 1)
                early = [act_chunk(nxt[1], 0)]
            acts.append(act_chunk(h, c))
        y = jnp.dot(jnp.concatenate(acts, axis=1), wo_s[...], preferred_element_type=F32)
        o = x + (0.5 * _mod(mod_ref, mi + 2, row)) * y
        if fg_ref is not None:
            o = (o * lax.rsqrt(jnp.mean(o * o, axis=-1, keepdims=True) + RMS_EPS)) * fg_ref[...]
        o_ref[rows, :] = o
        if proj is not None:
            pending.append(
                (_modulate(o, proj[0][...], _mod(mod_ref, 3, row), _mod(mod_ref, 4, row)).astype(BF16), rows))
    for h2, rows in pending:
        for stage in proj[1]:
            stage(h2, rows)


def _ffn_kernel(*refs, mi, nx, tiles_per_batch, ctx_row, has_ctx, has_mix, final, has_proj):
    refs = list(refs)
    x_ref = refs.pop(0)
    c_ref = refs.pop(0) if has_ctx else None
    mod_ref, g_ref, wi_ref, wo_ref = refs[:4]
    refs = refs[4:]
    mix = tuple(refs[:3]) if has_mix else None
    refs = refs[3:] if has_mix else refs
    fg_ref = refs.pop(0) if final else None
    proj = None
    if has_proj:
        g1_ref, wp_ref, wz_ref, w2_ref, b2_ref, cos_ref, sin_ref = refs[:7]
        proj = (g1_ref, _proj_stage_fns(wp_ref, wz_ref, w2_ref, b2_ref, cos_ref, sin_ref, refs[8:8 + N_PROJ_OUT]))
        refs = refs[7:8] + refs[8 + N_PROJ_OUT:]
    o_ref, wi_s, wo_s = refs
    j = pl.program_id(0)

    @pl.when(j < FFN_NW)
    def _():
        for k in range(FFN_CHUNKS_PER_STEP):
            wi_s[j * FFN_CHUNKS_PER_STEP + k] = wi_ref[:, k * FFN_WCHUNK:(k + 1) * FFN_WCHUNK].astype(BF16)
        wo_s[pl.ds(pl.multiple_of(j * FFN_WO_ROWS, FFN_WO_ROWS), FFN_WO_ROWS), :] = wo_ref[...].astype(BF16)

    @pl.when((j >= FFN_NW) & (j < FFN_NW + nx))
    def _():
        _ffn_tile(x_ref, o_ref, mod_ref, (j - FFN_NW) // tiles_per_batch, g_ref, wi_s, wo_s, mi, mix, fg_ref, proj)

    if has_ctx:
        @pl.when(j >= FFN_NW + nx)
        def _():
            _ffn_tile(c_ref, o_ref, mod_ref, ctx_row, g_ref, wi_s, wo_s, mi, None, fg_ref, proj)


def _ffn(x, n_rows, tok_per_batch, mods, g, wi, wo, layer, mi, *, tm=TM_FFN, ctx=None, ctx_row=None, mix=None,
         final_g=None, proj=None):
    d = x.shape[1]
    nx = n_rows // tm
    nc = 0 if ctx is None else ctx.shape[0] // tm
    tiles_per_batch = tok_per_batch // tm

    def tile(j):
        return jnp.clip(j - FFN_NW, 0, nx + nc - 1)

    def x_tile(j):
        return jnp.minimum(tile(j), nx - 1)

    def wchunk(j):
        return jnp.minimum(j, FFN_NW - 1)

    def pos_blk(j):
        return jnp.where(tile(j) < nx, tile(j) % tiles_per_batch, tiles_per_batch)

    in_specs = [pl.BlockSpec((tm, d), lambda j: (x_tile(j), 0))]
    args = [x]
    if ctx is not None:
        in_specs.append(pl.BlockSpec((tm, d), lambda j: (jnp.clip(j - FFN_NW - nx, 0, nc - 1), 0)))
        args.append(ctx)
    in_specs += [
        _mod_spec(layer, d),
        _const_spec((1, d)),
        pl.BlockSpec((None, d, FFN_WCHUNK * FFN_CHUNKS_PER_STEP), lambda j: (layer, 0, wchunk(j))),
        pl.BlockSpec((None, FFN_WO_ROWS, d), lambda j: (layer, wchunk(j), 0)),
    ]
    args += [mods, g.reshape(1, d), wi, wo]
    if mix is not None:
        oa, ob, w_out = mix
        in_specs += [pl.BlockSpec((tm, A_Q), lambda j: (x_tile(j), 0)),
                     pl.BlockSpec((tm, B_V), lambda j: (x_tile(j), 0)),
                     _const_spec(w_out.shape)]
        args += [oa, ob, w_out]
    if final_g is not None:
        in_specs.append(_const_spec((1, d)))
        args.append(final_g.reshape(1, d))
    n_out_rows = (nx + nc) * tm
    out_specs = [pl.BlockSpec((tm, d), lambda j: (tile(j), 0))]
    out_shape = [jax.ShapeDtypeStruct((n_out_rows, d), F32)]
    if proj is not None:
        g1, w_main, w_zg, w2, b2, cos, sin = proj
        in_specs += [_const_spec((1, d)), _const_spec(w_main.shape), _const_spec(w_zg.shape), _const_spec(w2.shape),
                     _const_spec(b2.shape),
                     pl.BlockSpec((tm, LANES), lambda j: (pos_blk(j), 0)),
                     pl.BlockSpec((tm, LANES), lambda j: (pos_blk(j), 0))]
        args += [g1.reshape(1, d), w_main, w_zg, w2, b2, cos, sin]
        out_specs += [pl.BlockSpec((tm, w), lambda j: (tile(j), 0)) for w, _ in PROJ_OUTS]
        out_shape += [jax.ShapeDtypeStruct((n_out_rows, w), dt) for w, dt in PROJ_OUTS]
    res = pl.pallas_call(
        functools.partial(_ffn_kernel, mi=mi, nx=nx, tiles_per_batch=tiles_per_batch, ctx_row=ctx_row,
                          has_ctx=ctx is not None, has_mix=mix is not None,
                          final=final_g is not None, has_proj=proj is not None),
        grid=(FFN_NW + nx + nc,),
        in_specs=in_specs,
        out_specs=out_specs,
        out_shape=out_shape,
        scratch_shapes=[pltpu.VMEM((2 * FFN_NCH, d, FFN_WCHUNK), BF16), pltpu.VMEM((D_FF, d), BF16)],
        compiler_params=_cparams("arbitrary"),
        name="ffn_proj" if proj is not None else "ffn",
    )(*args)
    return res if proj is not None else res[0]


def _swap16(x):
    lane = lax.broadcasted_iota(jnp.int32, x.shape, 1)
    return jnp.where((lane & 16) == 0, pltpu.roll(x, LANES - 16, 1), pltpu.roll(x, 16, 1))


def _dup_halves(x):
    lane = lax.broadcasted_iota(jnp.int32, x.shape, 1)
    r = pltpu.roll(x, LANES // 2, 1)
    lo = lane < LANES // 2
    return jnp.where(lo, x, r), jnp.where(lo, r, x)


def _halves_with_ones(x):
    lane = lax.broadcasted_iota(jnp.int32, x.shape, 1)
    r = pltpu.roll(x, LANES // 2, 1)
    lo = lane < LANES // 2
    return jnp.where(lo, x, 1.0), jnp.where(lo, r, 1.0)


PROJ_OUTS = ((A_Q, BF16), (KV_DUP, BF16), (B_QK, BF16), (B_QK, BF16), (B_V, BF16), (B_V, BF16), (2 * B_QK, F32))
N_PROJ_OUT = len(PROJ_OUTS)


def _proj_stage_fns(w_ref, wz_ref, w2_ref, b2_ref, cos_ref, sin_ref, outs):
    qa_ref, kv_ref, qb_ref, kb_ref, vb_ref, rb_ref, la_ref = outs
    scale = A_HEAD_DIM ** -0.5 * LOG2E
    o_b = A_Q + 2 * A_KV

    def cols(h, lo, hi):
        return jnp.dot(h, w_ref[:, lo:hi], preferred_element_type=F32)

    def rope(v, rows):
        return v * cos_ref[rows, :] + _swap16(v) * sin_ref[rows, :]

    def attn_q(h, rows):
        z = cols(h, 0, A_Q)
        for p in range(A_Q // LANES):
            q = z[:, p * LANES:(p + 1) * LANES]
            qa_ref[rows, p * LANES:(p + 1) * LANES] = (rope(q, rows) * scale).astype(BF16)

    def attn_kv(h, rows):
        z = cols(h, A_Q, A_Q + 2 * A_KV)
        k0, k1 = _dup_halves(rope(z[:, :A_KV], rows))
        v0, v1 = _halves_with_ones(z[:, A_KV:])
        for j, piece in enumerate((k0, k1, v0, v1)):
            kv_ref[rows, j * LANES:(j + 1) * LANES] = piece.astype(BF16)

    def gla_qk(h, rows):
        z = cols(h, o_b, o_b + 2 * B_QK)
        qb_ref[rows, :] = z[:, :B_QK].astype(BF16)
        kb_ref[rows, :] = z[:, B_QK:].astype(BF16)

    def gla_v(h, rows):
        vb_ref[rows, :] = cols(h, o_b + 2 * B_QK, o_b + 2 * B_QK + B_V).astype(BF16)

    def gla_r(h, rows):
        rb_ref[rows, :] = cols(h, o_b + 2 * B_QK + B_V, o_b + 2 * B_QK + 2 * B_V).astype(BF16)

    def gla_gate(h, rows):
        zg = jnp.dot(h, wz_ref[...], preferred_element_type=F32).astype(BF16)
        pre = jnp.dot(zg, w2_ref[...], preferred_element_type=F32) + b2_ref[...]
        la_ref[rows, :] = (jnp.minimum(pre, 0.0) - jnp.log1p(jnp.exp(-jnp.abs(pre)))) / B_GATE_NORM

    return (gla_gate, attn_q, attn_kv, gla_qk, gla_v, gla_r)


ATTN_QBLOCKS = 8


def _attn_kernel(sink_ref, q_ref, kvp_ref, kvc_ref, kvn_ref, ckv_ref, o_ref, *, nsteps):
    n = pl.program_id(1)
    blk = WINDOW
    row = lax.broadcasted_iota(jnp.int32, (blk, blk), 0)
    col = lax.broadcasted_iota(jnp.int32, (blk, blk), 1)
    ninf = jnp.float32(-jnp.inf)
    lane = lax.broadcasted_iota(jnp.int32, (blk, LANES), 1)
    lo = lane < LANES // 2
    rep = A_HEADS // A_KV_HEADS

    def key_block(i, lanes):
        if i == 0:
            return kvp_ref[:, lanes]
        if i == ATTN_QBLOCKS + 1:
            return kvn_ref[:, lanes]
        return kvc_ref[(i - 1) * blk:i * blk, lanes]

    def band(qb, lanes):
        return jnp.concatenate([key_block(qb + j, lanes) for j in range(3)] + [ckv_ref[:, lanes]], axis=0)

    kd = [[band(qb, slice(g * LANES, (g + 1) * LANES)) for g in range(A_KV_HEADS)] for qb in range(ATTN_QBLOCKS)]
    vd = [[band(qb, slice((A_KV_HEADS + g) * LANES, (A_KV_HEADS + g + 1) * LANES)) for g in range(A_KV_HEADS)]
          for qb in range(ATTN_QBLOCKS)]
    nkb = kd[0][0].shape[0] // blk
    bias_prev = [jnp.where((col >= row) & ((n > 0) if qb == 0 else True), 0.0, ninf) for qb in range(ATTN_QBLOCKS)]
    bias_next = [jnp.where((col <= row) & ((n < nsteps - 1) if qb == ATTN_QBLOCKS - 1 else True), 0.0, ninf)
                 for qb in range(ATTN_QBLOCKS)]

    def scores(qb, p):
        pair = q_ref[qb * blk:(qb + 1) * blk, p * LANES:(p + 1) * LANES]
        zero = jnp.zeros_like(pair)
        lhs = jnp.concatenate([jnp.where(lo, pair, zero), jnp.where(lo, zero, pair)], axis=0)
        return lax.dot_general(lhs, kd[qb][(2 * p) // rep], (((1,), (1,)), ((), ())), preferred_element_type=F32)

    def softmax(qb, p, s):
        ps, sinks = [], []
        for r in range(2):
            sk = sink_ref[2 * p + r] * LOG2E
            cb = [s[r * blk:(r + 1) * blk, j * blk:(j + 1) * blk] for j in range(nkb)]
            cb[0] = cb[0] + bias_prev[qb]
            cb[2] = cb[2] + bias_next[qb]
            mx = cb[0]
            for c in cb[1:]:
                mx = jnp.maximum(mx, c)
            m = jnp.maximum(jnp.max(mx, axis=-1, keepdims=True), sk)
            ps.append(jnp.concatenate([jnp.exp2((c - m).astype(BF16)) for c in cb], axis=1))
            sinks.append(jnp.exp2(sk - m))
        return jnp.concatenate(ps, axis=0), sinks

    def weighted_values(qb, p, probs, sinks):
        o = jnp.dot(probs, vd[qb][(2 * p) // rep], preferred_element_type=F32)
        o_even, o_odd = o[:blk, :], o[blk:, :]
        num = jnp.where(lo, o_even, pltpu.roll(o_odd, LANES // 2, 1))
        den = jnp.where(lo, pltpu.roll(o_even, LANES // 2, 1) + sinks[0], o_odd + sinks[1])
        o_ref[qb * blk:(qb + 1) * blk, p * LANES:(p + 1) * LANES] = (num / den).astype(BF16)

    units = [(qb, p) for qb in range(ATTN_QBLOCKS) for p in range(A_HEADS // 2)]
    s_next = scores(*units[0])
    for i, unit in enumerate(units):
        s_cur = s_next
        if i + 1 < len(units):
            s_next = scores(*units[i + 1])
        probs, sinks = softmax(*unit, s_cur)
        weighted_values(*unit, probs, sinks)


def _attention(sink, qa, kv, b, t, lctx):
    nb = t // WINDOW
    tq = ATTN_QBLOCKS * WINDOW
    nsteps = t // tq
    ctx0 = (b * t) // lctx

    def edge(off):
        return pl.BlockSpec((WINDOW, KV_DUP),
                            lambda bi, n: (bi * nb + jnp.clip(n * ATTN_QBLOCKS + off, 0, nb - 1), 0))

    return pl.pallas_call(
        functools.partial(_attn_kernel, nsteps=nsteps),
        grid=(b, nsteps),
        in_specs=[
            pl.BlockSpec(memory_space=pltpu.SMEM),
            pl.BlockSpec((tq, A_Q), lambda bi, n: (bi * nsteps + n, 0)),
            edge(-1),
            pl.BlockSpec((tq, KV_DUP), lambda bi, n: (bi * nsteps + n, 0)),
            edge(ATTN_QBLOCKS),
            pl.BlockSpec((lctx, KV_DUP), lambda bi, n: (ctx0 + bi, 0)),
        ],
        out_specs=pl.BlockSpec((tq, A_Q), lambda bi, n: (bi * nsteps + n, 0)),
        out_shape=jax.ShapeDtypeStruct((b * t, A_Q), BF16),
        compiler_params=_cparams("parallel", "parallel"),
        name="window_attn",
    )(sink, qa, kv, kv, kv, kv)


B_PAIRS = B_HEADS // 2
PAIR_QK = 2 * B_DK
PAIR_V = 2 * B_DV


class _GlaMasks:
    def __init__(self, reverse):
        c = B_CHUNK
        i = lax.broadcasted_iota(jnp.int32, (c, c), 0)
        j = lax.broadcasted_iota(jnp.int32, (c, c), 1)
        self.tri = jnp.where((j >= i) if reverse else (j <= i), 1.0, 0.0).astype(BF16)
        lane_qk = lax.broadcasted_iota(jnp.int32, (c, B_QK), 1)
        self.k_head = [(lane_qk >= h * B_DK) & (lane_qk < (h + 1) * B_DK) for h in range(B_HEADS)]
        lane_v = lax.broadcasted_iota(jnp.int32, (c, B_V), 1)
        self.v_head = [(lane_v >= h * B_DV) & (lane_v < (h + 1) * B_DV) for h in range(B_HEADS)]
        ai = lax.broadcasted_iota(jnp.int32, (c, B_HEADS * c), 0)
        aj = lax.broadcasted_iota(jnp.int32, (c, B_HEADS * c), 1) % c
        self.causal = (aj >= ai) if reverse else (aj <= ai)
        sr = lax.broadcasted_iota(jnp.int32, (PAIR_V, PAIR_QK), 0) // B_DV
        sl = lax.broadcasted_iota(jnp.int32, (PAIR_V, PAIR_QK), 1) // B_DK
        self.state_diag = sr == sl


def _gla_block(q_ref, k_ref, v_ref, la_ref, st, mk, reverse, want_out):
    c = B_CHUNK
    nc = k_ref.shape[0] // c
    order = list(reversed(range(nc))) if reverse else list(range(nc))
    rows = {ci: slice(ci * c, (ci + 1) * c) for ci in order}
    tb = (((1,), (1,)), ((), ()))
    ta = (((0,), (0,)), ((), ()))

    g = {}
    for ci in order:
        la = la_ref[rows[ci], :]
        hi = la.astype(BF16)
        lo = (la - hi.astype(F32)).astype(BF16)
        g[ci] = jnp.dot(mk.tri, hi, preferred_element_type=F32) + jnp.dot(mk.tri, lo, preferred_element_type=F32)

    o = {}
    for g0 in range(0, nc, GLA_GROUP):
        group = order[g0:g0 + GLA_GROUP]
        dec, k_end, q_t, kstack = {}, {}, {}, {}
        for ci in group:
            k = k_ref[rows[ci], :].astype(F32)
            g_last = g[ci][0:1, :] if reverse else g[ci][c - 1:c, :]
            dec[ci] = jnp.exp(g_last)
            k_end[ci] = (k * jnp.exp(g_last - g[ci])).astype(BF16)
            if want_out:
                q_t[ci] = ((q_ref[rows[ci], :].astype(F32) * (B_DK ** -0.5)) * jnp.exp(g[ci])).astype(BF16)
                k_t = k * jnp.exp(-g[ci])
                kstack[ci] = jnp.concatenate([jnp.where(m, k_t, 0.0) for m in mk.k_head], axis=0).astype(BF16)

        ds = {}
        for ci in group:
            v = v_ref[rows[ci], :]
            ds[ci] = [jnp.where(mk.state_diag,
                                lax.dot_general(v[:, p * PAIR_V:(p + 1) * PAIR_V],
                                                k_end[ci][:, p * PAIR_QK:(p + 1) * PAIR_QK], ta,
                                                preferred_element_type=F32), 0.0) for p in range(B_PAIRS)]

        st_prev = {}
        for ci in group:
            st_prev[ci] = st
            st = [st[p] * dec[ci][:, p * PAIR_QK:(p + 1) * PAIR_QK] + ds[ci][p] for p in range(B_PAIRS)]
        if not want_out:
            continue

        att = {}
        for ci in group:
            a = lax.dot_general(q_t[ci], kstack[ci], tb, preferred_element_type=F32)
            att[ci] = jnp.where(mk.causal, a, 0.0).astype(BF16)
        for ci in group:
            v = v_ref[rows[ci], :]
            vstack = jnp.concatenate([jnp.where(m, v, jnp.zeros_like(v)) for m in mk.v_head], axis=0)
            o[ci] = jnp.dot(att[ci], vstack, preferred_element_type=F32) + jnp.concatenate(
                [lax.dot_general(q_t[ci][:, p * PAIR_QK:(p + 1) * PAIR_QK], st_prev[ci][p].astype(BF16), tb,
                                 preferred_element_type=F32) for p in range(B_PAIRS)], axis=1)
    if not want_out:
        return None, st
    return o, st


def _gla_ctx_kernel(k_ref, v_ref, la_ref, st_ref, *, reverse):
    mk = _GlaMasks(reverse)
    st = [jnp.zeros((PAIR_V, PAIR_QK), F32) for _ in range(B_PAIRS)]
    _, st = _gla_block(None, k_ref, v_ref, la_ref, st, mk, reverse, False)
    for p in range(B_PAIRS):
        st_ref[p] = st[p]


def _gla_ctx_state(kb, vb, la, b, t, l, reverse):
    d = 1 if reverse else 0
    ctx0 = (b * t) // l
    return pl.pallas_call(
        functools.partial(_gla_ctx_kernel, reverse=reverse),
        grid=(b,),
        in_specs=[
            pl.BlockSpec((l, B_QK), lambda bi: (ctx0 + bi, 0)),
            pl.BlockSpec((l, B_V), lambda bi: (ctx0 + bi, 0)),
            pl.BlockSpec((l, B_QK), lambda bi: (ctx0 + bi, d)),
        ],
        out_specs=pl.BlockSpec((None, B_PAIRS, PAIR_V, PAIR_QK), lambda bi: (bi, 0, 0, 0)),
        out_shape=jax.ShapeDtypeStruct((b, B_PAIRS, PAIR_V, PAIR_QK), F32),
        compiler_params=_cparams("parallel"),
        name="gla_ctx_state",
    )(kb, vb, la)


def _gla_kernel(q_ref, k_ref, v_ref, la_ref, st0_ref, *rest, reverse):
    if reverse:
        of_ref, r_ref, gg_ref, o_ref, st_ref = rest
    else:
        o_ref, st_ref = rest

    @pl.when(pl.program_id(1) == 0)
    def _():
        st_ref[...] = st0_ref[...]

    mk = _GlaMasks(reverse)
    outs_by_chunk, st = _gla_block(q_ref, k_ref, v_ref, la_ref, [st_ref[p] for p in range(B_PAIRS)],
                                   mk, reverse, True)
    for ci, o in outs_by_chunk.items():
        rows = slice(ci * B_CHUNK, (ci + 1) * B_CHUNK)
        if reverse:
            o = o + of_ref[rows, :]
            outs = []
            for h in range(B_HEADS):
                oh = o[:, h * B_DV:(h + 1) * B_DV]
                oh = (oh * lax.rsqrt(jnp.mean(oh * oh, axis=-1, keepdims=True) + RMS_EPS)) * gg_ref[...]
                outs.append(oh * _silu(r_ref[rows, h * B_DV:(h + 1) * B_DV].astype(F32)))
            o_ref[rows, :] = jnp.concatenate(outs, axis=1).astype(BF16)
        else:
            o_ref[rows, :] = o
    for p in range(B_PAIRS):
        st_ref[p] = st[p]


def _gla(qb, kb, vb, la, st0, b, t, reverse, o_fwd=None, rb=None, gla_g=None):
    tb = TB_GLA
    nt = t // tb
    d = 1 if reverse else 0

    def tok(w, lane_blk=0):
        if reverse:
            return pl.BlockSpec((tb, w), lambda bi, j: (bi * nt + nt - 1 - j, lane_blk))
        return pl.BlockSpec((tb, w), lambda bi, j: (bi * nt + j, lane_blk))

    in_specs = [tok(B_QK), tok(B_QK), tok(B_V), tok(B_QK, d),
                pl.BlockSpec((None, B_PAIRS, PAIR_V, PAIR_QK), lambda bi, j: (bi, 0, 0, 0))]
    args = [qb, kb, vb, la, st0]
    if reverse:
        in_specs += [tok(B_V), tok(B_V), _const_spec((1, B_DV))]
        args += [o_fwd, rb, gla_g.reshape(1, B_DV)]
    return pl.pallas_call(
        functools.partial(_gla_kernel, reverse=reverse),
        grid=(b, nt),
        in_specs=in_specs,
        out_specs=tok(B_V),
        out_shape=jax.ShapeDtypeStruct((b * t, B_V), BF16 if reverse else F32),
        scratch_shapes=[pltpu.VMEM((B_PAIRS, PAIR_V, PAIR_QK), F32)],
        compiler_params=_cparams("parallel", "arbitrary"),
        name="gla_bwd" if reverse else "gla_fwd",
    )(*args)


def _shift_rows(e, d):
    n = e.shape[0]
    return pltpu.roll(e, (-d) % n, 0)


def _pool_kernel(x_ref, xp_ref, xn_ref, mod_ref, g_ref, w_ref, ps_ref, o_ref, *, seq):
    i = pl.program_id(1)
    tm = x_ref.shape[0]
    g = g_ref[...]
    row = pl.program_id(0)
    shift = _mod(mod_ref, 3, row)
    scale = _mod(mod_ref, 4, row)
    x = x_ref[...]
    h = _modulate(x, g, shift, scale)
    hp = jnp.where(i > 0, _modulate(xp_ref[...], g, shift, scale), 0.0)
    hn = jnp.where(i < pl.num_programs(1) - 1, _modulate(xn_ref[...], g, shift, scale), 0.0)
    groups, totals = [], []
    for gi, w in enumerate(POOL_WINDOWS):
        ls = slice(gi * POOL_GROUP, (gi + 1) * POOL_GROUP)
        groups.append(h[:, ls])
        s = jnp.concatenate([hp[:, ls], h[:, ls], hn[:, ls]], axis=0)
        k = 1
        while k < w:
            s = s + _shift_rows(s, -k)
            k *= 2
        if w // 2 > 1:
            s = _shift_rows(s, w // 2 - 1)
        totals.append(s[HALO:HALO + tm, :])

    def finish(scales):
        ys = [jnp.dot((totals[gi] * scales[gi] - groups[gi]).astype(BF16), w_ref[gi], preferred_element_type=F32)
              for gi in range(len(POOL_WINDOWS))]
        y = jnp.concatenate(ys, axis=1) * ps_ref[...]
        o_ref[...] = x + _mod(mod_ref, 5, row) * y

    clipped = (i == 0) | (i == pl.num_programs(1) - 1)

    @pl.when(clipped)
    def _():
        t = i * tm + lax.broadcasted_iota(jnp.int32, (tm, 1), 0)
        finish([1.0 / (jnp.minimum(t + w // 2, seq) - jnp.maximum(t - w // 2, 0)).astype(F32)
                for w in POOL_WINDOWS])

    @pl.when(jnp.logical_not(clipped))
    def _():
        finish([1.0 / w for w in POOL_WINDOWS])


def _pool(x, mods, layer, g, w_pool, pool_scale):
    b, t, d = x.shape
    tm = TM_POOL
    hb = tm // HALO
    last = t // HALO - 1
    return pl.pallas_call(
        functools.partial(_pool_kernel, seq=t),
        grid=(b, t // tm),
        in_specs=[
            pl.BlockSpec((None, tm, d), lambda bi, i: (bi, i, 0)),
            pl.BlockSpec((None, HALO, d), lambda bi, i: (bi, jnp.maximum(i * hb - 1, 0), 0)),
            pl.BlockSpec((None, HALO, d), lambda bi, i: (bi, jnp.minimum((i + 1) * hb, last), 0)),
            _mod_spec(layer, d),
            _const_spec((1, d)),
            _const_spec(w_pool.shape),
            _const_spec((1, d)),
        ],
        out_specs=pl.BlockSpec((None, tm, d), lambda bi, i: (bi, i, 0)),
        out_shape=jax.ShapeDtypeStruct((b, t, d), F32),
        compiler_params=_cparams("parallel", "parallel"),
        name="pool_mixer",
    )(x, x, x, mods, g.reshape(1, d), w_pool, pool_scale.reshape(1, d))


def _rope_tables(t, pad_rows):
    n = A_HEAD_DIM // 4
    freqs = ROPE_BASE ** (-jnp.arange(n, dtype=F32) / n)
    pos = jnp.arange(t)
    rows = (pos // GRID_W).astype(F32)
    cols = (pos % GRID_W).astype(F32)
    ang_r = rows[:, None] * freqs
    ang_c = cols[:, None] * freqs
    cos = jnp.concatenate([jnp.cos(ang_r), jnp.cos(ang_r), jnp.cos(ang_c), jnp.cos(ang_c)], axis=1)
    sin = jnp.concatenate([-jnp.sin(ang_r), jnp.sin(ang_r), -jnp.sin(ang_c), jnp.sin(ang_c)], axis=1)
    cos = jnp.concatenate([jnp.tile(cos, (1, 2)), jnp.ones((pad_rows, LANES), F32)], axis=0)
    sin = jnp.concatenate([jnp.tile(sin, (1, 2)), jnp.zeros((pad_rows, LANES), F32)], axis=0)
    return cos, sin


def kernel(x, c, ctx, c_ctx, w_mod, b_mod, norm_g, ffn1_wi, ffn1_wo, ffn2_wi, ffn2_wo,
           w_in, w_a2_f, b_a_f, w_a2_b, b_a_b, sink, gla_g, w_out, w_pool, pool_scale, final_g):
    b, t, d = x.shape
    lctx = ctx.shape[1]
    n_tok = b * t
    ctx_row = b

    cond = jnp.zeros((COND_ROWS, d), F32).at[:b].set(c).at[ctx_row].set(c_ctx)
    mods = _adaln(cond, w_mod, b_mod)
    x = x.reshape(n_tok, d)

    w_main = w_in[0][:, :PROJ_MAIN].astype(BF16)
    w_zg = jnp.zeros((d, LANES), F32).at[:, :2 * B_GATE_RANK].set(w_in[0][:, PROJ_MAIN:]).astype(BF16)
    w2 = jnp.zeros((LANES, 2 * B_QK), F32)
    w2 = w2.at[:B_GATE_RANK, :B_QK].set(w_a2_f[0]).at[B_GATE_RANK:2 * B_GATE_RANK, B_QK:].set(w_a2_b[0]).astype(BF16)
    b2 = jnp.concatenate([b_a_f[0], b_a_b[0]]).reshape(1, 2 * B_QK)
    cos, sin = _rope_tables(t, TM_FFN_PROJ)
    xc, qa, kv, qb, kb, vb, rb, la = _ffn(
        x, n_tok, t, mods, norm_g[0, 0], ffn1_wi, ffn1_wo, 0, 0, tm=TM_FFN_PROJ,
        ctx=ctx.reshape(b * lctx, d), ctx_row=ctx_row, proj=(norm_g[0, 1], w_main, w_zg, w2, b2, cos, sin))

    o_a = _attention(sink[0], qa, kv, b, t, lctx)
    st_f = _gla_ctx_state(kb, vb, la, b, t, lctx, False)
    st_b = _gla_ctx_state(kb, vb, la, b, t, lctx, True)
    o_f = _gla(qb, kb, vb, la, st_f, b, t, False)
    o_g = _gla(qb, kb, vb, la, st_b, b, t, True, o_f, rb, gla_g[0])
    x = _ffn(xc, n_tok, t, mods, norm_g[0, 2], ffn2_wi, ffn2_wo, 0, 6, mix=(o_a, o_g, w_out[0].astype(BF16)))

    x = _ffn(x, n_tok, t, mods, norm_g[1, 0], ffn1_wi, ffn1_wo, 1, 0)
    x = _pool(x.reshape(b, t, d), mods, 1, norm_g[1, 1], w_pool[0].astype(BF16), pool_scale[0])
    x = _ffn(x.reshape(n_tok, d), n_tok, t, mods, norm_g[1, 2], ffn2_wi, ffn2_wo, 1, 6, final_g=final_g)
    return x.reshape(b, t, d)
```

```python
import functools

import jax
import jax.numpy as jnp
import numpy as np
from jax import lax
from jax.experimental import pallas as pl
from jax.experimental.pallas import tpu as pltpu

F32 = jnp.float32
BF16 = jnp.bfloat16

D_MODEL = 1024
N_MOD = 9
D_FF = 2816
RMS_EPS = 1e-6
GRID_W = 64
A_HEADS = 8
A_KV_HEADS = 2
A_HEAD_DIM = 64
WINDOW = 128
ROPE_BASE = 10000.0
B_HEADS = 4
B_DK = 64
B_DV = 128
B_GATE_RANK = 16
B_GATE_NORM = 16.0
B_CHUNK = 64
POOL_WINDOWS = (2, 4, 8, 16)
POOL_GROUP = D_MODEL // len(POOL_WINDOWS)
A_Q = A_HEADS * A_HEAD_DIM
A_KV = A_KV_HEADS * A_HEAD_DIM
B_QK = B_HEADS * B_DK
B_V = B_HEADS * B_DV
PROJ_MAIN = A_Q + 2 * A_KV + 2 * B_QK + 2 * B_V
KV_DUP = 4 * A_KV

LOG2E = 1.4426950408889634
LANES = 128
COND_ROWS = 8
VMEM_LIMIT = 56 * 1024 * 1024

TM_FFN = 1024
SUB_FFN = 256
TM_FFN_PROJ = 512
TM_POOL = 512
TB_GLA = 1024
GLA_GROUP = 4
HALO = 8


def _cparams(*sem):
    return pltpu.CompilerParams(dimension_semantics=sem, vmem_limit_bytes=VMEM_LIMIT)


def _const_spec(shape):
    nd = len(shape)
    return pl.BlockSpec(shape, lambda *_: (0,) * nd, pipeline_mode=pl.Buffered(1))


def _modulate(x, g, shift, scale):
    y = x * lax.rsqrt(jnp.mean(x * x, axis=-1, keepdims=True) + RMS_EPS)
    return (y * g) * (1.0 + scale) + shift


def _silu(a):
    return a * jax.nn.sigmoid(a)


ADALN_MODS_PER_STEP = 3


def _adaln_kernel(c_ref, w_ref, b_ref, o_ref):
    d = c_ref.shape[1]
    s = _silu(c_ref[...]).astype(BF16)
    r = jnp.dot(s, w_ref[...].astype(BF16), preferred_element_type=F32) + b_ref[...]
    for k in range(ADALN_MODS_PER_STEP):
        o_ref[k] = r[:, k * d:(k + 1) * d]


def _adaln(cond, w_mod, b_mod):
    depth, d, n = w_mod.shape
    per = ADALN_MODS_PER_STEP
    tn = per * d
    return pl.pallas_call(
        _adaln_kernel,
        grid=(depth, N_MOD // per),
        in_specs=[
            pl.BlockSpec((COND_ROWS, d), lambda l, j: (0, 0)),
            pl.BlockSpec((None, d, tn), lambda l, j: (l, 0, j)),
            pl.BlockSpec((None, 1, tn), lambda l, j: (l, 0, j)),
        ],
        out_specs=pl.BlockSpec((None, per, COND_ROWS, d), lambda l, j: (l, j, 0, 0)),
        out_shape=jax.ShapeDtypeStruct((depth, N_MOD, COND_ROWS, d), F32),
        compiler_params=_cparams("parallel", "parallel"),
        name="adaln",
    )(cond, w_mod, b_mod.reshape(depth, 1, n))


def _mod_spec(layer, d):
    return pl.BlockSpec((None, N_MOD, COND_ROWS, d), lambda *_: (layer, 0, 0, 0))


def _mod(mod_ref, idx, row):
    return mod_ref[idx, pl.ds(row, 1), :]


FFN_WCHUNK = 256
FFN_NCH = D_FF // FFN_WCHUNK
FFN_CHUNKS_PER_STEP = 2
FFN_NW = 2 * FFN_NCH // FFN_CHUNKS_PER_STEP
FFN_WO_ROWS = D_FF // FFN_NW


def _ffn_tile(x_ref, o_ref, mod_ref, row, g_ref, wi_s, wo_s, mi, mix, fg_ref, proj):
    sub = min(SUB_FFN, x_ref.shape[0])
    nsub = x_ref.shape[0] // sub

    def prologue(s):
        rows = slice(s * sub, (s + 1) * sub)
        x = x_ref[rows, :]
        if mix is not None:
            oa_ref, ob_ref, w_ref = mix
            y0 = (jnp.dot(oa_ref[rows, :], w_ref[:A_Q, :], preferred_element_type=F32)
                  + jnp.dot(ob_ref[rows, :], w_ref[A_Q:, :], preferred_element_type=F32))
            x = x + _mod(mod_ref, 5, row) * y0
        return x, _modulate(x, g_ref[...], _mod(mod_ref, mi, row), _mod(mod_ref, mi + 1, row)).astype(BF16)

    def act_chunk(h, c):
        a = jnp.dot(h, wi_s[c], preferred_element_type=F32)
        u = jnp.dot(h, wi_s[FFN_NCH + c], preferred_element_type=F32)
        return (_silu(a) * u).astype(BF16)

    nxt = prologue(0)
    early = []
    pending = []
    for s in range(nsub):
        rows = slice(s * sub, (s + 1) * sub)
        x, h = nxt
        acts, early = early, []
        for c in range(len(acts), FFN_NCH):
            if c == FFN_NCH - 1 and s + 1 < nsub:
                nxt = prologue(s + 1)
                early = [act_chunk(nxt[1], 0)]
            acts.append(act_chunk(h, c))
        y = jnp.dot(jnp.concatenate(acts, axis=1), wo_s[...], preferred_element_type=F32)
        o = x + (0.5 * _mod(mod_ref, mi + 2, row)) * y
        if fg_ref is not None:
            o = (o * lax.rsqrt(jnp.mean(o * o, axis=-1, keepdims=True) + RMS_EPS)) * fg_ref[...]
        o_ref[rows, :] = o
        if proj is not None:
            pending.append(
                (_modulate(o, proj[0][...], _mod(mod_ref, 3, row), _mod(mod_ref, 4, row)).astype(BF16), rows))
    for h2, rows in pending:
        for stage in proj[1]:
            stage(h2, rows)


def _ffn_kernel(*refs, mi, nx, tiles_per_batch, ctx_row, has_ctx, has_mix, final, has_proj):
    refs = list(refs)
    x_ref = refs.pop(0)
    c_ref = refs.pop(0) if has_ctx else None
    mod_ref, g_ref, wi_ref, wo_ref = refs[:4]
    refs = refs[4:]
    mix = tuple(refs[:3]) if has_mix else None
    refs = refs[3:] if has_mix else refs
    fg_ref = refs.pop(0) if final else None
    proj = None
    if has_proj:
        g1_ref, wp_ref, wz_ref, w2_ref, b2_ref, cos_ref, sin_ref = refs[:7]
        proj = (g1_ref, _proj_stage_fns(wp_ref, wz_ref, w2_ref, b2_ref, cos_ref, sin_ref, refs[8:8 + N_PROJ_OUT]))
        refs = refs[7:8] + refs[8 + N_PROJ_OUT:]
    o_ref, wi_s, wo_s = refs
    j = pl.program_id(0)

    @pl.when(j < FFN_NW)
    def _():
        for k in range(FFN_CHUNKS_PER_STEP):
            wi_s[j * FFN_CHUNKS_PER_STEP + k] = wi_ref[:, k * FFN_WCHUNK:(k + 1) * FFN_WCHUNK].astype(BF16)
        wo_s[pl.ds(pl.multiple_of(j * FFN_WO_ROWS, FFN_WO_ROWS), FFN_WO_ROWS), :] = wo_ref[...].astype(BF16)

    @pl.when((j >= FFN_NW) & (j < FFN_NW + nx))
    def _():
        _ffn_tile(x_ref, o_ref, mod_ref, (j - FFN_NW) // tiles_per_batch, g_ref, wi_s, wo_s, mi, mix, fg_ref, proj)

    if has_ctx:
        @pl.when(j >= FFN_NW + nx)
        def _():
            _ffn_tile(c_ref, o_ref, mod_ref, ctx_row, g_ref, wi_s, wo_s, mi, None, fg_ref, proj)


def _ffn(x, n_rows, tok_per_batch, mods, g, wi, wo, layer, mi, *, tm=TM_FFN, ctx=None, ctx_row=None, mix=None,
         final_g=None, proj=None):
    d = x.shape[1]
    nx = n_rows // tm
    nc = 0 if ctx is None else ctx.shape[0] // tm
    tiles_per_batch = tok_per_batch // tm

    def tile(j):
        return jnp.clip(j - FFN_NW, 0, nx + nc - 1)

    def x_tile(j):
        return jnp.minimum(tile(j), nx - 1)

    def wchunk(j):
        return jnp.minimum(j, FFN_NW - 1)

    def pos_blk(j):
        return jnp.where(tile(j) < nx, tile(j) % tiles_per_batch, tiles_per_batch)

    in_specs = [pl.BlockSpec((tm, d), lambda j: (x_tile(j), 0))]
    args = [x]
    if ctx is not None:
        in_specs.append(pl.BlockSpec((tm, d), lambda j: (jnp.clip(j - FFN_NW - nx, 0, nc - 1), 0)))
        args.append(ctx)
    in_specs += [
        _mod_spec(layer, d),
        _const_spec((1, d)),
        pl.BlockSpec((None, d, FFN_WCHUNK * FFN_CHUNKS_PER_STEP), lambda j: (layer, 0, wchunk(j))),
        pl.BlockSpec((None, FFN_WO_ROWS, d), lambda j: (layer, wchunk(j), 0)),
    ]
    args += [mods, g.reshape(1, d), wi, wo]
    if mix is not None:
        oa, ob, w_out = mix
        in_specs += [pl.BlockSpec((tm, A_Q), lambda j: (x_tile(j), 0)),
                     pl.BlockSpec((tm, B_V), lambda j: (x_tile(j), 0)),
                     _const_spec(w_out.shape)]
        args += [oa, ob, w_out]
    if final_g is not None:
        in_specs.append(_const_spec((1, d)))
        args.append(final_g.reshape(1, d))
    n_out_rows = (nx + nc) * tm
    out_specs = [pl.BlockSpec((tm, d), lambda j: (tile(j), 0))]
    out_shape = [jax.ShapeDtypeStruct((n_out_rows, d), F32)]
    if proj is not None:
        g1, w_main, w_zg, w2, b2, cos, sin = proj
        in_specs += [_const_spec((1, d)), _const_spec(w_main.shape), _const_spec(w_zg.shape), _const_spec(w2.shape),
                     _const_spec(b2.shape),
                     pl.BlockSpec((tm, LANES), lambda j: (pos_blk(j), 0)),
                     pl.BlockSpec((tm, LANES), lambda j: (pos_blk(j), 0))]
        args += [g1.reshape(1, d), w_main, w_zg, w2, b2, cos, sin]
        out_specs += [pl.BlockSpec((tm, w), lambda j: (tile(j), 0)) for w, _ in PROJ_OUTS]
        out_shape += [jax.ShapeDtypeStruct((n_out_rows, w), dt) for w, dt in PROJ_OUTS]
    res = pl.pallas_call(
        functools.partial(_ffn_kernel, mi=mi, nx=nx, tiles_per_batch=tiles_per_batch, ctx_row=ctx_row,
                          has_ctx=ctx is not None, has_mix=mix is not None,
                          final=final_g is not None, has_proj=proj is not None),
        grid=(FFN_NW + nx + nc,),
        in_specs=in_specs,
        out_specs=out_specs,
        out_shape=out_shape,
        scratch_shapes=[pltpu.VMEM((2 * FFN_NCH, d, FFN_WCHUNK), BF16), pltpu.VMEM((D_FF, d), BF16)],
        compiler_params=_cparams("arbitrary"),
        name="ffn_proj" if proj is not None else "ffn",
    )(*args)
    return res if proj is not None else res[0]


def _swap16(x):
    lane = lax.broadcasted_iota(jnp.int32, x.shape, 1)
    return jnp.where((lane & 16) == 0, pltpu.roll(x, LANES - 16, 1), pltpu.roll(x, 16, 1))


def _dup_halves(x):
    lane = lax.broadcasted_iota(jnp.int32, x.shape, 1)
    r = pltpu.roll(x, LANES // 2, 1)
    lo = lane < LANES // 2
    return jnp.where(lo, x, r), jnp.where(lo, r, x)


def _halves_with_ones(x):
    lane = lax.broadcasted_iota(jnp.int32, x.shape, 1)
    r = pltpu.roll(x, LANES // 2, 1)
    lo = lane < LANES // 2
    return jnp.where(lo, x, 1.0), jnp.where(lo, r, 1.0)


Z_QA = (A_Q, 0)
Z_KV = (KV_DUP, 1)
Z_VB = (B_V, 2)
Z_RB = (B_V, 3)
Z_QB = (B_QK, 8)
Z_KB = (B_QK, 9)
Z_WIDTH = (Z_KB[1] + 1) * B_QK
PROJ_OUTS = ((Z_WIDTH, BF16), (2 * B_QK, F32))
N_PROJ_OUT = len(PROJ_OUTS)


def _z_cols(field):
    w, blk = field
    return slice(w * blk, w * (blk + 1))


def _proj_stage_fns(w_ref, wz_ref, w2_ref, b2_ref, cos_ref, sin_ref, outs):
    z_ref, la_ref = outs
    qa0, kv0 = _z_cols(Z_QA).start, _z_cols(Z_KV).start
    scale = A_HEAD_DIM ** -0.5 * LOG2E
    o_b = A_Q + 2 * A_KV

    def cols(h, lo, hi):
        return jnp.dot(h, w_ref[:, lo:hi], preferred_element_type=F32)

    def rope(v, rows):
        return v * cos_ref[rows, :] + _swap16(v) * sin_ref[rows, :]

    def attn_q(h, rows):
        z = cols(h, 0, A_Q)
        for p in range(A_Q // LANES):
            q = z[:, p * LANES:(p + 1) * LANES]
            z_ref[rows, qa0 + p * LANES:qa0 + (p + 1) * LANES] = (rope(q, rows) * scale).astype(BF16)

    def attn_kv(h, rows):
        z = cols(h, A_Q, A_Q + 2 * A_KV)
        k0, k1 = _dup_halves(rope(z[:, :A_KV], rows))
        v0, v1 = _halves_with_ones(z[:, A_KV:])
        for j, piece in enumerate((k0, k1, v0, v1)):
            z_ref[rows, kv0 + j * LANES:kv0 + (j + 1) * LANES] = piece.astype(BF16)

    def gla_qk(h, rows):
        z = cols(h, o_b, o_b + 2 * B_QK)
        z_ref[rows, _z_cols(Z_QB)] = z[:, :B_QK].astype(BF16)
        z_ref[rows, _z_cols(Z_KB)] = z[:, B_QK:].astype(BF16)

    def gla_v(h, rows):
        z_ref[rows, _z_cols(Z_VB)] = cols(h, o_b + 2 * B_QK, o_b + 2 * B_QK + B_V).astype(BF16)

    def gla_r(h, rows):
        z_ref[rows, _z_cols(Z_RB)] = cols(h, o_b + 2 * B_QK + B_V, o_b + 2 * B_QK + 2 * B_V).astype(BF16)

    def gla_gate(h, rows):
        zg = jnp.dot(h, wz_ref[...], preferred_element_type=F32).astype(BF16)
        pre = jnp.dot(zg, w2_ref[...], preferred_element_type=F32) + b2_ref[...]
        la_ref[rows, :] = (jnp.minimum(pre, 0.0) - jnp.log1p(jnp.exp(-jnp.abs(pre)))) / B_GATE_NORM

    return (gla_gate, attn_q, attn_kv, gla_qk, gla_v, gla_r)


ATTN_QBLOCKS = 8


def _attn_kernel(sink_ref, q_ref, kvp_ref, kvc_ref, kvn_ref, ckv_ref, o_ref, *, nsteps):
    n = pl.program_id(1)
    blk = WINDOW
    row = lax.broadcasted_iota(jnp.int32, (blk, blk), 0)
    col = lax.broadcasted_iota(jnp.int32, (blk, blk), 1)
    ninf = jnp.float32(-jnp.inf)
    lane = lax.broadcasted_iota(jnp.int32, (blk, LANES), 1)
    lo = lane < LANES // 2
    rep = A_HEADS // A_KV_HEADS

    def key_block(i, lanes):
        if i == 0:
            return kvp_ref[:, lanes]
        if i == ATTN_QBLOCKS + 1:
            return kvn_ref[:, lanes]
        return kvc_ref[(i - 1) * blk:i * blk, lanes]

    def band(qb, lanes):
        return jnp.concatenate([key_block(qb + j, lanes) for j in range(3)] + [ckv_ref[:, lanes]], axis=0)

    kd = [[band(qb, slice(g * LANES, (g + 1) * LANES)) for g in range(A_KV_HEADS)] for qb in range(ATTN_QBLOCKS)]
    vd = [[band(qb, slice((A_KV_HEADS + g) * LANES, (A_KV_HEADS + g + 1) * LANES)) for g in range(A_KV_HEADS)]
          for qb in range(ATTN_QBLOCKS)]
    nkb = kd[0][0].shape[0] // blk
    bias_prev = [jnp.where((col >= row) & ((n > 0) if qb == 0 else True), 0.0, ninf) for qb in range(ATTN_QBLOCKS)]
    bias_next = [jnp.where((col <= row) & ((n < nsteps - 1) if qb == ATTN_QBLOCKS - 1 else True), 0.0, ninf)
                 for qb in range(ATTN_QBLOCKS)]

    def scores(qb, p):
        pair = q_ref[qb * blk:(qb + 1) * blk, p * LANES:(p + 1) * LANES]
        zero = jnp.zeros_like(pair)
        lhs = jnp.concatenate([jnp.where(lo, pair, zero), jnp.where(lo, zero, pair)], axis=0)
        return lax.dot_general(lhs, kd[qb][(2 * p) // rep], (((1,), (1,)), ((), ())), preferred_element_type=F32)

    def softmax(qb, p, s):
        ps, sinks = [], []
        for r in range(2):
            sk = sink_ref[2 * p + r] * LOG2E
            cb = [s[r * blk:(r + 1) * blk, j * blk:(j + 1) * blk] for j in range(nkb)]
            cb[0] = cb[0] + bias_prev[qb]
            cb[2] = cb[2] + bias_next[qb]
            mx = cb[0]
            for c in cb[1:]:
                mx = jnp.maximum(mx, c)
            m = jnp.maximum(jnp.max(mx, axis=-1, keepdims=True), sk)
            ps.append(jnp.concatenate([jnp.exp2((c - m).astype(BF16)) for c in cb], axis=1))
            sinks.append(jnp.exp2(sk - m))
        return jnp.concatenate(ps, axis=0), sinks

    def weighted_values(qb, p, probs, sinks):
        o = jnp.dot(probs, vd[qb][(2 * p) // rep], preferred_element_type=F32)
        o_even, o_odd = o[:blk, :], o[blk:, :]
        num = jnp.where(lo, o_even, pltpu.roll(o_odd, LANES // 2, 1))
        den = jnp.where(lo, pltpu.roll(o_even, LANES // 2, 1) + sinks[0], o_odd + sinks[1])
        o_ref[qb * blk:(qb + 1) * blk, p * LANES:(p + 1) * LANES] = (num / den).astype(BF16)

    units = [(qb, p) for qb in range(ATTN_QBLOCKS) for p in range(A_HEADS // 2)]
    s_next = scores(*units[0])
    for i, unit in enumerate(units):
        s_cur = s_next
        if i + 1 < len(units):
            s_next = scores(*units[i + 1])
        probs, sinks = softmax(*unit, s_cur)
        weighted_values(*unit, probs, sinks)


def _attention(sink, z, b, t, lctx):
    nb = t // WINDOW
    tq = ATTN_QBLOCKS * WINDOW
    nsteps = t // tq
    ctx0 = (b * t) // lctx

    def edge(off):
        return pl.BlockSpec((WINDOW, KV_DUP),
                            lambda bi, n: (bi * nb + jnp.clip(n * ATTN_QBLOCKS + off, 0, nb - 1), Z_KV[1]))

    return pl.pallas_call(
        functools.partial(_attn_kernel, nsteps=nsteps),
        grid=(b, nsteps),
        in_specs=[
            pl.BlockSpec(memory_space=pltpu.SMEM),
            pl.BlockSpec((tq, A_Q), lambda bi, n: (bi * nsteps + n, Z_QA[1])),
            edge(-1),
            pl.BlockSpec((tq, KV_DUP), lambda bi, n: (bi * nsteps + n, Z_KV[1])),
            edge(ATTN_QBLOCKS),
            pl.BlockSpec((lctx, KV_DUP), lambda bi, n: (ctx0 + bi, Z_KV[1])),
        ],
        out_specs=pl.BlockSpec((tq, A_Q), lambda bi, n: (bi * nsteps + n, 0)),
        out_shape=jax.ShapeDtypeStruct((b * t, A_Q), BF16),
        compiler_params=_cparams("parallel", "parallel"),
        name="window_attn",
    )(sink, z, z, z, z, z)


B_PAIRS = B_HEADS // 2
PAIR_QK = 2 * B_DK
PAIR_V = 2 * B_DV


class _GlaMasks:
    def __init__(self, reverse):
        c = B_CHUNK
        i = lax.broadcasted_iota(jnp.int32, (c, c), 0)
        j = lax.broadcasted_iota(jnp.int32, (c, c), 1)
        self.tri = jnp.where((j >= i) if reverse else (j <= i), 1.0, 0.0).astype(BF16)
        lane_qk = lax.broadcasted_iota(jnp.int32, (c, B_QK), 1)
        self.k_head = [(lane_qk >= h * B_DK) & (lane_qk < (h + 1) * B_DK) for h in range(B_HEADS)]
        lane_v = lax.broadcasted_iota(jnp.int32, (c, B_V), 1)
        self.v_head = [(lane_v >= h * B_DV) & (lane_v < (h + 1) * B_DV) for h in range(B_HEADS)]
        ai = lax.broadcasted_iota(jnp.int32, (c, B_HEADS * c), 0)
        aj = lax.broadcasted_iota(jnp.int32, (c, B_HEADS * c), 1) % c
        self.causal = (aj >= ai) if reverse else (aj <= ai)
        sr = lax.broadcasted_iota(jnp.int32, (PAIR_V, PAIR_QK), 0) // B_DV
        sl = lax.broadcasted_iota(jnp.int32, (PAIR_V, PAIR_QK), 1) // B_DK
        self.state_diag = sr == sl


def _gla_block(q_ref, k_ref, v_ref, la_ref, st, mk, reverse, want_out):
    c = B_CHUNK
    nc = k_ref.shape[0] // c
    order = list(reversed(range(nc))) if reverse else list(range(nc))
    rows = {ci: slice(ci * c, (ci + 1) * c) for ci in order}
    tb = (((1,), (1,)), ((), ()))
    ta = (((0,), (0,)), ((), ()))

    g = {}
    for ci in order:
        la = la_ref[rows[ci], :]
        hi = la.astype(BF16)
        lo = (la - hi.astype(F32)).astype(BF16)
        g[ci] = jnp.dot(mk.tri, hi, preferred_element_type=F32) + jnp.dot(mk.tri, lo, preferred_element_type=F32)

    o = {}
    for g0 in range(0, nc, GLA_GROUP):
        group = order[g0:g0 + GLA_GROUP]
        dec, k_end, q_t, kstack = {}, {}, {}, {}
        for ci in group:
            k = k_ref[rows[ci], :].astype(F32)
            g_last = g[ci][0:1, :] if reverse else g[ci][c - 1:c, :]
            dec[ci] = jnp.exp(g_last)
            k_end[ci] = (k * jnp.exp(g_last - g[ci])).astype(BF16)
            if want_out:
                q_t[ci] = ((q_ref[rows[ci], :].astype(F32) * (B_DK ** -0.5)) * jnp.exp(g[ci])).astype(BF16)
                k_t = k * jnp.exp(-g[ci])
                kstack[ci] = jnp.concatenate([jnp.where(m, k_t, 0.0) for m in mk.k_head], axis=0).astype(BF16)

        ds = {}
        for ci in group:
            v = v_ref[rows[ci], :]
            ds[ci] = [jnp.where(mk.state_diag,
                                lax.dot_general(v[:, p * PAIR_V:(p + 1) * PAIR_V],
                                                k_end[ci][:, p * PAIR_QK:(p + 1) * PAIR_QK], ta,
                                                preferred_element_type=F32), 0.0) for p in range(B_PAIRS)]

        st_prev = {}
        for ci in group:
            st_prev[ci] = st
            st = [st[p] * dec[ci][:, p * PAIR_QK:(p + 1) * PAIR_QK] + ds[ci][p] for p in range(B_PAIRS)]
        if not want_out:
            continue

        att = {}
        for ci in group:
            a = lax.dot_general(q_t[ci], kstack[ci], tb, preferred_element_type=F32)
            att[ci] = jnp.where(mk.causal, a, 0.0).astype(BF16)
        for ci in group:
            v = v_ref[rows[ci], :]
            vstack = jnp.concatenate([jnp.where(m, v, jnp.zeros_like(v)) for m in mk.v_head], axis=0)
            o[ci] = jnp.dot(att[ci], vstack, preferred_element_type=F32) + jnp.concatenate(
                [lax.dot_general(q_t[ci][:, p * PAIR_QK:(p + 1) * PAIR_QK], st_prev[ci][p].astype(BF16), tb,
                                 preferred_element_type=F32) for p in range(B_PAIRS)], axis=1)
    if not want_out:
        return None, st
    return o, st


def _gla_ctx_kernel(k_ref, v_ref, la_ref, st_ref, *, reverse):
    mk = _GlaMasks(reverse)
    st = [jnp.zeros((PAIR_V, PAIR_QK), F32) for _ in range(B_PAIRS)]
    _, st = _gla_block(None, k_ref, v_ref, la_ref, st, mk, reverse, False)
    for p in range(B_PAIRS):
        st_ref[p] = st[p]


def _gla_ctx_state(z, la, b, t, l, reverse):
    d = 1 if reverse else 0
    ctx0 = (b * t) // l
    return pl.pallas_call(
        functools.partial(_gla_ctx_kernel, reverse=reverse),
        grid=(b,),
        in_specs=[
            pl.BlockSpec((l, B_QK), lambda bi: (ctx0 + bi, Z_KB[1])),
            pl.BlockSpec((l, B_V), lambda bi: (ctx0 + bi, Z_VB[1])),
            pl.BlockSpec((l, B_QK), lambda bi: (ctx0 + bi, d)),
        ],
        out_specs=pl.BlockSpec((None, B_PAIRS, PAIR_V, PAIR_QK), lambda bi: (bi, 0, 0, 0)),
        out_shape=jax.ShapeDtypeStruct((b, B_PAIRS, PAIR_V, PAIR_QK), F32),
        compiler_params=_cparams("parallel"),
        name="gla_ctx_state",
    )(z, z, la)


def _gla_kernel(q_ref, k_ref, v_ref, la_ref, st0_ref, *rest, reverse):
    if reverse:
        of_ref, r_ref, gg_ref, o_ref, st_ref = rest
    else:
        o_ref, st_ref = rest

    @pl.when(pl.program_id(1) == 0)
    def _():
        st_ref[...] = st0_ref[...]

    mk = _GlaMasks(reverse)
    outs_by_chunk, st = _gla_block(q_ref, k_ref, v_ref, la_ref, [st_ref[p] for p in range(B_PAIRS)],
                                   mk, reverse, True)
    for ci, o in outs_by_chunk.items():
        rows = slice(ci * B_CHUNK, (ci + 1) * B_CHUNK)
        if reverse:
            o = o + of_ref[rows, :]
            outs = []
            for h in range(B_HEADS):
                oh = o[:, h * B_DV:(h + 1) * B_DV]
                oh = (oh * lax.rsqrt(jnp.mean(oh * oh, axis=-1, keepdims=True) + RMS_EPS)) * gg_ref[...]
                outs.append(oh * _silu(r_ref[rows, h * B_DV:(h + 1) * B_DV].astype(F32)))
            o_ref[rows, :] = jnp.concatenate(outs, axis=1).astype(BF16)
        else:
            o_ref[rows, :] = o
    for p in range(B_PAIRS):
        st_ref[p] = st[p]


def _gla(z, la, st0, b, t, reverse, o_fwd=None, gla_g=None):
    tb = TB_GLA
    nt = t // tb
    d = 1 if reverse else 0

    def tok(w, lane_blk=0):
        if reverse:
            return pl.BlockSpec((tb, w), lambda bi, j: (bi * nt + nt - 1 - j, lane_blk))
        return pl.BlockSpec((tb, w), lambda bi, j: (bi * nt + j, lane_blk))

    in_specs = [tok(*Z_QB), tok(*Z_KB), tok(*Z_VB), tok(B_QK, d),
                pl.BlockSpec((None, B_PAIRS, PAIR_V, PAIR_QK), lambda bi, j: (bi, 0, 0, 0))]
    args = [z, z, z, la, st0]
    if reverse:
        in_specs += [tok(B_V), tok(*Z_RB), _const_spec((1, B_DV))]
        args += [o_fwd, z, gla_g.reshape(1, B_DV)]
    return pl.pallas_call(
        functools.partial(_gla_kernel, reverse=reverse),
        grid=(b, nt),
        in_specs=in_specs,
        out_specs=tok(B_V),
        out_shape=jax.ShapeDtypeStruct((b * t, B_V), BF16 if reverse else F32),
        scratch_shapes=[pltpu.VMEM((B_PAIRS, PAIR_V, PAIR_QK), F32)],
        compiler_params=_cparams("parallel", "arbitrary"),
        name="gla_bwd" if reverse else "gla_fwd",
    )(*args)


def _shift_rows(e, d):
    n = e.shape[0]
    return pltpu.roll(e, (-d) % n, 0)


def _pool_kernel(x_ref, xp_ref, xn_ref, mod_ref, g_ref, w_ref, ps_ref, o_ref, *, seq):
    i = pl.program_id(1)
    tm = x_ref.shape[0]
    g = g_ref[...]
    row = pl.program_id(0)
    shift = _mod(mod_ref, 3, row)
    scale = _mod(mod_ref, 4, row)
    x = x_ref[...]
    h = _modulate(x, g, shift, scale)
    hp = jnp.where(i > 0, _modulate(xp_ref[...], g, shift, scale), 0.0)
    hn = jnp.where(i < pl.num_programs(1) - 1, _modulate(xn_ref[...], g, shift, scale), 0.0)
    groups, totals = [], []
    for gi, w in enumerate(POOL_WINDOWS):
        ls = slice(gi * POOL_GROUP, (gi + 1) * POOL_GROUP)
        groups.append(h[:, ls])
        s = jnp.concatenate([hp[:, ls], h[:, ls], hn[:, ls]], axis=0)
        k = 1
        while k < w:
            s = s + _shift_rows(s, -k)
            k *= 2
        if w // 2 > 1:
            s = _shift_rows(s, w // 2 - 1)
        totals.append(s[HALO:HALO + tm, :])

    def finish(scales):
        ys = [jnp.dot((totals[gi] * scales[gi] - groups[gi]).astype(BF16), w_ref[gi], preferred_element_type=F32)
              for gi in range(len(POOL_WINDOWS))]
        y = jnp.concatenate(ys, axis=1) * ps_ref[...]
        o_ref[...] = x + _mod(mod_ref, 5, row) * y

    clipped = (i == 0) | (i == pl.num_programs(1) - 1)

    @pl.when(clipped)
    def _():
        t = i * tm + lax.broadcasted_iota(jnp.int32, (tm, 1), 0)
        finish([1.0 / (jnp.minimum(t + w // 2, seq) - jnp.maximum(t - w // 2, 0)).astype(F32)
                for w in POOL_WINDOWS])

    @pl.when(jnp.logical_not(clipped))
    def _():
        finish([1.0 / w for w in POOL_WINDOWS])


def _pool(x, mods, layer, g, w_pool, pool_scale):
    b, t, d = x.shape
    tm = TM_POOL
    hb = tm // HALO
    last = t // HALO - 1
    return pl.pallas_call(
        functools.partial(_pool_kernel, seq=t),
        grid=(b, t // tm),
        in_specs=[
            pl.BlockSpec((None, tm, d), lambda bi, i: (bi, i, 0)),
            pl.BlockSpec((None, HALO, d), lambda bi, i: (bi, jnp.maximum(i * hb - 1, 0), 0)),
            pl.BlockSpec((None, HALO, d), lambda bi, i: (bi, jnp.minimum((i + 1) * hb, last), 0)),
            _mod_spec(layer, d),
            _const_spec((1, d)),
            _const_spec(w_pool.shape),
            _const_spec((1, d)),
        ],
        out_specs=pl.BlockSpec((None, tm, d), lambda bi, i: (bi, i, 0)),
        out_shape=jax.ShapeDtypeStruct((b, t, d), F32),
        compiler_params=_cparams("parallel", "parallel"),
        name="pool_mixer",
    )(x, x, x, mods, g.reshape(1, d), w_pool, pool_scale.reshape(1, d))


def _rope_tables(t, pad_rows):
    n = A_HEAD_DIM // 4
    freqs = ROPE_BASE ** (-jnp.arange(n, dtype=F32) / n)
    pos = jnp.arange(t)
    rows = (pos // GRID_W).astype(F32)
    cols = (pos % GRID_W).astype(F32)
    ang_r = rows[:, None] * freqs
    ang_c = cols[:, None] * freqs
    cos = jnp.concatenate([jnp.cos(ang_r), jnp.cos(ang_r), jnp.cos(ang_c), jnp.cos(ang_c)], axis=1)
    sin = jnp.concatenate([-jnp.sin(ang_r), jnp.sin(ang_r), -jnp.sin(ang_c), jnp.sin(ang_c)], axis=1)
    cos = jnp.concatenate([jnp.tile(cos, (1, 2)), jnp.ones((pad_rows, LANES), F32)], axis=0)
    sin = jnp.concatenate([jnp.tile(sin, (1, 2)), jnp.zeros((pad_rows, LANES), F32)], axis=0)
    return cos, sin


def kernel(x, c, ctx, c_ctx, w_mod, b_mod, norm_g, ffn1_wi, ffn1_wo, ffn2_wi, ffn2_wo,
           w_in, w_a2_f, b_a_f, w_a2_b, b_a_b, sink, gla_g, w_out, w_pool, pool_scale, final_g):
    b, t, d = x.shape
    lctx = ctx.shape[1]
    n_tok = b * t
    ctx_row = b

    cond = jnp.zeros((COND_ROWS, d), F32).at[:b].set(c).at[ctx_row].set(c_ctx)
    mods = _adaln(cond, w_mod, b_mod)
    x = x.reshape(n_tok, d)

    w_main = w_in[0][:, :PROJ_MAIN].astype(BF16)
    w_zg = jnp.zeros((d, LANES), F32).at[:, :2 * B_GATE_RANK].set(w_in[0][:, PROJ_MAIN:]).astype(BF16)
    w2 = jnp.zeros((LANES, 2 * B_QK), F32)
    w2 = w2.at[:B_GATE_RANK, :B_QK].set(w_a2_f[0]).at[B_GATE_RANK:2 * B_GATE_RANK, B_QK:].set(w_a2_b[0]).astype(BF16)
    b2 = jnp.concatenate([b_a_f[0], b_a_b[0]]).reshape(1, 2 * B_QK)
    cos, sin = _rope_tables(t, TM_FFN_PROJ)
    xc, z, la = _ffn(
        x, n_tok, t, mods, norm_g[0, 0], ffn1_wi, ffn1_wo, 0, 0, tm=TM_FFN_PROJ,
        ctx=ctx.reshape(b * lctx, d), ctx_row=ctx_row, proj=(norm_g[0, 1], w_main, w_zg, w2, b2, cos, sin))

    o_a = _attention(sink[0], z, b, t, lctx)
    st_f = _gla_ctx_state(z, la, b, t, lctx, False)
    st_b = _gla_ctx_state(z, la, b, t, lctx, True)
    o_f = _gla(z, la, st_f, b, t, False)
    o_g = _gla(z, la, st_b, b, t, True, o_f, gla_g[0])
    x = _ffn(xc, n_tok, t, mods, norm_g[0, 2], ffn2_wi, ffn2_wo, 0, 6, mix=(o_a, o_g, w_out[0].astype(BF16)))

    x = _ffn(x, n_tok, t, mods, norm_g[1, 0], ffn1_wi, ffn1_wo, 1, 0)
    x = _pool(x.reshape(b, t, d), mods, 1, norm_g[1, 1], w_pool[0].astype(BF16), pool_scale[0])
    x = _ffn(x.reshape(n_tok, d), n_tok, t, mods, norm_g[1, 2], ffn2_wi, ffn2_wo, 1, 6, final_g=final_g)
    return x.reshape(b, t, d)
```

```python
import functools

import jax
import jax.numpy as jnp
import numpy as np
from jax import lax
from jax.experimental import pallas as pl
from jax.experimental.pallas import tpu as pltpu

F32 = jnp.float32
BF16 = jnp.bfloat16

D_MODEL = 1024
N_MOD = 9
D_FF = 2816
RMS_EPS = 1e-6
GRID_W = 64
A_HEADS = 8
A_KV_HEADS = 2
A_HEAD_DIM = 64
WINDOW = 128
ROPE_BASE = 10000.0
B_HEADS = 4
B_DK = 64
B_DV = 128
B_GATE_RANK = 16
B_GATE_NORM = 16.0
B_CHUNK = 64
POOL_WINDOWS = (2, 4, 8, 16)
POOL_GROUP = D_MODEL // len(POOL_WINDOWS)
A_Q = A_HEADS * A_HEAD_DIM
A_KV = A_KV_HEADS * A_HEAD_DIM
B_QK = B_HEADS * B_DK
B_V = B_HEADS * B_DV
PROJ_MAIN = A_Q + 2 * A_KV + 2 * B_QK + 2 * B_V
KV_DUP = 4 * A_KV

LOG2E = 1.4426950408889634
LANES = 128
COND_ROWS = 8
VMEM_LIMIT = 56 * 1024 * 1024

TM_FFN = 1024
SUB_FFN = 256
TM_FFN_PROJ = 512
TM_POOL = 1024
TB_GLA = 1024
GLA_GROUP = 4
HALO = 8


def _cparams(*sem):
    return pltpu.CompilerParams(dimension_semantics=sem, vmem_limit_bytes=VMEM_LIMIT)


def _const_spec(shape):
    nd = len(shape)
    return pl.BlockSpec(shape, lambda *_: (0,) * nd, pipeline_mode=pl.Buffered(1))


def _modulate(x, g, shift, scale):
    y = x * lax.rsqrt(jnp.mean(x * x, axis=-1, keepdims=True) + RMS_EPS)
    return (y * g) * (1.0 + scale) + shift


def _silu(a):
    return a * jax.nn.sigmoid(a)


ADALN_MODS_PER_STEP = 3


def _adaln_kernel(c_ref, w_ref, b_ref, o_ref):
    d = c_ref.shape[1]
    s = _silu(c_ref[...]).astype(BF16)
    r = jnp.dot(s, w_ref[...].astype(BF16), preferred_element_type=F32) + b_ref[...]
    for k in range(ADALN_MODS_PER_STEP):
        o_ref[k] = r[:, k * d:(k + 1) * d]


def _adaln(cond, w_mod, b_mod):
    depth, d, n = w_mod.shape
    per = ADALN_MODS_PER_STEP
    tn = per * d
    return pl.pallas_call(
        _adaln_kernel,
        grid=(depth, N_MOD // per),
        in_specs=[
            pl.BlockSpec((COND_ROWS, d), lambda l, j: (0, 0)),
            pl.BlockSpec((None, d, tn), lambda l, j: (l, 0, j)),
            pl.BlockSpec((None, 1, tn), lambda l, j: (l, 0, j)),
        ],
        out_specs=pl.BlockSpec((None, per, COND_ROWS, d), lambda l, j: (l, j, 0, 0)),
        out_shape=jax.ShapeDtypeStruct((depth, N_MOD, COND_ROWS, d), F32),
        compiler_params=_cparams("parallel", "parallel"),
        name="adaln",
    )(cond, w_mod, b_mod.reshape(depth, 1, n))


def _mod_spec(layer, d):
    return pl.BlockSpec((None, N_MOD, COND_ROWS, d), lambda *_: (layer, 0, 0, 0))


def _mod(mod_ref, idx, row):
    return mod_ref[idx, pl.ds(row, 1), :]


FFN_WCHUNK = 256
FFN_NCH = D_FF // FFN_WCHUNK
FFN_CHUNKS_PER_STEP = 2
FFN_NW = 2 * FFN_NCH // FFN_CHUNKS_PER_STEP
FFN_WO_ROWS = D_FF // FFN_NW
FFN_MIX_LEAD = 5


def _ffn_tile(x_ref, o_ref, mod_ref, row, g_ref, wi_s, wo_s, mi, mix, fg_ref, proj):
    sub = min(SUB_FFN, x_ref.shape[0])
    nsub = x_ref.shape[0] // sub

    def residual_in(s):
        rows = slice(s * sub, (s + 1) * sub)
        x = x_ref[rows, :]
        if mix is not None:
            oa_ref, ob_ref, w_ref = mix
            y0 = (jnp.dot(oa_ref[rows, :], w_ref[:A_Q, :], preferred_element_type=F32)
                  + jnp.dot(ob_ref[rows, :], w_ref[A_Q:, :], preferred_element_type=F32))
            x = x + _mod(mod_ref, 5, row) * y0
        return x

    def normed(x):
        return _modulate(x, g_ref[...], _mod(mod_ref, mi, row), _mod(mod_ref, mi + 1, row)).astype(BF16)

    def act_chunk(h, c):
        a = jnp.dot(h, wi_s[c], preferred_element_type=F32)
        u = jnp.dot(h, wi_s[FFN_NCH + c], preferred_element_type=F32)
        return (_silu(a) * u).astype(BF16)

    x_next = residual_in(0)
    h_next = normed(x_next)
    early = []
    pending = []
    for s in range(nsub):
        rows = slice(s * sub, (s + 1) * sub)
        x, h = x_next, h_next
        acts, early = early, []
        for c in range(len(acts), FFN_NCH):
            if s + 1 < nsub:
                if c == FFN_NCH - FFN_MIX_LEAD:
                    x_next = residual_in(s + 1)
                if c == FFN_NCH - 1:
                    h_next = normed(x_next)
                    early = [act_chunk(h_next, 0)]
            acts.append(act_chunk(h, c))
        y = jnp.dot(jnp.concatenate(acts, axis=1), wo_s[...], preferred_element_type=F32)
        o = x + (0.5 * _mod(mod_ref, mi + 2, row)) * y
        if fg_ref is not None:
            o = (o * lax.rsqrt(jnp.mean(o * o, axis=-1, keepdims=True) + RMS_EPS)) * fg_ref[...]
        o_ref[rows, :] = o
        if proj is not None:
            pending.append(
                (_modulate(o, proj[0][...], _mod(mod_ref, 3, row), _mod(mod_ref, 4, row)).astype(BF16), rows))
    for h2, rows in pending:
        for stage in proj[1]:
            stage(h2, rows)


def _ffn_kernel(*refs, mi, nx, tiles_per_batch, ctx_row, has_ctx, has_mix, final, has_proj):
    refs = list(refs)
    x_ref = refs.pop(0)
    c_ref = refs.pop(0) if has_ctx else None
    mod_ref, g_ref, wi_ref, wo_ref = refs[:4]
    refs = refs[4:]
    mix = tuple(refs[:3]) if has_mix else None
    refs = refs[3:] if has_mix else refs
    fg_ref = refs.pop(0) if final else None
    proj = None
    if has_proj:
        g1_ref, wp_ref, wz_ref, w2_ref, b2_ref, cos_ref, sin_ref = refs[:7]
        proj = (g1_ref, _proj_stage_fns(wp_ref, wz_ref, w2_ref, b2_ref, cos_ref, sin_ref, refs[8:8 + N_PROJ_OUT]))
        refs = refs[7:8] + refs[8 + N_PROJ_OUT:]
    o_ref, wi_s, wo_s = refs
    j = pl.program_id(0)

    @pl.when(j < FFN_NW)
    def _():
        for k in range(FFN_CHUNKS_PER_STEP):
            wi_s[j * FFN_CHUNKS_PER_STEP + k] = wi_ref[:, k * FFN_WCHUNK:(k + 1) * FFN_WCHUNK].astype(BF16)
        wo_s[pl.ds(pl.multiple_of(j * FFN_WO_ROWS, FFN_WO_ROWS), FFN_WO_ROWS), :] = wo_ref[...].astype(BF16)

    @pl.when((j >= FFN_NW) & (j < FFN_NW + nx))
    def _():
        _ffn_tile(x_ref, o_ref, mod_ref, (j - FFN_NW) // tiles_per_batch, g_ref, wi_s, wo_s, mi, mix, fg_ref, proj)

    if has_ctx:
        @pl.when(j >= FFN_NW + nx)
        def _():
            _ffn_tile(c_ref, o_ref, mod_ref, ctx_row, g_ref, wi_s, wo_s, mi, None, fg_ref, proj)


def _ffn(x, n_rows, tok_per_batch, mods, g, wi, wo, layer, mi, *, tm=TM_FFN, ctx=None, ctx_row=None, mix=None,
         final_g=None, proj=None):
    d = x.shape[1]
    nx = n_rows // tm
    nc = 0 if ctx is None else ctx.shape[0] // tm
    tiles_per_batch = tok_per_batch // tm

    def tile(j):
        return jnp.clip(j - FFN_NW, 0, nx + nc - 1)

    def x_tile(j):
        return jnp.minimum(tile(j), nx - 1)

    def wchunk(j):
        return jnp.minimum(j, FFN_NW - 1)

    def pos_blk(j):
        return jnp.where(tile(j) < nx, tile(j) % tiles_per_batch, tiles_per_batch)

    in_specs = [pl.BlockSpec((tm, d), lambda j: (x_tile(j), 0))]
    args = [x]
    if ctx is not None:
        in_specs.append(pl.BlockSpec((tm, d), lambda j: (jnp.clip(j - FFN_NW - nx, 0, nc - 1), 0)))
        args.append(ctx)
    in_specs += [
        _mod_spec(layer, d),
        _const_spec((1, d)),
        pl.BlockSpec((None, d, FFN_WCHUNK * FFN_CHUNKS_PER_STEP), lambda j: (layer, 0, wchunk(j))),
        pl.BlockSpec((None, FFN_WO_ROWS, d), lambda j: (layer, wchunk(j), 0)),
    ]
    args += [mods, g.reshape(1, d), wi, wo]
    if mix is not None:
        oa, ob, w_out = mix
        in_specs += [pl.BlockSpec((tm, A_Q), lambda j: (x_tile(j), 0)),
                     pl.BlockSpec((tm, B_V), lambda j: (x_tile(j), 0)),
                     _const_spec(w_out.shape)]
        args += [oa, ob, w_out]
    if final_g is not None:
        in_specs.append(_const_spec((1, d)))
        args.append(final_g.reshape(1, d))
    n_out_rows = (nx + nc) * tm
    out_specs = [pl.BlockSpec((tm, d), lambda j: (tile(j), 0))]
    out_shape = [jax.ShapeDtypeStruct((n_out_rows, d), F32)]
    if proj is not None:
        g1, w_main, w_zg, w2, b2, cos, sin = proj
        in_specs += [_const_spec((1, d)), _const_spec(w_main.shape), _const_spec(w_zg.shape), _const_spec(w2.shape),
                     _const_spec(b2.shape),
                     pl.BlockSpec((tm, LANES), lambda j: (pos_blk(j), 0)),
                     pl.BlockSpec((tm, LANES), lambda j: (pos_blk(j), 0))]
        args += [g1.reshape(1, d), w_main, w_zg, w2, b2, cos, sin]
        out_specs += [pl.BlockSpec((tm, w), lambda j: (tile(j), 0)) for w, _ in PROJ_OUTS]
        out_shape += [jax.ShapeDtypeStruct((n_out_rows, w), dt) for w, dt in PROJ_OUTS]
    res = pl.pallas_call(
        functools.partial(_ffn_kernel, mi=mi, nx=nx, tiles_per_batch=tiles_per_batch, ctx_row=ctx_row,
                          has_ctx=ctx is not None, has_mix=mix is not None,
                          final=final_g is not None, has_proj=proj is not None),
        grid=(FFN_NW + nx + nc,),
        in_specs=in_specs,
        out_specs=out_specs,
        out_shape=out_shape,
        scratch_shapes=[pltpu.VMEM((2 * FFN_NCH, d, FFN_WCHUNK), BF16), pltpu.VMEM((D_FF, d), BF16)],
        compiler_params=_cparams("arbitrary"),
        name="ffn_proj" if proj is not None else "ffn",
    )(*args)
    return res if proj is not None else res[0]


def _swap16(x):
    lane = lax.broadcasted_iota(jnp.int32, x.shape, 1)
    return jnp.where((lane & 16) == 0, pltpu.roll(x, LANES - 16, 1), pltpu.roll(x, 16, 1))


def _dup_halves(x):
    lane = lax.broadcasted_iota(jnp.int32, x.shape, 1)
    r = pltpu.roll(x, LANES // 2, 1)
    lo = lane < LANES // 2
    return jnp.where(lo, x, r), jnp.where(lo, r, x)


def _halves_with_ones(x):
    lane = lax.broadcasted_iota(jnp.int32, x.shape, 1)
    r = pltpu.roll(x, LANES // 2, 1)
    lo = lane < LANES // 2
    return jnp.where(lo, x, 1.0), jnp.where(lo, r, 1.0)


Z_QA = (A_Q, 0)
Z_KV = (KV_DUP, 1)
Z_VB = (B_V, 2)
Z_RB = (B_V, 3)
Z_QB = (B_QK, 8)
Z_KB = (B_QK, 9)
Z_WIDTH = (Z_KB[1] + 1) * B_QK
PROJ_OUTS = ((Z_WIDTH, BF16), (2 * B_QK, F32))
N_PROJ_OUT = len(PROJ_OUTS)


def _z_cols(field):
    w, blk = field
    return slice(w * blk, w * (blk + 1))


def _proj_stage_fns(w_ref, wz_ref, w2_ref, b2_ref, cos_ref, sin_ref, outs):
    z_ref, la_ref = outs
    qa0, kv0 = _z_cols(Z_QA).start, _z_cols(Z_KV).start
    scale = A_HEAD_DIM ** -0.5 * LOG2E
    o_b = A_Q + 2 * A_KV

    def cols(h, lo, hi):
        return jnp.dot(h, w_ref[:, lo:hi], preferred_element_type=F32)

    def rope(v, rows):
        return v * cos_ref[rows, :] + _swap16(v) * sin_ref[rows, :]

    def attn_q(h, rows):
        z = cols(h, 0, A_Q)
        for p in range(A_Q // LANES):
            q = z[:, p * LANES:(p + 1) * LANES]
            z_ref[rows, qa0 + p * LANES:qa0 + (p + 1) * LANES] = (rope(q, rows) * scale).astype(BF16)

    def attn_kv(h, rows):
        z = cols(h, A_Q, A_Q + 2 * A_KV)
        k0, k1 = _dup_halves(rope(z[:, :A_KV], rows))
        v0, v1 = _halves_with_ones(z[:, A_KV:])
        for j, piece in enumerate((k0, k1, v0, v1)):
            z_ref[rows, kv0 + j * LANES:kv0 + (j + 1) * LANES] = piece.astype(BF16)

    def gla_qk(h, rows):
        z = cols(h, o_b, o_b + 2 * B_QK)
        z_ref[rows, _z_cols(Z_QB)] = z[:, :B_QK].astype(BF16)
        z_ref[rows, _z_cols(Z_KB)] = z[:, B_QK:].astype(BF16)

    def gla_v(h, rows):
        z_ref[rows, _z_cols(Z_VB)] = cols(h, o_b + 2 * B_QK, o_b + 2 * B_QK + B_V).astype(BF16)

    def gla_r(h, rows):
        z_ref[rows, _z_cols(Z_RB)] = cols(h, o_b + 2 * B_QK + B_V, o_b + 2 * B_QK + 2 * B_V).astype(BF16)

    def gla_gate(h, rows):
        zg = jnp.dot(h, wz_ref[...], preferred_element_type=F32).astype(BF16)
        pre = jnp.dot(zg, w2_ref[...], preferred_element_type=F32) + b2_ref[...]
        la_ref[rows, :] = (jnp.minimum(pre, 0.0) - jnp.log1p(jnp.exp(-jnp.abs(pre)))) / B_GATE_NORM

    return (gla_gate, attn_q, attn_kv, gla_qk, gla_v, gla_r)


ATTN_QBLOCKS = 8


def _attn_kernel(sink_ref, q_ref, kvp_ref, kvc_ref, kvn_ref, ckv_ref, o_ref, *, nsteps):
    n = pl.program_id(1)
    blk = WINDOW
    row = lax.broadcasted_iota(jnp.int32, (blk, blk), 0)
    col = lax.broadcasted_iota(jnp.int32, (blk, blk), 1)
    ninf = jnp.float32(-jnp.inf)
    lane = lax.broadcasted_iota(jnp.int32, (blk, LANES), 1)
    lo = lane < LANES // 2
    rep = A_HEADS // A_KV_HEADS

    def key_block(i, lanes):
        if i == 0:
            return kvp_ref[:, lanes]
        if i == ATTN_QBLOCKS + 1:
            return kvn_ref[:, lanes]
        return kvc_ref[(i - 1) * blk:i * blk, lanes]

    def band(qb, lanes):
        return jnp.concatenate([key_block(qb + j, lanes) for j in range(3)] + [ckv_ref[:, lanes]], axis=0)

    kd = [[band(qb, slice(g * LANES, (g + 1) * LANES)) for g in range(A_KV_HEADS)] for qb in range(ATTN_QBLOCKS)]
    vd = [[band(qb, slice((A_KV_HEADS + g) * LANES, (A_KV_HEADS + g + 1) * LANES)) for g in range(A_KV_HEADS)]
          for qb in range(ATTN_QBLOCKS)]
    nkb = kd[0][0].shape[0] // blk
    bias_prev = [jnp.where((col >= row) & ((n > 0) if qb == 0 else True), 0.0, ninf) for qb in range(ATTN_QBLOCKS)]
    bias_next = [jnp.where((col <= row) & ((n < nsteps - 1) if qb == ATTN_QBLOCKS - 1 else True), 0.0, ninf)
                 for qb in range(ATTN_QBLOCKS)]

    def scores(qb, p):
        pair = q_ref[qb * blk:(qb + 1) * blk, p * LANES:(p + 1) * LANES]
        zero = jnp.zeros_like(pair)
        lhs = jnp.concatenate([jnp.where(lo, pair, zero), jnp.where(lo, zero, pair)], axis=0)
        return lax.dot_general(lhs, kd[qb][(2 * p) // rep], (((1,), (1,)), ((), ())), preferred_element_type=F32)

    def softmax(qb, p, s):
        ps, sinks = [], []
        for r in range(2):
            sk = sink_ref[2 * p + r] * LOG2E
            cb = [s[r * blk:(r + 1) * blk, j * blk:(j + 1) * blk] for j in range(nkb)]
            cb[0] = cb[0] + bias_prev[qb]
            cb[2] = cb[2] + bias_next[qb]
            mx = cb[0]
            for c in cb[1:]:
                mx = jnp.maximum(mx, c)
            m = jnp.maximum(jnp.max(mx, axis=-1, keepdims=True), sk)
            ps.append(jnp.concatenate([jnp.exp2((c - m).astype(BF16)) for c in cb], axis=1))
            sinks.append(jnp.exp2(sk - m))
        return jnp.concatenate(ps, axis=0), sinks

    def weighted_values(qb, p, probs, sinks):
        o = jnp.dot(probs, vd[qb][(2 * p) // rep], preferred_element_type=F32)
        o_even, o_odd = o[:blk, :], o[blk:, :]
        num = jnp.where(lo, o_even, pltpu.roll(o_odd, LANES // 2, 1))
        den = jnp.where(lo, pltpu.roll(o_even, LANES // 2, 1) + sinks[0], o_odd + sinks[1])
        o_ref[qb * blk:(qb + 1) * blk, p * LANES:(p + 1) * LANES] = (num / den).astype(BF16)

    units = [(qb, p) for qb in range(ATTN_QBLOCKS) for p in range(A_HEADS // 2)]
    s_next = scores(*units[0])
    for i, unit in enumerate(units):
        s_cur = s_next
        if i + 1 < len(units):
            s_next = scores(*units[i + 1])
        probs, sinks = softmax(*unit, s_cur)
        weighted_values(*unit, probs, sinks)


def _attention(sink, z, b, t, lctx):
    nb = t // WINDOW
    tq = ATTN_QBLOCKS * WINDOW
    nsteps = t // tq
    ctx0 = (b * t) // lctx

    def edge(off):
        return pl.BlockSpec((WINDOW, KV_DUP),
                            lambda bi, n: (bi * nb + jnp.clip(n * ATTN_QBLOCKS + off, 0, nb - 1), Z_KV[1]))

    return pl.pallas_call(
        functools.partial(_attn_kernel, nsteps=nsteps),
        grid=(b, nsteps),
        in_specs=[
            pl.BlockSpec(memory_space=pltpu.SMEM),
            pl.BlockSpec((tq, A_Q), lambda bi, n: (bi * nsteps + n, Z_QA[1])),
            edge(-1),
            pl.BlockSpec((tq, KV_DUP), lambda bi, n: (bi * nsteps + n, Z_KV[1])),
            edge(ATTN_QBLOCKS),
            pl.BlockSpec((lctx, KV_DUP), lambda bi, n: (ctx0 + bi, Z_KV[1])),
        ],
        out_specs=pl.BlockSpec((tq, A_Q), lambda bi, n: (bi * nsteps + n, 0)),
        out_shape=jax.ShapeDtypeStruct((b * t, A_Q), BF16),
        compiler_params=_cparams("parallel", "parallel"),
        name="window_attn",
    )(sink, z, z, z, z, z)


B_PAIRS = B_HEADS // 2
PAIR_QK = 2 * B_DK
PAIR_V = 2 * B_DV


class _GlaMasks:
    def __init__(self, reverse):
        c = B_CHUNK
        i = lax.broadcasted_iota(jnp.int32, (c, c), 0)
        j = lax.broadcasted_iota(jnp.int32, (c, c), 1)
        self.tri = jnp.where((j >= i) if reverse else (j <= i), 1.0, 0.0).astype(BF16)
        lane_qk = lax.broadcasted_iota(jnp.int32, (c, B_QK), 1)
        self.k_head = [(lane_qk >= h * B_DK) & (lane_qk < (h + 1) * B_DK) for h in range(B_HEADS)]
        lane_v = lax.broadcasted_iota(jnp.int32, (c, B_V), 1)
        self.v_head = [(lane_v >= h * B_DV) & (lane_v < (h + 1) * B_DV) for h in range(B_HEADS)]
        ai = lax.broadcasted_iota(jnp.int32, (c, B_HEADS * c), 0)
        aj = lax.broadcasted_iota(jnp.int32, (c, B_HEADS * c), 1) % c
        self.causal = (aj >= ai) if reverse else (aj <= ai)
        sr = lax.broadcasted_iota(jnp.int32, (PAIR_V, PAIR_QK), 0) // B_DV
        sl = lax.broadcasted_iota(jnp.int32, (PAIR_V, PAIR_QK), 1) // B_DK
        self.state_diag = sr == sl


def _gla_block(q_ref, k_ref, v_ref, la_ref, st, mk, reverse, want_out):
    c = B_CHUNK
    nc = k_ref.shape[0] // c
    order = list(reversed(range(nc))) if reverse else list(range(nc))
    rows = {ci: slice(ci * c, (ci + 1) * c) for ci in order}
    tb = (((1,), (1,)), ((), ()))
    ta = (((0,), (0,)), ((), ()))

    g = {}
    for ci in order:
        la = la_ref[rows[ci], :]
        hi = la.astype(BF16)
        lo = (la - hi.astype(F32)).astype(BF16)
        g[ci] = jnp.dot(mk.tri, hi, preferred_element_type=F32) + jnp.dot(mk.tri, lo, preferred_element_type=F32)

    o = {}
    for g0 in range(0, nc, GLA_GROUP):
        group = order[g0:g0 + GLA_GROUP]
        dec, k_end, q_t, kstack = {}, {}, {}, {}
        for ci in group:
            k = k_ref[rows[ci], :].astype(F32)
            g_last = g[ci][0:1, :] if reverse else g[ci][c - 1:c, :]
            dec[ci] = jnp.exp(g_last)
            k_end[ci] = (k * jnp.exp(g_last - g[ci])).astype(BF16)
            if want_out:
                q_t[ci] = ((q_ref[rows[ci], :].astype(F32) * (B_DK ** -0.5)) * jnp.exp(g[ci])).astype(BF16)
                k_t = k * jnp.exp(-g[ci])
                kstack[ci] = jnp.concatenate([jnp.where(m, k_t, 0.0) for m in mk.k_head], axis=0).astype(BF16)

        ds = {}
        for ci in group:
            v = v_ref[rows[ci], :]
            ds[ci] = [jnp.where(mk.state_diag,
                                lax.dot_general(v[:, p * PAIR_V:(p + 1) * PAIR_V],
                                                k_end[ci][:, p * PAIR_QK:(p + 1) * PAIR_QK], ta,
                                                preferred_element_type=F32), 0.0) for p in range(B_PAIRS)]

        st_prev = {}
        for ci in group:
            st_prev[ci] = st
            st = [st[p] * dec[ci][:, p * PAIR_QK:(p + 1) * PAIR_QK] + ds[ci][p] for p in range(B_PAIRS)]
        if not want_out:
            continue

        att = {}
        for ci in group:
            a = lax.dot_general(q_t[ci], kstack[ci], tb, preferred_element_type=F32)
            att[ci] = jnp.where(mk.causal, a, 0.0).astype(BF16)
        for ci in group:
            v = v_ref[rows[ci], :]
            vstack = jnp.concatenate([jnp.where(m, v, jnp.zeros_like(v)) for m in mk.v_head], axis=0)
            o[ci] = jnp.dot(att[ci], vstack, preferred_element_type=F32) + jnp.concatenate(
                [lax.dot_general(q_t[ci][:, p * PAIR_QK:(p + 1) * PAIR_QK], st_prev[ci][p].astype(BF16), tb,
                                 preferred_element_type=F32) for p in range(B_PAIRS)], axis=1)
    if not want_out:
        return None, st
    return o, st


def _gla_ctx_kernel(k_ref, v_ref, la_ref, st_ref, *, reverse):
    mk = _GlaMasks(reverse)
    st = [jnp.zeros((PAIR_V, PAIR_QK), F32) for _ in range(B_PAIRS)]
    _, st = _gla_block(None, k_ref, v_ref, la_ref, st, mk, reverse, False)
    for p in range(B_PAIRS):
        st_ref[p] = st[p]


def _gla_ctx_state(z, la, b, t, l, reverse):
    d = 1 if reverse else 0
    ctx0 = (b * t) // l
    return pl.pallas_call(
        functools.partial(_gla_ctx_kernel, reverse=reverse),
        grid=(b,),
        in_specs=[
            pl.BlockSpec((l, B_QK), lambda bi: (ctx0 + bi, Z_KB[1])),
            pl.BlockSpec((l, B_V), lambda bi: (ctx0 + bi, Z_VB[1])),
            pl.BlockSpec((l, B_QK), lambda bi: (ctx0 + bi, d)),
        ],
        out_specs=pl.BlockSpec((None, B_PAIRS, PAIR_V, PAIR_QK), lambda bi: (bi, 0, 0, 0)),
        out_shape=jax.ShapeDtypeStruct((b, B_PAIRS, PAIR_V, PAIR_QK), F32),
        compiler_params=_cparams("parallel"),
        name="gla_ctx_state",
    )(z, z, la)


def _gla_kernel(q_ref, k_ref, v_ref, la_ref, st0_ref, *rest, reverse):
    if reverse:
        of_ref, r_ref, gg_ref, o_ref, st_ref = rest
    else:
        o_ref, st_ref = rest

    @pl.when(pl.program_id(1) == 0)
    def _():
        st_ref[...] = st0_ref[...]

    mk = _GlaMasks(reverse)
    outs_by_chunk, st = _gla_block(q_ref, k_ref, v_ref, la_ref, [st_ref[p] for p in range(B_PAIRS)],
                                   mk, reverse, True)
    for ci, o in outs_by_chunk.items():
        rows = slice(ci * B_CHUNK, (ci + 1) * B_CHUNK)
        if reverse:
            o = o + of_ref[rows, :]
            outs = []
            for h in range(B_HEADS):
                oh = o[:, h * B_DV:(h + 1) * B_DV]
                oh = (oh * lax.rsqrt(jnp.mean(oh * oh, axis=-1, keepdims=True) + RMS_EPS)) * gg_ref[...]
                outs.append(oh * _silu(r_ref[rows, h * B_DV:(h + 1) * B_DV].astype(F32)))
            o_ref[rows, :] = jnp.concatenate(outs, axis=1).astype(BF16)
        else:
            o_ref[rows, :] = o
    for p in range(B_PAIRS):
        st_ref[p] = st[p]


def _gla(z, la, st0, b, t, reverse, o_fwd=None, gla_g=None):
    tb = TB_GLA
    nt = t // tb
    d = 1 if reverse else 0

    def tok(w, lane_blk=0):
        if reverse:
            return pl.BlockSpec((tb, w), lambda bi, j: (bi * nt + nt - 1 - j, lane_blk))
        return pl.BlockSpec((tb, w), lambda bi, j: (bi * nt + j, lane_blk))

    in_specs = [tok(*Z_QB), tok(*Z_KB), tok(*Z_VB), tok(B_QK, d),
                pl.BlockSpec((None, B_PAIRS, PAIR_V, PAIR_QK), lambda bi, j: (bi, 0, 0, 0))]
    args = [z, z, z, la, st0]
    if reverse:
        in_specs += [tok(B_V), tok(*Z_RB), _const_spec((1, B_DV))]
        args += [o_fwd, z, gla_g.reshape(1, B_DV)]
    return pl.pallas_call(
        functools.partial(_gla_kernel, reverse=reverse),
        grid=(b, nt),
        in_specs=in_specs,
        out_specs=tok(B_V),
        out_shape=jax.ShapeDtypeStruct((b * t, B_V), BF16 if reverse else F32),
        scratch_shapes=[pltpu.VMEM((B_PAIRS, PAIR_V, PAIR_QK), F32)],
        compiler_params=_cparams("parallel", "arbitrary"),
        name="gla_bwd" if reverse else "gla_fwd",
    )(*args)


def _shift_rows(e, d):
    n = e.shape[0]
    return pltpu.roll(e, (-d) % n, 0)


def _pool_kernel(x_ref, xp_ref, xn_ref, mod_ref, g_ref, w_ref, ps_ref, o_ref, *, seq):
    i = pl.program_id(1)
    tm = x_ref.shape[0]
    g = g_ref[...]
    row = pl.program_id(0)
    shift = _mod(mod_ref, 3, row)
    scale = _mod(mod_ref, 4, row)
    x = x_ref[...]
    h = _modulate(x, g, shift, scale)
    hp = jnp.where(i > 0, _modulate(xp_ref[...], g, shift, scale), 0.0)
    hn = jnp.where(i < pl.num_programs(1) - 1, _modulate(xn_ref[...], g, shift, scale), 0.0)
    groups, totals = [], []
    for gi, w in enumerate(POOL_WINDOWS):
        ls = slice(gi * POOL_GROUP, (gi + 1) * POOL_GROUP)
        groups.append(h[:, ls])
        s = jnp.concatenate([hp[:, ls], h[:, ls], hn[:, ls]], axis=0)
        k = 1
        while k < w:
            s = s + _shift_rows(s, -k)
            k *= 2
        if w // 2 > 1:
            s = _shift_rows(s, w // 2 - 1)
        totals.append(s[HALO:HALO + tm, :])

    def finish(scales):
        ys = [jnp.dot((totals[gi] * scales[gi] - groups[gi]).astype(BF16), w_ref[gi], preferred_element_type=F32)
              for gi in range(len(POOL_WINDOWS))]
        y = jnp.concatenate(ys, axis=1) * ps_ref[...]
        o_ref[...] = x + _mod(mod_ref, 5, row) * y

    clipped = (i == 0) | (i == pl.num_programs(1) - 1)

    @pl.when(clipped)
    def _():
        t = i * tm + lax.broadcasted_iota(jnp.int32, (tm, 1), 0)
        finish([1.0 / (jnp.minimum(t + w // 2, seq) - jnp.maximum(t - w // 2, 0)).astype(F32)
                for w in POOL_WINDOWS])

    @pl.when(jnp.logical_not(clipped))
    def _():
        finish([1.0 / w for w in POOL_WINDOWS])


def _pool(x, mods, layer, g, w_pool, pool_scale):
    b, t, d = x.shape
    tm = TM_POOL
    hb = tm // HALO
    last = t // HALO - 1
    return pl.pallas_call(
        functools.partial(_pool_kernel, seq=t),
        grid=(b, t // tm),
        in_specs=[
            pl.BlockSpec((None, tm, d), lambda bi, i: (bi, i, 0)),
            pl.BlockSpec((None, HALO, d), lambda bi, i: (bi, jnp.maximum(i * hb - 1, 0), 0)),
            pl.BlockSpec((None, HALO, d), lambda bi, i: (bi, jnp.minimum((i + 1) * hb, last), 0)),
            _mod_spec(layer, d),
            _const_spec((1, d)),
            _const_spec(w_pool.shape),
            _const_spec((1, d)),
        ],
        out_specs=pl.BlockSpec((None, tm, d), lambda bi, i: (bi, i, 0)),
        out_shape=jax.ShapeDtypeStruct((b, t, d), F32),
        compiler_params=_cparams("parallel", "parallel"),
        name="pool_mixer",
    )(x, x, x, mods, g.reshape(1, d), w_pool, pool_scale.reshape(1, d))


def _rope_tables(t, pad_rows):
    n = A_HEAD_DIM // 4
    freqs = ROPE_BASE ** (-jnp.arange(n, dtype=F32) / n)
    pos = jnp.arange(t)
    rows = (pos // GRID_W).astype(F32)
    cols = (pos % GRID_W).astype(F32)
    ang_r = rows[:, None] * freqs
    ang_c = cols[:, None] * freqs
    cos = jnp.concatenate([jnp.cos(ang_r), jnp.cos(ang_r), jnp.cos(ang_c), jnp.cos(ang_c)], axis=1)
    sin = jnp.concatenate([-jnp.sin(ang_r), jnp.sin(ang_r), -jnp.sin(ang_c), jnp.sin(ang_c)], axis=1)
    cos = jnp.concatenate([jnp.tile(cos, (1, 2)), jnp.ones((pad_rows, LANES), F32)], axis=0)
    sin = jnp.concatenate([jnp.tile(sin, (1, 2)), jnp.zeros((pad_rows, LANES), F32)], axis=0)
    return cos, sin


def kernel(x, c, ctx, c_ctx, w_mod, b_mod, norm_g, ffn1_wi, ffn1_wo, ffn2_wi, ffn2_wo,
           w_in, w_a2_f, b_a_f, w_a2_b, b_a_b, sink, gla_g, w_out, w_pool, pool_scale, final_g):
    b, t, d = x.shape
    lctx = ctx.shape[1]
    n_tok = b * t
    ctx_row = b

    cond = jnp.zeros((COND_ROWS, d), F32).at[:b].set(c).at[ctx_row].set(c_ctx)
    mods = _adaln(cond, w_mod, b_mod)
    x = x.reshape(n_tok, d)

    w_main = w_in[0][:, :PROJ_MAIN].astype(BF16)
    w_zg = jnp.zeros((d, LANES), F32).at[:, :2 * B_GATE_RANK].set(w_in[0][:, PROJ_MAIN:]).astype(BF16)
    w2 = jnp.zeros((LANES, 2 * B_QK), F32)
    w2 = w2.at[:B_GATE_RANK, :B_QK].set(w_a2_f[0]).at[B_GATE_RANK:2 * B_GATE_RANK, B_QK:].set(w_a2_b[0]).astype(BF16)
    b2 = jnp.concatenate([b_a_f[0], b_a_b[0]]).reshape(1, 2 * B_QK)
    cos, sin = _rope_tables(t, TM_FFN_PROJ)
    xc, z, la = _ffn(
        x, n_tok, t, mods, norm_g[0, 0], ffn1_wi, ffn1_wo, 0, 0, tm=TM_FFN_PROJ,
        ctx=ctx.reshape(b * lctx, d), ctx_row=ctx_row, proj=(norm_g[0, 1], w_main, w_zg, w2, b2, cos, sin))

    o_a = _attention(sink[0], z, b, t, lctx)
    st_f = _gla_ctx_state(z, la, b, t, lctx, False)
    st_b = _gla_ctx_state(z, la, b, t, lctx, True)
    o_f = _gla(z, la, st_f, b, t, False)
    o_g = _gla(z, la, st_b, b, t, True, o_f, gla_g[0])
    x = _ffn(xc, n_tok, t, mods, norm_g[0, 2], ffn2_wi, ffn2_wo, 0, 6, mix=(o_a, o_g, w_out[0].astype(BF16)))

    x = _ffn(x, n_tok, t, mods, norm_g[1, 0], ffn1_wi, ffn1_wo, 1, 0)
    x = _pool(x.reshape(b, t, d), mods, 1, norm_g[1, 1], w_pool[0].astype(BF16), pool_scale[0])
    x = _ffn(x.reshape(n_tok, d), n_tok, t, mods, norm_g[1, 2], ffn2_wi, ffn2_wo, 1, 6, final_g=final_g)
    return x.reshape(b, t, d)
```

```python
import functools

import jax
import jax.numpy as jnp
import numpy as np
from jax import lax
from jax.experimental import pallas as pl
from jax.experimental.pallas import tpu as pltpu

F32 = jnp.float32
BF16 = jnp.bfloat16

D_MODEL = 1024
N_MOD = 9
D_FF = 2816
RMS_EPS = 1e-6
GRID_W = 64
A_HEADS = 8
A_KV_HEADS = 2
A_HEAD_DIM = 64
WINDOW = 128
ROPE_BASE = 10000.0
B_HEADS = 4
B_DK = 64
B_DV = 128
B_GATE_RANK = 16
B_GATE_NORM = 16.0
B_CHUNK = 64
POOL_WINDOWS = (2, 4, 8, 16)
POOL_GROUP = D_MODEL // len(POOL_WINDOWS)
A_Q = A_HEADS * A_HEAD_DIM
A_KV = A_KV_HEADS * A_HEAD_DIM
B_QK = B_HEADS * B_DK
B_V = B_HEADS * B_DV
PROJ_MAIN = A_Q + 2 * A_KV + 2 * B_QK + 2 * B_V
KV_DUP = 4 * A_KV

LOG2E = 1.4426950408889634
LANES = 128
COND_ROWS = 8
VMEM_LIMIT = 56 * 1024 * 1024

TM_FFN = 1024
SUB_FFN = 256
TM_FFN_PROJ = 512
TM_POOL = 1024
TB_GLA = 2048
GLA_GROUP = 4
HALO = 8


def _cparams(*sem):
    return pltpu.CompilerParams(dimension_semantics=sem, vmem_limit_bytes=VMEM_LIMIT)


def _const_spec(shape):
    nd = len(shape)
    return pl.BlockSpec(shape, lambda *_: (0,) * nd, pipeline_mode=pl.Buffered(1))


def _modulate(x, g, shift, scale):
    y = x * lax.rsqrt(jnp.mean(x * x, axis=-1, keepdims=True) + RMS_EPS)
    return (y * g) * (1.0 + scale) + shift


def _silu(a):
    return a * jax.nn.sigmoid(a)


ADALN_MODS_PER_STEP = 3


def _adaln_kernel(c_ref, w_ref, b_ref, o_ref):
    d = c_ref.shape[1]
    s = _silu(c_ref[...]).astype(BF16)
    r = jnp.dot(s, w_ref[...].astype(BF16), preferred_element_type=F32) + b_ref[...]
    for k in range(ADALN_MODS_PER_STEP):
        o_ref[k] = r[:, k * d:(k + 1) * d]


def _adaln(cond, w_mod, b_mod):
    depth, d, n = w_mod.shape
    per = ADALN_MODS_PER_STEP
    tn = per * d
    return pl.pallas_call(
        _adaln_kernel,
        grid=(depth, N_MOD // per),
        in_specs=[
            pl.BlockSpec((COND_ROWS, d), lambda l, j: (0, 0)),
            pl.BlockSpec((None, d, tn), lambda l, j: (l, 0, j)),
            pl.BlockSpec((None, 1, tn), lambda l, j: (l, 0, j)),
        ],
        out_specs=pl.BlockSpec((None, per, COND_ROWS, d), lambda l, j: (l, j, 0, 0)),
        out_shape=jax.ShapeDtypeStruct((depth, N_MOD, COND_ROWS, d), F32),
        compiler_params=_cparams("parallel", "parallel"),
        name="adaln",
    )(cond, w_mod, b_mod.reshape(depth, 1, n))


def _mod_spec(layer, d):
    return pl.BlockSpec((None, N_MOD, COND_ROWS, d), lambda *_: (layer, 0, 0, 0))


def _mod(mod_ref, idx, row):
    return mod_ref[idx, pl.ds(row, 1), :]


FFN_WCHUNK = 256
FFN_NCH = D_FF // FFN_WCHUNK
FFN_CHUNKS_PER_STEP = 2
FFN_NW = 2 * FFN_NCH // FFN_CHUNKS_PER_STEP
FFN_WO_ROWS = D_FF // FFN_NW
FFN_MIX_LEAD = 5


def _ffn_tile(x_ref, o_ref, mod_ref, row, g_ref, wi_s, wo_s, mi, mix, fg_ref, proj):
    sub = min(SUB_FFN, x_ref.shape[0])
    nsub = x_ref.shape[0] // sub

    def residual_in(s):
        rows = slice(s * sub, (s + 1) * sub)
        x = x_ref[rows, :]
        if mix is not None:
            oa_ref, ob_ref, w_ref = mix
            y0 = (jnp.dot(oa_ref[rows, :], w_ref[:A_Q, :], preferred_element_type=F32)
                  + jnp.dot(ob_ref[rows, :], w_ref[A_Q:, :], preferred_element_type=F32))
            x = x + _mod(mod_ref, 5, row) * y0
        return x

    def normed(x):
        return _modulate(x, g_ref[...], _mod(mod_ref, mi, row), _mod(mod_ref, mi + 1, row)).astype(BF16)

    def act_chunk(h, c):
        a = jnp.dot(h, wi_s[c], preferred_element_type=F32)
        u = jnp.dot(h, wi_s[FFN_NCH + c], preferred_element_type=F32)
        return (_silu(a) * u).astype(BF16)

    x_next = residual_in(0)
    h_next = normed(x_next)
    early = []
    pending = []
    for s in range(nsub):
        rows = slice(s * sub, (s + 1) * sub)
        x, h = x_next, h_next
        acts, early = early, []
        for c in range(len(acts), FFN_NCH):
            if s + 1 < nsub:
                if c == FFN_NCH - FFN_MIX_LEAD:
                    x_next = residual_in(s + 1)
                if c == FFN_NCH - 1:
                    h_next = normed(x_next)
                    early = [act_chunk(h_next, 0)]
            acts.append(act_chunk(h, c))
        y = jnp.dot(jnp.concatenate(acts, axis=1), wo_s[...], preferred_element_type=F32)
        o = x + (0.5 * _mod(mod_ref, mi + 2, row)) * y
        if fg_ref is not None:
            o = (o * lax.rsqrt(jnp.mean(o * o, axis=-1, keepdims=True) + RMS_EPS)) * fg_ref[...]
        o_ref[rows, :] = o
        if proj is not None:
            pending.append(
                (_modulate(o, proj[0][...], _mod(mod_ref, 3, row), _mod(mod_ref, 4, row)).astype(BF16), rows))
    for h2, rows in pending:
        for stage in proj[1]:
            stage(h2, rows)


def _ffn_kernel(*refs, mi, nx, tiles_per_batch, ctx_row, has_ctx, has_mix, final, has_proj):
    refs = list(refs)
    x_ref = refs.pop(0)
    c_ref = refs.pop(0) if has_ctx else None
    mod_ref, g_ref, wi_ref, wo_ref = refs[:4]
    refs = refs[4:]
    mix = tuple(refs[:3]) if has_mix else None
    refs = refs[3:] if has_mix else refs
    fg_ref = refs.pop(0) if final else None
    proj = None
    if has_proj:
        g1_ref, wp_ref, wz_ref, w2_ref, b2_ref, cos_ref, sin_ref = refs[:7]
        proj = (g1_ref, _proj_stage_fns(wp_ref, wz_ref, w2_ref, b2_ref, cos_ref, sin_ref, refs[8:8 + N_PROJ_OUT]))
        refs = refs[7:8] + refs[8 + N_PROJ_OUT:]
    o_ref, wi_s, wo_s = refs
    j = pl.program_id(0)

    @pl.when(j < FFN_NW)
    def _():
        for k in range(FFN_CHUNKS_PER_STEP):
            wi_s[j * FFN_CHUNKS_PER_STEP + k] = wi_ref[:, k * FFN_WCHUNK:(k + 1) * FFN_WCHUNK].astype(BF16)
        wo_s[pl.ds(pl.multiple_of(j * FFN_WO_ROWS, FFN_WO_ROWS), FFN_WO_ROWS), :] = wo_ref[...].astype(BF16)

    @pl.when((j >= FFN_NW) & (j < FFN_NW + nx))
    def _():
        _ffn_tile(x_ref, o_ref, mod_ref, (j - FFN_NW) // tiles_per_batch, g_ref, wi_s, wo_s, mi, mix, fg_ref, proj)

    if has_ctx:
        @pl.when(j >= FFN_NW + nx)
        def _():
            _ffn_tile(c_ref, o_ref, mod_ref, ctx_row, g_ref, wi_s, wo_s, mi, None, fg_ref, proj)


def _ffn(x, n_rows, tok_per_batch, mods, g, wi, wo, layer, mi, *, tm=TM_FFN, ctx=None, ctx_row=None, mix=None,
         final_g=None, proj=None):
    d = x.shape[1]
    nx = n_rows // tm
    nc = 0 if ctx is None else ctx.shape[0] // tm
    tiles_per_batch = tok_per_batch // tm

    def tile(j):
        return jnp.clip(j - FFN_NW, 0, nx + nc - 1)

    def x_tile(j):
        return jnp.minimum(tile(j), nx - 1)

    def wchunk(j):
        return jnp.minimum(j, FFN_NW - 1)

    def pos_blk(j):
        return jnp.where(tile(j) < nx, tile(j) % tiles_per_batch, tiles_per_batch)

    in_specs = [pl.BlockSpec((tm, d), lambda j: (x_tile(j), 0))]
    args = [x]
    if ctx is not None:
        in_specs.append(pl.BlockSpec((tm, d), lambda j: (jnp.clip(j - FFN_NW - nx, 0, nc - 1), 0)))
        args.append(ctx)
    in_specs += [
        _mod_spec(layer, d),
        _const_spec((1, d)),
        pl.BlockSpec((None, d, FFN_WCHUNK * FFN_CHUNKS_PER_STEP), lambda j: (layer, 0, wchunk(j))),
        pl.BlockSpec((None, FFN_WO_ROWS, d), lambda j: (layer, wchunk(j), 0)),
    ]
    args += [mods, g.reshape(1, d), wi, wo]
    if mix is not None:
        oa, ob, w_out = mix
        in_specs += [pl.BlockSpec((tm, A_Q), lambda j: (x_tile(j), 0)),
                     pl.BlockSpec((tm, B_V), lambda j: (x_tile(j), 0)),
                     _const_spec(w_out.shape)]
        args += [oa, ob, w_out]
    if final_g is not None:
        in_specs.append(_const_spec((1, d)))
        args.append(final_g.reshape(1, d))
    n_out_rows = (nx + nc) * tm
    out_specs = [pl.BlockSpec((tm, d), lambda j: (tile(j), 0))]
    out_shape = [jax.ShapeDtypeStruct((n_out_rows, d), F32)]
    if proj is not None:
        g1, w_main, w_zg, w2, b2, cos, sin = proj
        in_specs += [_const_spec((1, d)), _const_spec(w_main.shape), _const_spec(w_zg.shape), _const_spec(w2.shape),
                     _const_spec(b2.shape),
                     pl.BlockSpec((tm, LANES), lambda j: (pos_blk(j), 0)),
                     pl.BlockSpec((tm, LANES), lambda j: (pos_blk(j), 0))]
        args += [g1.reshape(1, d), w_main, w_zg, w2, b2, cos, sin]
        out_specs += [pl.BlockSpec((tm, w), lambda j: (tile(j), 0)) for w, _ in PROJ_OUTS]
        out_shape += [jax.ShapeDtypeStruct((n_out_rows, w), dt) for w, dt in PROJ_OUTS]
    res = pl.pallas_call(
        functools.partial(_ffn_kernel, mi=mi, nx=nx, tiles_per_batch=tiles_per_batch, ctx_row=ctx_row,
                          has_ctx=ctx is not None, has_mix=mix is not None,
                          final=final_g is not None, has_proj=proj is not None),
        grid=(FFN_NW + nx + nc,),
        in_specs=in_specs,
        out_specs=out_specs,
        out_shape=out_shape,
        scratch_shapes=[pltpu.VMEM((2 * FFN_NCH, d, FFN_WCHUNK), BF16), pltpu.VMEM((D_FF, d), BF16)],
        compiler_params=_cparams("arbitrary"),
        name="ffn_proj" if proj is not None else "ffn",
    )(*args)
    return res if proj is not None else res[0]


def _swap16(x):
    lane = lax.broadcasted_iota(jnp.int32, x.shape, 1)
    return jnp.where((lane & 16) == 0, pltpu.roll(x, LANES - 16, 1), pltpu.roll(x, 16, 1))


def _dup_halves(x):
    lane = lax.broadcasted_iota(jnp.int32, x.shape, 1)
    r = pltpu.roll(x, LANES // 2, 1)
    lo = lane < LANES // 2
    return jnp.where(lo, x, r), jnp.where(lo, r, x)


def _halves_with_ones(x):
    lane = lax.broadcasted_iota(jnp.int32, x.shape, 1)
    r = pltpu.roll(x, LANES // 2, 1)
    lo = lane < LANES // 2
    return jnp.where(lo, x, 1.0), jnp.where(lo, r, 1.0)


Z_QA = (A_Q, 0)
Z_KV = (KV_DUP, 1)
Z_VB = (B_V, 2)
Z_RB = (B_V, 3)
Z_QB = (B_QK, 8)
Z_KB = (B_QK, 9)
Z_WIDTH = (Z_KB[1] + 1) * B_QK
PROJ_OUTS = ((Z_WIDTH, BF16), (2 * B_QK, F32))
N_PROJ_OUT = len(PROJ_OUTS)


def _z_cols(field):
    w, blk = field
    return slice(w * blk, w * (blk + 1))


def _proj_stage_fns(w_ref, wz_ref, w2_ref, b2_ref, cos_ref, sin_ref, outs):
    z_ref, la_ref = outs
    qa0, kv0 = _z_cols(Z_QA).start, _z_cols(Z_KV).start
    scale = A_HEAD_DIM ** -0.5 * LOG2E
    o_b = A_Q + 2 * A_KV

    def cols(h, lo, hi):
        return jnp.dot(h, w_ref[:, lo:hi], preferred_element_type=F32)

    def rope(v, rows):
        return v * cos_ref[rows, :] + _swap16(v) * sin_ref[rows, :]

    def attn_q(h, rows):
        z = cols(h, 0, A_Q)
        for p in range(A_Q // LANES):
            q = z[:, p * LANES:(p + 1) * LANES]
            z_ref[rows, qa0 + p * LANES:qa0 + (p + 1) * LANES] = (rope(q, rows) * scale).astype(BF16)

    def attn_kv(h, rows):
        z = cols(h, A_Q, A_Q + 2 * A_KV)
        k0, k1 = _dup_halves(rope(z[:, :A_KV], rows))
        v0, v1 = _halves_with_ones(z[:, A_KV:])
        for j, piece in enumerate((k0, k1, v0, v1)):
            z_ref[rows, kv0 + j * LANES:kv0 + (j + 1) * LANES] = piece.astype(BF16)

    def gla_qk(h, rows):
        z = cols(h, o_b, o_b + 2 * B_QK)
        z_ref[rows, _z_cols(Z_QB)] = z[:, :B_QK].astype(BF16)
        z_ref[rows, _z_cols(Z_KB)] = z[:, B_QK:].astype(BF16)

    def gla_v(h, rows):
        z_ref[rows, _z_cols(Z_VB)] = cols(h, o_b + 2 * B_QK, o_b + 2 * B_QK + B_V).astype(BF16)

    def gla_r(h, rows):
        z_ref[rows, _z_cols(Z_RB)] = cols(h, o_b + 2 * B_QK + B_V, o_b + 2 * B_QK + 2 * B_V).astype(BF16)

    def gla_gate(h, rows):
        zg = jnp.dot(h, wz_ref[...], preferred_element_type=F32).astype(BF16)
        pre = jnp.dot(zg, w2_ref[...], preferred_element_type=F32) + b2_ref[...]
        la_ref[rows, :] = (jnp.minimum(pre, 0.0) - jnp.log1p(jnp.exp(-jnp.abs(pre)))) / B_GATE_NORM

    return (gla_gate, attn_q, attn_kv, gla_qk, gla_v, gla_r)


ATTN_QBLOCKS = 16


def _attn_kernel(sink_ref, q_ref, kvp_ref, kvc_ref, kvn_ref, ckv_ref, o_ref, *, nsteps):
    n = pl.program_id(1)
    blk = WINDOW
    row = lax.broadcasted_iota(jnp.int32, (blk, blk), 0)
    col = lax.broadcasted_iota(jnp.int32, (blk, blk), 1)
    ninf = jnp.float32(-jnp.inf)
    lane = lax.broadcasted_iota(jnp.int32, (blk, LANES), 1)
    lo = lane < LANES // 2
    rep = A_HEADS // A_KV_HEADS

    def key_block(i, lanes):
        if i == 0:
            return kvp_ref[:, lanes]
        if i == ATTN_QBLOCKS + 1:
            return kvn_ref[:, lanes]
        return kvc_ref[(i - 1) * blk:i * blk, lanes]

    def band(qb, lanes):
        return jnp.concatenate([key_block(qb + j, lanes) for j in range(3)] + [ckv_ref[:, lanes]], axis=0)

    kd = [[band(qb, slice(g * LANES, (g + 1) * LANES)) for g in range(A_KV_HEADS)] for qb in range(ATTN_QBLOCKS)]
    vd = [[band(qb, slice((A_KV_HEADS + g) * LANES, (A_KV_HEADS + g + 1) * LANES)) for g in range(A_KV_HEADS)]
          for qb in range(ATTN_QBLOCKS)]
    nkb = kd[0][0].shape[0] // blk
    bias_prev = [jnp.where((col >= row) & ((n > 0) if qb == 0 else True), 0.0, ninf) for qb in range(ATTN_QBLOCKS)]
    bias_next = [jnp.where((col <= row) & ((n < nsteps - 1) if qb == ATTN_QBLOCKS - 1 else True), 0.0, ninf)
                 for qb in range(ATTN_QBLOCKS)]

    def scores(qb, p):
        pair = q_ref[qb * blk:(qb + 1) * blk, p * LANES:(p + 1) * LANES]
        zero = jnp.zeros_like(pair)
        lhs = jnp.concatenate([jnp.where(lo, pair, zero), jnp.where(lo, zero, pair)], axis=0)
        return lax.dot_general(lhs, kd[qb][(2 * p) // rep], (((1,), (1,)), ((), ())), preferred_element_type=F32)

    def softmax(qb, p, s):
        ps, sinks = [], []
        for r in range(2):
            sk = sink_ref[2 * p + r] * LOG2E
            cb = [s[r * blk:(r + 1) * blk, j * blk:(j + 1) * blk] for j in range(nkb)]
            cb[0] = cb[0] + bias_prev[qb]
            cb[2] = cb[2] + bias_next[qb]
            mx = cb[0]
            for c in cb[1:]:
                mx = jnp.maximum(mx, c)
            m = jnp.maximum(jnp.max(mx, axis=-1, keepdims=True), sk)
            ps.append(jnp.concatenate([jnp.exp2((c - m).astype(BF16)) for c in cb], axis=1))
            sinks.append(jnp.exp2(sk - m))
        return jnp.concatenate(ps, axis=0), sinks

    def weighted_values(qb, p, probs, sinks):
        o = jnp.dot(probs, vd[qb][(2 * p) // rep], preferred_element_type=F32)
        o_even, o_odd = o[:blk, :], o[blk:, :]
        num = jnp.where(lo, o_even, pltpu.roll(o_odd, LANES // 2, 1))
        den = jnp.where(lo, pltpu.roll(o_even, LANES // 2, 1) + sinks[0], o_odd + sinks[1])
        o_ref[qb * blk:(qb + 1) * blk, p * LANES:(p + 1) * LANES] = (num / den).astype(BF16)

    units = [(qb, p) for qb in range(ATTN_QBLOCKS) for p in range(A_HEADS // 2)]
    s_next = scores(*units[0])
    for i, unit in enumerate(units):
        s_cur = s_next
        if i + 1 < len(units):
            s_next = scores(*units[i + 1])
        probs, sinks = softmax(*unit, s_cur)
        weighted_values(*unit, probs, sinks)


def _attention(sink, z, b, t, lctx):
    nb = t // WINDOW
    tq = ATTN_QBLOCKS * WINDOW
    nsteps = t // tq
    ctx0 = (b * t) // lctx

    def edge(off):
        return pl.BlockSpec((WINDOW, KV_DUP),
                            lambda bi, n: (bi * nb + jnp.clip(n * ATTN_QBLOCKS + off, 0, nb - 1), Z_KV[1]))

    return pl.pallas_call(
        functools.partial(_attn_kernel, nsteps=nsteps),
        grid=(b, nsteps),
        in_specs=[
            pl.BlockSpec(memory_space=pltpu.SMEM),
            pl.BlockSpec((tq, A_Q), lambda bi, n: (bi * nsteps + n, Z_QA[1])),
            edge(-1),
            pl.BlockSpec((tq, KV_DUP), lambda bi, n: (bi * nsteps + n, Z_KV[1])),
            edge(ATTN_QBLOCKS),
            pl.BlockSpec((lctx, KV_DUP), lambda bi, n: (ctx0 + bi, Z_KV[1])),
        ],
        out_specs=pl.BlockSpec((tq, A_Q), lambda bi, n: (bi * nsteps + n, 0)),
        out_shape=jax.ShapeDtypeStruct((b * t, A_Q), BF16),
        compiler_params=_cparams("parallel", "parallel"),
        name="window_attn",
    )(sink, z, z, z, z, z)


B_PAIRS = B_HEADS // 2
PAIR_QK = 2 * B_DK
PAIR_V = 2 * B_DV


class _GlaMasks:
    def __init__(self, reverse):
        c = B_CHUNK
        i = lax.broadcasted_iota(jnp.int32, (c, c), 0)
        j = lax.broadcasted_iota(jnp.int32, (c, c), 1)
        self.tri = jnp.where((j >= i) if reverse else (j <= i), 1.0, 0.0).astype(BF16)
        lane_qk = lax.broadcasted_iota(jnp.int32, (c, B_QK), 1)
        self.k_head = [(lane_qk >= h * B_DK) & (lane_qk < (h + 1) * B_DK) for h in range(B_HEADS)]
        lane_v = lax.broadcasted_iota(jnp.int32, (c, B_V), 1)
        self.v_head = [(lane_v >= h * B_DV) & (lane_v < (h + 1) * B_DV) for h in range(B_HEADS)]
        ai = lax.broadcasted_iota(jnp.int32, (c, B_HEADS * c), 0)
        aj = lax.broadcasted_iota(jnp.int32, (c, B_HEADS * c), 1) % c
        self.causal = (aj >= ai) if reverse else (aj <= ai)
        sr = lax.broadcasted_iota(jnp.int32, (PAIR_V, PAIR_QK), 0) // B_DV
        sl = lax.broadcasted_iota(jnp.int32, (PAIR_V, PAIR_QK), 1) // B_DK
        self.state_diag = sr == sl


def _gla_block(q_ref, k_ref, v_ref, la_ref, st, mk, reverse, want_out):
    c = B_CHUNK
    nc = k_ref.shape[0] // c
    order = list(reversed(range(nc))) if reverse else list(range(nc))
    rows = {ci: slice(ci * c, (ci + 1) * c) for ci in order}
    tb = (((1,), (1,)), ((), ()))
    ta = (((0,), (0,)), ((), ()))

    g = {}
    for ci in order:
        la = la_ref[rows[ci], :]
        hi = la.astype(BF16)
        lo = (la - hi.astype(F32)).astype(BF16)
        g[ci] = jnp.dot(mk.tri, hi, preferred_element_type=F32) + jnp.dot(mk.tri, lo, preferred_element_type=F32)

    o = {}
    for g0 in range(0, nc, GLA_GROUP):
        group = order[g0:g0 + GLA_GROUP]
        dec, k_end, q_t, kstack = {}, {}, {}, {}
        for ci in group:
            k = k_ref[rows[ci], :].astype(F32)
            g_last = g[ci][0:1, :] if reverse else g[ci][c - 1:c, :]
            dec[ci] = jnp.exp(g_last)
            k_end[ci] = (k * jnp.exp(g_last - g[ci])).astype(BF16)
            if want_out:
                q_t[ci] = ((q_ref[rows[ci], :].astype(F32) * (B_DK ** -0.5)) * jnp.exp(g[ci])).astype(BF16)
                k_t = k * jnp.exp(-g[ci])
                kstack[ci] = jnp.concatenate([jnp.where(m, k_t, 0.0) for m in mk.k_head], axis=0).astype(BF16)

        ds = {}
        for ci in group:
            v = v_ref[rows[ci], :]
            ds[ci] = [jnp.where(mk.state_diag,
                                lax.dot_general(v[:, p * PAIR_V:(p + 1) * PAIR_V],
                                                k_end[ci][:, p * PAIR_QK:(p + 1) * PAIR_QK], ta,
                                                preferred_element_type=F32), 0.0) for p in range(B_PAIRS)]

        st_prev = {}
        for ci in group:
            st_prev[ci] = st
            st = [st[p] * dec[ci][:, p * PAIR_QK:(p + 1) * PAIR_QK] + ds[ci][p] for p in range(B_PAIRS)]
        if not want_out:
            continue

        att = {}
        for ci in group:
            a = lax.dot_general(q_t[ci], kstack[ci], tb, preferred_element_type=F32)
            att[ci] = jnp.where(mk.causal, a, 0.0).astype(BF16)
        for ci in group:
            v = v_ref[rows[ci], :]
            vstack = jnp.concatenate([jnp.where(m, v, jnp.zeros_like(v)) for m in mk.v_head], axis=0)
            o[ci] = jnp.dot(att[ci], vstack, preferred_element_type=F32) + jnp.concatenate(
                [lax.dot_general(q_t[ci][:, p * PAIR_QK:(p + 1) * PAIR_QK], st_prev[ci][p].astype(BF16), tb,
                                 preferred_element_type=F32) for p in range(B_PAIRS)], axis=1)
    if not want_out:
        return None, st
    return o, st


def _gla_ctx_kernel(k_ref, v_ref, la_ref, st_ref, *, reverse):
    mk = _GlaMasks(reverse)
    st = [jnp.zeros((PAIR_V, PAIR_QK), F32) for _ in range(B_PAIRS)]
    _, st = _gla_block(None, k_ref, v_ref, la_ref, st, mk, reverse, False)
    for p in range(B_PAIRS):
        st_ref[p] = st[p]


def _gla_ctx_state(z, la, b, t, l, reverse):
    d = 1 if reverse else 0
    ctx0 = (b * t) // l
    return pl.pallas_call(
        functools.partial(_gla_ctx_kernel, reverse=reverse),
        grid=(b,),
        in_specs=[
            pl.BlockSpec((l, B_QK), lambda bi: (ctx0 + bi, Z_KB[1])),
            pl.BlockSpec((l, B_V), lambda bi: (ctx0 + bi, Z_VB[1])),
            pl.BlockSpec((l, B_QK), lambda bi: (ctx0 + bi, d)),
        ],
        out_specs=pl.BlockSpec((None, B_PAIRS, PAIR_V, PAIR_QK), lambda bi: (bi, 0, 0, 0)),
        out_shape=jax.ShapeDtypeStruct((b, B_PAIRS, PAIR_V, PAIR_QK), F32),
        compiler_params=_cparams("parallel"),
        name="gla_ctx_state",
    )(z, z, la)


def _gla_kernel(q_ref, k_ref, v_ref, la_ref, st0_ref, *rest, reverse):
    if reverse:
        of_ref, r_ref, gg_ref, o_ref, st_ref = rest
    else:
        o_ref, st_ref = rest

    @pl.when(pl.program_id(1) == 0)
    def _():
        st_ref[...] = st0_ref[...]

    mk = _GlaMasks(reverse)
    outs_by_chunk, st = _gla_block(q_ref, k_ref, v_ref, la_ref, [st_ref[p] for p in range(B_PAIRS)],
                                   mk, reverse, True)
    for ci, o in outs_by_chunk.items():
        rows = slice(ci * B_CHUNK, (ci + 1) * B_CHUNK)
        if reverse:
            o = o + of_ref[rows, :]
            outs = []
            for h in range(B_HEADS):
                oh = o[:, h * B_DV:(h + 1) * B_DV]
                oh = (oh * lax.rsqrt(jnp.mean(oh * oh, axis=-1, keepdims=True) + RMS_EPS)) * gg_ref[...]
                outs.append(oh * _silu(r_ref[rows, h * B_DV:(h + 1) * B_DV].astype(F32)))
            o_ref[rows, :] = jnp.concatenate(outs, axis=1).astype(BF16)
        else:
            o_ref[rows, :] = o
    for p in range(B_PAIRS):
        st_ref[p] = st[p]


def _gla(z, la, st0, b, t, reverse, o_fwd=None, gla_g=None):
    tb = TB_GLA
    nt = t // tb
    d = 1 if reverse else 0

    def tok(w, lane_blk=0):
        if reverse:
            return pl.BlockSpec((tb, w), lambda bi, j: (bi * nt + nt - 1 - j, lane_blk))
        return pl.BlockSpec((tb, w), lambda bi, j: (bi * nt + j, lane_blk))

    in_specs = [tok(*Z_QB), tok(*Z_KB), tok(*Z_VB), tok(B_QK, d),
                pl.BlockSpec((None, B_PAIRS, PAIR_V, PAIR_QK), lambda bi, j: (bi, 0, 0, 0))]
    args = [z, z, z, la, st0]
    if reverse:
        in_specs += [tok(B_V), tok(*Z_RB), _const_spec((1, B_DV))]
        args += [o_fwd, z, gla_g.reshape(1, B_DV)]
    return pl.pallas_call(
        functools.partial(_gla_kernel, reverse=reverse),
        grid=(b, nt),
        in_specs=in_specs,
        out_specs=tok(B_V),
        out_shape=jax.ShapeDtypeStruct((b * t, B_V), BF16 if reverse else F32),
        scratch_shapes=[pltpu.VMEM((B_PAIRS, PAIR_V, PAIR_QK), F32)],
        compiler_params=_cparams("parallel", "arbitrary"),
        name="gla_bwd" if reverse else "gla_fwd",
    )(*args)


def _shift_rows(e, d):
    n = e.shape[0]
    return pltpu.roll(e, (-d) % n, 0)


def _pool_kernel(x_ref, xp_ref, xn_ref, mod_ref, g_ref, w_ref, ps_ref, o_ref, *, seq):
    i = pl.program_id(1)
    tm = x_ref.shape[0]
    g = g_ref[...]
    row = pl.program_id(0)
    shift = _mod(mod_ref, 3, row)
    scale = _mod(mod_ref, 4, row)
    x = x_ref[...]
    h = _modulate(x, g, shift, scale)
    hp = jnp.where(i > 0, _modulate(xp_ref[...], g, shift, scale), 0.0)
    hn = jnp.where(i < pl.num_programs(1) - 1, _modulate(xn_ref[...], g, shift, scale), 0.0)
    groups, totals = [], []
    for gi, w in enumerate(POOL_WINDOWS):
        ls = slice(gi * POOL_GROUP, (gi + 1) * POOL_GROUP)
        groups.append(h[:, ls])
        s = jnp.concatenate([hp[:, ls], h[:, ls], hn[:, ls]], axis=0)
        k = 1
        while k < w:
            s = s + _shift_rows(s, -k)
            k *= 2
        if w // 2 > 1:
            s = _shift_rows(s, w // 2 - 1)
        totals.append(s[HALO:HALO + tm, :])

    def finish(scales):
        ys = [jnp.dot((totals[gi] * scales[gi] - groups[gi]).astype(BF16), w_ref[gi], preferred_element_type=F32)
              for gi in range(len(POOL_WINDOWS))]
        y = jnp.concatenate(ys, axis=1) * ps_ref[...]
        o_ref[...] = x + _mod(mod_ref, 5, row) * y

    clipped = (i == 0) | (i == pl.num_programs(1) - 1)

    @pl.when(clipped)
    def _():
        t = i * tm + lax.broadcasted_iota(jnp.int32, (tm, 1), 0)
        finish([1.0 / (jnp.minimum(t + w // 2, seq) - jnp.maximum(t - w // 2, 0)).astype(F32)
                for w in POOL_WINDOWS])

    @pl.when(jnp.logical_not(clipped))
    def _():
        finish([1.0 / w for w in POOL_WINDOWS])


def _pool(x, mods, layer, g, w_pool, pool_scale):
    b, t, d = x.shape
    tm = TM_POOL
    hb = tm // HALO
    last = t // HALO - 1
    return pl.pallas_call(
        functools.partial(_pool_kernel, seq=t),
        grid=(b, t // tm),
        in_specs=[
            pl.BlockSpec((None, tm, d), lambda bi, i: (bi, i, 0)),
            pl.BlockSpec((None, HALO, d), lambda bi, i: (bi, jnp.maximum(i * hb - 1, 0), 0)),
            pl.BlockSpec((None, HALO, d), lambda bi, i: (bi, jnp.minimum((i + 1) * hb, last), 0)),
            _mod_spec(layer, d),
            _const_spec((1, d)),
            _const_spec(w_pool.shape),
            _const_spec((1, d)),
        ],
        out_specs=pl.BlockSpec((None, tm, d), lambda bi, i: (bi, i, 0)),
        out_shape=jax.ShapeDtypeStruct((b, t, d), F32),
        compiler_params=_cparams("parallel", "parallel"),
        name="pool_mixer",
    )(x, x, x, mods, g.reshape(1, d), w_pool, pool_scale.reshape(1, d))


def _rope_tables(t, pad_rows):
    n = A_HEAD_DIM // 4
    freqs = ROPE_BASE ** (-jnp.arange(n, dtype=F32) / n)
    pos = jnp.arange(t)
    rows = (pos // GRID_W).astype(F32)
    cols = (pos % GRID_W).astype(F32)
    ang_r = rows[:, None] * freqs
    ang_c = cols[:, None] * freqs
    cos = jnp.concatenate([jnp.cos(ang_r), jnp.cos(ang_r), jnp.cos(ang_c), jnp.cos(ang_c)], axis=1)
    sin = jnp.concatenate([-jnp.sin(ang_r), jnp.sin(ang_r), -jnp.sin(ang_c), jnp.sin(ang_c)], axis=1)
    cos = jnp.concatenate([jnp.tile(cos, (1, 2)), jnp.ones((pad_rows, LANES), F32)], axis=0)
    sin = jnp.concatenate([jnp.tile(sin, (1, 2)), jnp.zeros((pad_rows, LANES), F32)], axis=0)
    return cos, sin


def kernel(x, c, ctx, c_ctx, w_mod, b_mod, norm_g, ffn1_wi, ffn1_wo, ffn2_wi, ffn2_wo,
           w_in, w_a2_f, b_a_f, w_a2_b, b_a_b, sink, gla_g, w_out, w_pool, pool_scale, final_g):
    b, t, d = x.shape
    lctx = ctx.shape[1]
    n_tok = b * t
    ctx_row = b

    cond = jnp.zeros((COND_ROWS, d), F32).at[:b].set(c).at[ctx_row].set(c_ctx)
    mods = _adaln(cond, w_mod, b_mod)
    x = x.reshape(n_tok, d)

    w_main = w_in[0][:, :PROJ_MAIN].astype(BF16)
    w_zg = jnp.zeros((d, LANES), F32).at[:, :2 * B_GATE_RANK].set(w_in[0][:, PROJ_MAIN:]).astype(BF16)
    w2 = jnp.zeros((LANES, 2 * B_QK), F32)
    w2 = w2.at[:B_GATE_RANK, :B_QK].set(w_a2_f[0]).at[B_GATE_RANK:2 * B_GATE_RANK, B_QK:].set(w_a2_b[0]).astype(BF16)
    b2 = jnp.concatenate([b_a_f[0], b_a_b[0]]).reshape(1, 2 * B_QK)
    cos, sin = _rope_tables(t, TM_FFN_PROJ)
    xc, z, la = _ffn(
        x, n_tok, t, mods, norm_g[0, 0], ffn1_wi, ffn1_wo, 0, 0, tm=TM_FFN_PROJ,
        ctx=ctx.reshape(b * lctx, d), ctx_row=ctx_row, proj=(norm_g[0, 1], w_main, w_zg, w2, b2, cos, sin))

    o_a = _attention(sink[0], z, b, t, lctx)
    st_f = _gla_ctx_state(z, la, b, t, lctx, False)
    st_b = _gla_ctx_state(z, la, b, t, lctx, True)
    o_f = _gla(z, la, st_f, b, t, False)
    o_g = _gla(z, la, st_b, b, t, True, o_f, gla_g[0])
    x = _ffn(xc, n_tok, t, mods, norm_g[0, 2], ffn2_wi, ffn2_wo, 0, 6, mix=(o_a, o_g, w_out[0].astype(BF16)))

    x = _ffn(x, n_tok, t, mods, norm_g[1, 0], ffn1_wi, ffn1_wo, 1, 0)
    x = _pool(x.reshape(b, t, d), mods, 1, norm_g[1, 1], w_pool[0].astype(BF16), pool_scale[0])
    x = _ffn(x.reshape(n_tok, d), n_tok, t, mods, norm_g[1, 2], ffn2_wi, ffn2_wo, 1, 6, final_g=final_g)
    return x.reshape(b, t, d)
```

```python
import functools

import jax
import jax.numpy as jnp
import numpy as np
from jax import lax
from jax.experimental import pallas as pl
from jax.experimental.pallas import tpu as pltpu

F32 = jnp.float32
BF16 = jnp.bfloat16

D_MODEL = 1024
N_MOD = 9
D_FF = 2816
RMS_EPS = 1e-6
GRID_W = 64
A_HEADS = 8
A_KV_HEADS = 2
A_HEAD_DIM = 64
WINDOW = 128
ROPE_BASE = 10000.0
B_HEADS = 4
B_DK = 64
B_DV = 128
B_GATE_RANK = 16
B_GATE_NORM = 16.0
B_CHUNK = 64
POOL_WINDOWS = (2, 4, 8, 16)
POOL_GROUP = D_MODEL // len(POOL_WINDOWS)
A_Q = A_HEADS * A_HEAD_DIM
A_KV = A_KV_HEADS * A_HEAD_DIM
B_QK = B_HEADS * B_DK
B_V = B_HEADS * B_DV
PROJ_MAIN = A_Q + 2 * A_KV + 2 * B_QK + 2 * B_V
KV_DUP = 4 * A_KV

LOG2E = 1.4426950408889634
LANES = 128
COND_ROWS = 8
VMEM_LIMIT = 56 * 1024 * 1024

TM_FFN = 1024
SUB_FFN = 256
TM_FFN_PROJ = 512
TM_POOL = 1024
TB_GLA = 1024
GLA_GROUP = 4
HALO = 8


def _cparams(*sem):
    return pltpu.CompilerParams(dimension_semantics=sem, vmem_limit_bytes=VMEM_LIMIT)


def _const_spec(shape):
    nd = len(shape)
    return pl.BlockSpec(shape, lambda *_: (0,) * nd, pipeline_mode=pl.Buffered(1))


def _modulate(x, g, shift, scale):
    y = x * lax.rsqrt(jnp.mean(x * x, axis=-1, keepdims=True) + RMS_EPS)
    return (y * g) * (1.0 + scale) + shift


def _silu(a):
    return a * jax.nn.sigmoid(a)


ADALN_MODS_PER_STEP = 3


def _adaln_kernel(c_ref, w_ref, b_ref, o_ref):
    d = c_ref.shape[1]
    s = _silu(c_ref[...]).astype(BF16)
    r = jnp.dot(s, w_ref[...].astype(BF16), preferred_element_type=F32) + b_ref[...]
    for k in range(ADALN_MODS_PER_STEP):
        o_ref[k] = r[:, k * d:(k + 1) * d]


def _adaln(cond, w_mod, b_mod):
    depth, d, n = w_mod.shape
    per = ADALN_MODS_PER_STEP
    tn = per * d
    return pl.pallas_call(
        _adaln_kernel,
        grid=(depth, N_MOD // per),
        in_specs=[
            pl.BlockSpec((COND_ROWS, d), lambda l, j: (0, 0)),
            pl.BlockSpec((None, d, tn), lambda l, j: (l, 0, j)),
            pl.BlockSpec((None, 1, tn), lambda l, j: (l, 0, j)),
        ],
        out_specs=pl.BlockSpec((None, per, COND_ROWS, d), lambda l, j: (l, j, 0, 0)),
        out_shape=jax.ShapeDtypeStruct((depth, N_MOD, COND_ROWS, d), F32),
        compiler_params=_cparams("parallel", "parallel"),
        name="adaln",
    )(cond, w_mod, b_mod.reshape(depth, 1, n))


def _mod_spec(layer, d):
    return pl.BlockSpec((None, N_MOD, COND_ROWS, d), lambda *_: (layer, 0, 0, 0))


def _mod(mod_ref, idx, row):
    return mod_ref[idx, pl.ds(row, 1), :]


FFN_WCHUNK = 256
FFN_NCH = D_FF // FFN_WCHUNK
FFN_CHUNKS_PER_STEP = 2
FFN_NW = 2 * FFN_NCH // FFN_CHUNKS_PER_STEP
FFN_WO_ROWS = D_FF // FFN_NW
FFN_MIX_LEAD = 5


def _ffn_tile(x_ref, o_ref, mod_ref, row, g_ref, wi_s, wo_s, mi, mix, fg_ref, proj):
    sub = min(SUB_FFN, x_ref.shape[0])
    nsub = x_ref.shape[0] // sub

    def residual_in(s):
        rows = slice(s * sub, (s + 1) * sub)
        x = x_ref[rows, :]
        if mix is not None:
            oa_ref, ob_ref, w_ref = mix
            y0 = (jnp.dot(oa_ref[rows, :], w_ref[:A_Q, :], preferred_element_type=F32)
                  + jnp.dot(ob_ref[rows, :], w_ref[A_Q:, :], preferred_element_type=F32))
            x = x + _mod(mod_ref, 5, row) * y0
        return x

    def normed(x):
        return _modulate(x, g_ref[...], _mod(mod_ref, mi, row), _mod(mod_ref, mi + 1, row)).astype(BF16)

    def act_chunk(h, c):
        a = jnp.dot(h, wi_s[c], preferred_element_type=F32)
        u = jnp.dot(h, wi_s[FFN_NCH + c], preferred_element_type=F32)
        return (_silu(a) * u).astype(BF16)

    x_next = residual_in(0)
    h_next = normed(x_next)
    early = []
    pending = []
    for s in range(nsub):
        rows = slice(s * sub, (s + 1) * sub)
        x, h = x_next, h_next
        acts, early = early, []
        for c in range(len(acts), FFN_NCH):
            if s + 1 < nsub:
                if c == FFN_NCH - FFN_MIX_LEAD:
                    x_next = residual_in(s + 1)
                if c == FFN_NCH - 1:
                    h_next = normed(x_next)
                    early = [act_chunk(h_next, 0)]
            acts.append(act_chunk(h, c))
        y = jnp.dot(jnp.concatenate(acts, axis=1), wo_s[...], preferred_element_type=F32)
        o = x + (0.5 * _mod(mod_ref, mi + 2, row)) * y
        if fg_ref is not None:
            o = (o * lax.rsqrt(jnp.mean(o * o, axis=-1, keepdims=True) + RMS_EPS)) * fg_ref[...]
        o_ref[rows, :] = o
        if proj is not None:
            pending.append(
                (_modulate(o, proj[0][...], _mod(mod_ref, 3, row), _mod(mod_ref, 4, row)).astype(BF16), rows))
    for h2, rows in pending:
        for stage in proj[1]:
            stage(h2, rows)


def _ffn_kernel(*refs, mi, nx, tiles_per_batch, ctx_row, has_ctx, has_mix, final, has_proj):
    refs = list(refs)
    x_ref = refs.pop(0)
    c_ref = refs.pop(0) if has_ctx else None
    mod_ref, g_ref, wi_ref, wo_ref = refs[:4]
    refs = refs[4:]
    mix = tuple(refs[:3]) if has_mix else None
    refs = refs[3:] if has_mix else refs
    fg_ref = refs.pop(0) if final else None
    proj = None
    if has_proj:
        g1_ref, wp_ref, wz_ref, w2_ref, b2_ref, cos_ref, sin_ref = refs[:7]
        proj = (g1_ref, _proj_stage_fns(wp_ref, wz_ref, w2_ref, b2_ref, cos_ref, sin_ref, refs[8:8 + N_PROJ_OUT]))
        refs = refs[7:8] + refs[8 + N_PROJ_OUT:]
    o_ref, wi_s, wo_s = refs
    j = pl.program_id(0)

    @pl.when(j < FFN_NW)
    def _():
        for k in range(FFN_CHUNKS_PER_STEP):
            wi_s[j * FFN_CHUNKS_PER_STEP + k] = wi_ref[:, k * FFN_WCHUNK:(k + 1) * FFN_WCHUNK].astype(BF16)
        wo_s[pl.ds(pl.multiple_of(j * FFN_WO_ROWS, FFN_WO_ROWS), FFN_WO_ROWS), :] = wo_ref[...].astype(BF16)

    @pl.when((j >= FFN_NW) & (j < FFN_NW + nx))
    def _():
        _ffn_tile(x_ref, o_ref, mod_ref, (j - FFN_NW) // tiles_per_batch, g_ref, wi_s, wo_s, mi, mix, fg_ref, proj)

    if has_ctx:
        @pl.when(j >= FFN_NW + nx)
        def _():
            _ffn_tile(c_ref, o_ref, mod_ref, ctx_row, g_ref, wi_s, wo_s, mi, None, fg_ref, proj)


def _ffn(x, n_rows, tok_per_batch, mods, g, wi, wo, layer, mi, *, tm=TM_FFN, ctx=None, ctx_row=None, mix=None,
         final_g=None, proj=None):
    d = x.shape[1]
    nx = n_rows // tm
    nc = 0 if ctx is None else ctx.shape[0] // tm
    tiles_per_batch = tok_per_batch // tm

    def tile(j):
        return jnp.clip(j - FFN_NW, 0, nx + nc - 1)

    def x_tile(j):
        return jnp.minimum(tile(j), nx - 1)

    def wchunk(j):
        return jnp.minimum(j, FFN_NW - 1)

    def pos_blk(j):
        return jnp.where(tile(j) < nx, tile(j) % tiles_per_batch, tiles_per_batch)

    in_specs = [pl.BlockSpec((tm, d), lambda j: (x_tile(j), 0))]
    args = [x]
    if ctx is not None:
        in_specs.append(pl.BlockSpec((tm, d), lambda j: (jnp.clip(j - FFN_NW - nx, 0, nc - 1), 0)))
        args.append(ctx)
    in_specs += [
        _mod_spec(layer, d),
        _const_spec((1, d)),
        pl.BlockSpec((None, d, FFN_WCHUNK * FFN_CHUNKS_PER_STEP), lambda j: (layer, 0, wchunk(j))),
        pl.BlockSpec((None, FFN_WO_ROWS, d), lambda j: (layer, wchunk(j), 0)),
    ]
    args += [mods, g.reshape(1, d), wi, wo]
    if mix is not None:
        oa, ob, w_out = mix
        in_specs += [pl.BlockSpec((tm, A_Q), lambda j: (x_tile(j), 0)),
                     pl.BlockSpec((tm, B_V), lambda j: (x_tile(j), 0)),
                     _const_spec(w_out.shape)]
        args += [oa, ob, w_out]
    if final_g is not None:
        in_specs.append(_const_spec((1, d)))
        args.append(final_g.reshape(1, d))
    n_out_rows = (nx + nc) * tm
    out_specs = [pl.BlockSpec((tm, d), lambda j: (tile(j), 0))]
    out_shape = [jax.ShapeDtypeStruct((n_out_rows, d), F32)]
    if proj is not None:
        g1, w_main, w_zg, w2, b2, cos, sin = proj
        in_specs += [_const_spec((1, d)), _const_spec(w_main.shape), _const_spec(w_zg.shape), _const_spec(w2.shape),
                     _const_spec(b2.shape),
                     pl.BlockSpec((tm, LANES), lambda j: (pos_blk(j), 0)),
                     pl.BlockSpec((tm, LANES), lambda j: (pos_blk(j), 0))]
        args += [g1.reshape(1, d), w_main, w_zg, w2, b2, cos, sin]
        out_specs += [pl.BlockSpec((tm, w), lambda j: (tile(j), 0)) for w, _ in PROJ_OUTS]
        out_shape += [jax.ShapeDtypeStruct((n_out_rows, w), dt) for w, dt in PROJ_OUTS]
    res = pl.pallas_call(
        functools.partial(_ffn_kernel, mi=mi, nx=nx, tiles_per_batch=tiles_per_batch, ctx_row=ctx_row,
                          has_ctx=ctx is not None, has_mix=mix is not None,
                          final=final_g is not None, has_proj=proj is not None),
        grid=(FFN_NW + nx + nc,),
        in_specs=in_specs,
        out_specs=out_specs,
        out_shape=out_shape,
        scratch_shapes=[pltpu.VMEM((2 * FFN_NCH, d, FFN_WCHUNK), BF16), pltpu.VMEM((D_FF, d), BF16)],
        compiler_params=_cparams("arbitrary"),
        name="ffn_proj" if proj is not None else "ffn",
    )(*args)
    return res if proj is not None else res[0]


def _swap16(x):
    lane = lax.broadcasted_iota(jnp.int32, x.shape, 1)
    return jnp.where((lane & 16) == 0, pltpu.roll(x, LANES - 16, 1), pltpu.roll(x, 16, 1))


def _dup_halves(x):
    lane = lax.broadcasted_iota(jnp.int32, x.shape, 1)
    r = pltpu.roll(x, LANES // 2, 1)
    lo = lane < LANES // 2
    return jnp.where(lo, x, r), jnp.where(lo, r, x)


def _halves_with_ones(x):
    lane = lax.broadcasted_iota(jnp.int32, x.shape, 1)
    r = pltpu.roll(x, LANES // 2, 1)
    lo = lane < LANES // 2
    return jnp.where(lo, x, 1.0), jnp.where(lo, r, 1.0)


Z_QA = (A_Q, 0)
Z_KV = (KV_DUP, 1)
Z_VB = (B_V, 2)
Z_RB = (B_V, 3)
Z_QB = (B_QK, 8)
Z_KB = (B_QK, 9)
Z_WIDTH = (Z_KB[1] + 1) * B_QK
PROJ_OUTS = ((Z_WIDTH, BF16), (2 * B_QK, F32))
N_PROJ_OUT = len(PROJ_OUTS)


def _z_cols(field):
    w, blk = field
    return slice(w * blk, w * (blk + 1))


def _proj_stage_fns(w_ref, wz_ref, w2_ref, b2_ref, cos_ref, sin_ref, outs):
    z_ref, la_ref = outs
    qa0, kv0 = _z_cols(Z_QA).start, _z_cols(Z_KV).start
    scale = A_HEAD_DIM ** -0.5 * LOG2E
    o_b = A_Q + 2 * A_KV

    def cols(h, lo, hi):
        return jnp.dot(h, w_ref[:, lo:hi], preferred_element_type=F32)

    def rope(v, rows):
        return v * cos_ref[rows, :] + _swap16(v) * sin_ref[rows, :]

    def attn_q(h, rows):
        z = cols(h, 0, A_Q)
        for p in range(A_Q // LANES):
            q = z[:, p * LANES:(p + 1) * LANES]
            z_ref[rows, qa0 + p * LANES:qa0 + (p + 1) * LANES] = (rope(q, rows) * scale).astype(BF16)

    def attn_kv(h, rows):
        z = cols(h, A_Q, A_Q + 2 * A_KV)
        k0, k1 = _dup_halves(rope(z[:, :A_KV], rows))
        v0, v1 = _halves_with_ones(z[:, A_KV:])
        for j, piece in enumerate((k0, k1, v0, v1)):
            z_ref[rows, kv0 + j * LANES:kv0 + (j + 1) * LANES] = piece.astype(BF16)

    def gla_qk(h, rows):
        z = cols(h, o_b, o_b + 2 * B_QK)
        z_ref[rows, _z_cols(Z_QB)] = z[:, :B_QK].astype(BF16)
        z_ref[rows, _z_cols(Z_KB)] = z[:, B_QK:].astype(BF16)

    def gla_v(h, rows):
        z_ref[rows, _z_cols(Z_VB)] = cols(h, o_b + 2 * B_QK, o_b + 2 * B_QK + B_V).astype(BF16)

    def gla_r(h, rows):
        z_ref[rows, _z_cols(Z_RB)] = cols(h, o_b + 2 * B_QK + B_V, o_b + 2 * B_QK + 2 * B_V).astype(BF16)

    def gla_gate(h, rows):
        zg = jnp.dot(h, wz_ref[...], preferred_element_type=F32).astype(BF16)
        pre = jnp.dot(zg, w2_ref[...], preferred_element_type=F32) + b2_ref[...]
        la_ref[rows, :] = (jnp.minimum(pre, 0.0) - jnp.log1p(jnp.exp(-jnp.abs(pre)))) / B_GATE_NORM

    return (gla_gate, attn_q, attn_kv, gla_qk, gla_v, gla_r)


ATTN_QBLOCKS = 8


def _attn_kernel(sink_ref, q_ref, kvp_ref, kvc_ref, kvn_ref, ckv_ref, o_ref, *, nsteps):
    n = pl.program_id(1)
    blk = WINDOW
    row = lax.broadcasted_iota(jnp.int32, (blk, blk), 0)
    col = lax.broadcasted_iota(jnp.int32, (blk, blk), 1)
    ninf = jnp.float32(-jnp.inf)
    lane = lax.broadcasted_iota(jnp.int32, (blk, LANES), 1)
    lo = lane < LANES // 2
    rep = A_HEADS // A_KV_HEADS

    def key_block(i, lanes):
        if i == 0:
            return kvp_ref[:, lanes]
        if i == ATTN_QBLOCKS + 1:
            return kvn_ref[:, lanes]
        return kvc_ref[(i - 1) * blk:i * blk, lanes]

    def band(qb, lanes):
        return jnp.concatenate([key_block(qb + j, lanes) for j in range(3)] + [ckv_ref[:, lanes]], axis=0)

    kd = [[band(qb, slice(g * LANES, (g + 1) * LANES)) for g in range(A_KV_HEADS)] for qb in range(ATTN_QBLOCKS)]
    vd = [[band(qb, slice((A_KV_HEADS + g) * LANES, (A_KV_HEADS + g + 1) * LANES)) for g in range(A_KV_HEADS)]
          for qb in range(ATTN_QBLOCKS)]
    nkb = kd[0][0].shape[0] // blk
    bias_prev = [jnp.where((col >= row) & ((n > 0) if qb == 0 else True), 0.0, ninf) for qb in range(ATTN_QBLOCKS)]
    bias_next = [jnp.where((col <= row) & ((n < nsteps - 1) if qb == ATTN_QBLOCKS - 1 else True), 0.0, ninf)
                 for qb in range(ATTN_QBLOCKS)]

    def scores(qb, p):
        pair = q_ref[qb * blk:(qb + 1) * blk, p * LANES:(p + 1) * LANES]
        zero = jnp.zeros_like(pair)
        lhs = jnp.concatenate([jnp.where(lo, pair, zero), jnp.where(lo, zero, pair)], axis=0)
        return lax.dot_general(lhs, kd[qb][(2 * p) // rep], (((1,), (1,)), ((), ())), preferred_element_type=F32)

    def softmax(qb, p, s):
        ps, sinks = [], []
        for r in range(2):
            sk = sink_ref[2 * p + r] * LOG2E
            cb = [s[r * blk:(r + 1) * blk, j * blk:(j + 1) * blk] for j in range(nkb)]
            cb[0] = cb[0] + bias_prev[qb]
            cb[2] = cb[2] + bias_next[qb]
            mx = cb[0]
            for c in cb[1:]:
                mx = jnp.maximum(mx, c)
            m = jnp.maximum(jnp.max(mx, axis=-1, keepdims=True), sk)
            ps.append(jnp.concatenate([jnp.exp2((c - m).astype(BF16)) for c in cb], axis=1))
            sinks.append(jnp.exp2(sk - m))
        return jnp.concatenate(ps, axis=0), sinks

    def weighted_values(qb, p, probs, sinks):
        o = jnp.dot(probs, vd[qb][(2 * p) // rep], preferred_element_type=F32)
        o_even, o_odd = o[:blk, :], o[blk:, :]
        num = jnp.where(lo, o_even, pltpu.roll(o_odd, LANES // 2, 1))
        den = jnp.where(lo, pltpu.roll(o_even, LANES // 2, 1) + sinks[0], o_odd + sinks[1])
        o_ref[qb * blk:(qb + 1) * blk, p * LANES:(p + 1) * LANES] = (num / den).astype(BF16)

    units = [(qb, p) for qb in range(ATTN_QBLOCKS) for p in range(A_HEADS // 2)]
    s_next = scores(*units[0])
    for i, unit in enumerate(units):
        s_cur = s_next
        if i + 1 < len(units):
            s_next = scores(*units[i + 1])
        probs, sinks = softmax(*unit, s_cur)
        weighted_values(*unit, probs, sinks)


def _attention(sink, z, b, t, lctx):
    nb = t // WINDOW
    tq = ATTN_QBLOCKS * WINDOW
    nsteps = t // tq
    ctx0 = (b * t) // lctx

    def edge(off):
        return pl.BlockSpec((WINDOW, KV_DUP),
                            lambda bi, n: (bi * nb + jnp.clip(n * ATTN_QBLOCKS + off, 0, nb - 1), Z_KV[1]))

    return pl.pallas_call(
        functools.partial(_attn_kernel, nsteps=nsteps),
        grid=(b, nsteps),
        in_specs=[
            pl.BlockSpec(memory_space=pltpu.SMEM),
            pl.BlockSpec((tq, A_Q), lambda bi, n: (bi * nsteps + n, Z_QA[1])),
            edge(-1),
            pl.BlockSpec((tq, KV_DUP), lambda bi, n: (bi * nsteps + n, Z_KV[1])),
            edge(ATTN_QBLOCKS),
            pl.BlockSpec((lctx, KV_DUP), lambda bi, n: (ctx0 + bi, Z_KV[1])),
        ],
        out_specs=pl.BlockSpec((tq, A_Q), lambda bi, n: (bi * nsteps + n, 0)),
        out_shape=jax.ShapeDtypeStruct((b * t, A_Q), BF16),
        compiler_params=_cparams("parallel", "parallel"),
        name="window_attn",
    )(sink, z, z, z, z, z)


B_PAIRS = B_HEADS // 2
PAIR_QK = 2 * B_DK
PAIR_V = 2 * B_DV


class _GlaMasks:
    def __init__(self, reverse):
        c = B_CHUNK
        i = lax.broadcasted_iota(jnp.int32, (c, c), 0)
        j = lax.broadcasted_iota(jnp.int32, (c, c), 1)
        self.tri = jnp.where((j >= i) if reverse else (j <= i), 1.0, 0.0).astype(BF16)
        lane_qk = lax.broadcasted_iota(jnp.int32, (c, B_QK), 1)
        self.k_head = [(lane_qk >= h * B_DK) & (lane_qk < (h + 1) * B_DK) for h in range(B_HEADS)]
        lane_v = lax.broadcasted_iota(jnp.int32, (c, B_V), 1)
        self.v_head = [(lane_v >= h * B_DV) & (lane_v < (h + 1) * B_DV) for h in range(B_HEADS)]
        ai = lax.broadcasted_iota(jnp.int32, (c, B_HEADS * c), 0)
        aj = lax.broadcasted_iota(jnp.int32, (c, B_HEADS * c), 1) % c
        self.causal = (aj >= ai) if reverse else (aj <= ai)
        sr = lax.broadcasted_iota(jnp.int32, (PAIR_V, PAIR_QK), 0) // B_DV
        sl = lax.broadcasted_iota(jnp.int32, (PAIR_V, PAIR_QK), 1) // B_DK
        self.state_diag = sr == sl


def _gla_block(q_ref, k_ref, v_ref, la_ref, st, mk, reverse, want_out):
    c = B_CHUNK
    nc = k_ref.shape[0] // c
    order = list(reversed(range(nc))) if reverse else list(range(nc))
    rows = {ci: slice(ci * c, (ci + 1) * c) for ci in order}
    tb = (((1,), (1,)), ((), ()))
    ta = (((0,), (0,)), ((), ()))

    g = {}
    for ci in order:
        la = la_ref[rows[ci], :]
        hi = la.astype(BF16)
        lo = (la - hi.astype(F32)).astype(BF16)
        g[ci] = jnp.dot(mk.tri, hi, preferred_element_type=F32) + jnp.dot(mk.tri, lo, preferred_element_type=F32)

    o = {}
    for g0 in range(0, nc, GLA_GROUP):
        group = order[g0:g0 + GLA_GROUP]
        dec, k_end, q_t, kstack = {}, {}, {}, {}
        for ci in group:
            k = k_ref[rows[ci], :].astype(F32)
            g_last = g[ci][0:1, :] if reverse else g[ci][c - 1:c, :]
            dec[ci] = jnp.exp(g_last)
            k_end[ci] = (k * jnp.exp(g_last - g[ci])).astype(BF16)
            if want_out:
                q_t[ci] = ((q_ref[rows[ci], :].astype(F32) * (B_DK ** -0.5)) * jnp.exp(g[ci])).astype(BF16)
                k_t = k * jnp.exp(-g[ci])
                kstack[ci] = jnp.concatenate([jnp.where(m, k_t, 0.0) for m in mk.k_head], axis=0).astype(BF16)

        ds = {}
        for ci in group:
            v = v_ref[rows[ci], :]
            ds[ci] = [jnp.where(mk.state_diag,
                                lax.dot_general(v[:, p * PAIR_V:(p + 1) * PAIR_V],
                                                k_end[ci][:, p * PAIR_QK:(p + 1) * PAIR_QK], ta,
                                                preferred_element_type=F32), 0.0) for p in range(B_PAIRS)]

        st_prev = {}
        for ci in group:
            st_prev[ci] = st
            st = [st[p] * dec[ci][:, p * PAIR_QK:(p + 1) * PAIR_QK] + ds[ci][p] for p in range(B_PAIRS)]
        if not want_out:
            continue

        att = {}
        for ci in group:
            a = lax.dot_general(q_t[ci], kstack[ci], tb, preferred_element_type=F32)
            att[ci] = jnp.where(mk.causal, a, 0.0).astype(BF16)
        for ci in group:
            v = v_ref[rows[ci], :]
            vstack = jnp.concatenate([jnp.where(m, v, jnp.zeros_like(v)) for m in mk.v_head], axis=0)
            o[ci] = jnp.dot(att[ci], vstack, preferred_element_type=F32) + jnp.concatenate(
                [lax.dot_general(q_t[ci][:, p * PAIR_QK:(p + 1) * PAIR_QK], st_prev[ci][p].astype(BF16), tb,
                                 preferred_element_type=F32) for p in range(B_PAIRS)], axis=1)
    if not want_out:
        return None, st
    return o, st


def _gla_ctx_kernel(k_ref, v_ref, la_ref, st_ref, *, reverse):
    mk = _GlaMasks(reverse)
    st = [jnp.zeros((PAIR_V, PAIR_QK), F32) for _ in range(B_PAIRS)]
    _, st = _gla_block(None, k_ref, v_ref, la_ref, st, mk, reverse, False)
    for p in range(B_PAIRS):
        st_ref[p] = st[p]


def _gla_ctx_state(z, la, b, t, l, reverse):
    d = 1 if reverse else 0
    ctx0 = (b * t) // l
    return pl.pallas_call(
        functools.partial(_gla_ctx_kernel, reverse=reverse),
        grid=(b,),
        in_specs=[
            pl.BlockSpec((l, B_QK), lambda bi: (ctx0 + bi, Z_KB[1])),
            pl.BlockSpec((l, B_V), lambda bi: (ctx0 + bi, Z_VB[1])),
            pl.BlockSpec((l, B_QK), lambda bi: (ctx0 + bi, d)),
        ],
        out_specs=pl.BlockSpec((None, B_PAIRS, PAIR_V, PAIR_QK), lambda bi: (bi, 0, 0, 0)),
        out_shape=jax.ShapeDtypeStruct((b, B_PAIRS, PAIR_V, PAIR_QK), F32),
        compiler_params=_cparams("parallel"),
        name="gla_ctx_state",
    )(z, z, la)


def _gla_kernel(q_ref, k_ref, v_ref, la_ref, st0_ref, *rest, reverse):
    if reverse:
        of_ref, r_ref, gg_ref, o_ref, st_ref = rest
    else:
        o_ref, st_ref = rest

    @pl.when(pl.program_id(1) == 0)
    def _():
        st_ref[...] = st0_ref[...]

    mk = _GlaMasks(reverse)
    outs_by_chunk, st = _gla_block(q_ref, k_ref, v_ref, la_ref, [st_ref[p] for p in range(B_PAIRS)],
                                   mk, reverse, True)
    for ci, o in outs_by_chunk.items():
        rows = slice(ci * B_CHUNK, (ci + 1) * B_CHUNK)
        if reverse:
            o = o + of_ref[rows, :]
            outs = []
            for h in range(B_HEADS):
                oh = o[:, h * B_DV:(h + 1) * B_DV]
                oh = (oh * lax.rsqrt(jnp.mean(oh * oh, axis=-1, keepdims=True) + RMS_EPS)) * gg_ref[...]
                outs.append(oh * _silu(r_ref[rows, h * B_DV:(h + 1) * B_DV].astype(F32)))
            o_ref[rows, :] = jnp.concatenate(outs, axis=1).astype(BF16)
        else:
            o_ref[rows, :] = o
    for p in range(B_PAIRS):
        st_ref[p] = st[p]


def _gla(z, la, st0, b, t, reverse, o_fwd=None, gla_g=None):
    tb = TB_GLA
    nt = t // tb
    d = 1 if reverse else 0

    def tok(w, lane_blk=0):
        if reverse:
            return pl.BlockSpec((tb, w), lambda bi, j: (bi * nt + nt - 1 - j, lane_blk))
        return pl.BlockSpec((tb, w), lambda bi, j: (bi * nt + j, lane_blk))

    in_specs = [tok(*Z_QB), tok(*Z_KB), tok(*Z_VB), tok(B_QK, d),
                pl.BlockSpec((None, B_PAIRS, PAIR_V, PAIR_QK), lambda bi, j: (bi, 0, 0, 0))]
    args = [z, z, z, la, st0]
    if reverse:
        in_specs += [tok(B_V), tok(*Z_RB), _const_spec((1, B_DV))]
        args += [o_fwd, z, gla_g.reshape(1, B_DV)]
    return pl.pallas_call(
        functools.partial(_gla_kernel, reverse=reverse),
        grid=(b, nt),
        in_specs=in_specs,
        out_specs=tok(B_V),
        out_shape=jax.ShapeDtypeStruct((b * t, B_V), BF16 if reverse else F32),
        scratch_shapes=[pltpu.VMEM((B_PAIRS, PAIR_V, PAIR_QK), F32)],
        compiler_params=_cparams("parallel", "arbitrary"),
        name="gla_bwd" if reverse else "gla_fwd",
    )(*args)


def _shift_rows(e, d):
    n = e.shape[0]
    return pltpu.roll(e, (-d) % n, 0)


def _pool_kernel(x_ref, xp_ref, xn_ref, mod_ref, g_ref, w_ref, ps_ref, o_ref, *, seq):
    i = pl.program_id(1)
    tm = x_ref.shape[0]
    g = g_ref[...]
    row = pl.program_id(0)
    shift = _mod(mod_ref, 3, row)
    gain = g * (1.0 + _mod(mod_ref, 4, row))
    out_gain = _mod(mod_ref, 5, row) * ps_ref[...]

    def normed(v):
        return (v * lax.rsqrt(jnp.mean(v * v, axis=-1, keepdims=True) + RMS_EPS)) * gain + shift

    x = x_ref[...]
    h = normed(x)
    hp = jnp.where(i > 0, normed(xp_ref[...]), 0.0)
    hn = jnp.where(i < pl.num_programs(1) - 1, normed(xn_ref[...]), 0.0)
    groups, totals = [], []
    for gi, w in enumerate(POOL_WINDOWS):
        ls = slice(gi * POOL_GROUP, (gi + 1) * POOL_GROUP)
        groups.append(h[:, ls])
        f = jnp.concatenate([hp[:, ls], h[:, ls], hn[:, ls]], axis=0)
        half = w // 2
        k = 1
        while k < half:
            f = f + _shift_rows(f, k)
            k *= 2
        if half == HALO:
            totals.append(f[:tm, :] + f[HALO:HALO + tm, :])
        else:
            totals.append((_shift_rows(f, -half) + f)[HALO:HALO + tm, :])

    def finish(scales):
        ys = [jnp.dot((totals[gi] * scales[gi] - groups[gi]).astype(BF16), w_ref[gi], preferred_element_type=F32)
              for gi in range(len(POOL_WINDOWS))]
        o_ref[...] = x + out_gain * jnp.concatenate(ys, axis=1)

    clipped = (i == 0) | (i == pl.num_programs(1) - 1)

    @pl.when(clipped)
    def _():
        t = i * tm + lax.broadcasted_iota(jnp.int32, (tm, 1), 0)
        finish([1.0 / (jnp.minimum(t + w // 2, seq) - jnp.maximum(t - w // 2, 0)).astype(F32)
                for w in POOL_WINDOWS])

    @pl.when(jnp.logical_not(clipped))
    def _():
        finish([1.0 / w for w in POOL_WINDOWS])


def _pool(x, mods, layer, g, w_pool, pool_scale):
    b, t, d = x.shape
    tm = TM_POOL
    hb = tm // HALO
    last = t // HALO - 1
    return pl.pallas_call(
        functools.partial(_pool_kernel, seq=t),
        grid=(b, t // tm),
        in_specs=[
            pl.BlockSpec((None, tm, d), lambda bi, i: (bi, i, 0)),
            pl.BlockSpec((None, HALO, d), lambda bi, i: (bi, jnp.maximum(i * hb - 1, 0), 0)),
            pl.BlockSpec((None, HALO, d), lambda bi, i: (bi, jnp.minimum((i + 1) * hb, last), 0)),
            _mod_spec(layer, d),
            _const_spec((1, d)),
            _const_spec(w_pool.shape),
            _const_spec((1, d)),
        ],
        out_specs=pl.BlockSpec((None, tm, d), lambda bi, i: (bi, i, 0)),
        out_shape=jax.ShapeDtypeStruct((b, t, d), F32),
        compiler_params=_cparams("parallel", "parallel"),
        name="pool_mixer",
    )(x, x, x, mods, g.reshape(1, d), w_pool, pool_scale.reshape(1, d))


def _rope_tables(t, pad_rows):
    n = A_HEAD_DIM // 4
    freqs = (np.float32(ROPE_BASE) ** (-np.arange(n, dtype=np.float32) / np.float32(n))).astype(np.float32)
    pos = np.arange(t)
    ang_r = (pos // GRID_W).astype(np.float32)[:, None] * freqs
    ang_c = (pos % GRID_W).astype(np.float32)[:, None] * freqs
    cos = np.concatenate([np.cos(ang_r), np.cos(ang_r), np.cos(ang_c), np.cos(ang_c)], axis=1)
    sin = np.concatenate([-np.sin(ang_r), np.sin(ang_r), -np.sin(ang_c), np.sin(ang_c)], axis=1)
    cos = np.concatenate([np.tile(cos, (1, 2)), np.ones((pad_rows, LANES), np.float32)], axis=0)
    sin = np.concatenate([np.tile(sin, (1, 2)), np.zeros((pad_rows, LANES), np.float32)], axis=0)
    return jnp.asarray(cos, F32), jnp.asarray(sin, F32)


def kernel(x, c, ctx, c_ctx, w_mod, b_mod, norm_g, ffn1_wi, ffn1_wo, ffn2_wi, ffn2_wo,
           w_in, w_a2_f, b_a_f, w_a2_b, b_a_b, sink, gla_g, w_out, w_pool, pool_scale, final_g):
    b, t, d = x.shape
    lctx = ctx.shape[1]
    n_tok = b * t
    ctx_row = b

    cond = jnp.concatenate([c, c_ctx[None, :], jnp.zeros((COND_ROWS - b - 1, d), F32)], axis=0)
    mods = _adaln(cond, w_mod, b_mod)
    x = x.reshape(n_tok, d)

    w_main = w_in[0][:, :PROJ_MAIN].astype(BF16)
    w_zg = jnp.zeros((d, LANES), F32).at[:, :2 * B_GATE_RANK].set(w_in[0][:, PROJ_MAIN:]).astype(BF16)
    zr = jnp.zeros((B_GATE_RANK, B_QK), F32)
    w2 = jnp.concatenate([jnp.concatenate([w_a2_f[0], zr], axis=1), jnp.concatenate([zr, w_a2_b[0]], axis=1),
                          jnp.zeros((LANES - 2 * B_GATE_RANK, 2 * B_QK), F32)], axis=0).astype(BF16)
    b2 = jnp.concatenate([b_a_f[0], b_a_b[0]]).reshape(1, 2 * B_QK)
    cos, sin = _rope_tables(t, TM_FFN_PROJ)
    xc, z, la = _ffn(
        x, n_tok, t, mods, norm_g[0, 0], ffn1_wi, ffn1_wo, 0, 0, tm=TM_FFN_PROJ,
        ctx=ctx.reshape(b * lctx, d), ctx_row=ctx_row, proj=(norm_g[0, 1], w_main, w_zg, w2, b2, cos, sin))

    o_a = _attention(sink[0], z, b, t, lctx)
    st_f = _gla_ctx_state(z, la, b, t, lctx, False)
    st_b = _gla_ctx_state(z, la, b, t, lctx, True)
    o_f = _gla(z, la, st_f, b, t, False)
    o_g = _gla(z, la, st_b, b, t, True, o_f, gla_g[0])
    x = _ffn(xc, n_tok, t, mods, norm_g[0, 2], ffn2_wi, ffn2_wo, 0, 6, mix=(o_a, o_g, w_out[0].astype(BF16)))

    x = _ffn(x, n_tok, t, mods, norm_g[1, 0], ffn1_wi, ffn1_wo, 1, 0)
    x = _pool(x.reshape(b, t, d), mods, 1, norm_g[1, 1], w_pool[0].astype(BF16), pool_scale[0])
    x = _ffn(x.reshape(n_tok, d), n_tok, t, mods, norm_g[1, 2], ffn2_wi, ffn2_wo, 1, 6, final_g=final_g)
    return x.reshape(b, t, d)
```

```python
import functools

import jax
import jax.numpy as jnp
import numpy as np
from jax import lax
from jax.experimental import pallas as pl
from jax.experimental.pallas import tpu as pltpu

F32 = jnp.float32
BF16 = jnp.bfloat16

D_MODEL = 1024
N_MOD = 9
D_FF = 2816
RMS_EPS = 1e-6
GRID_W = 64
A_HEADS = 8
A_KV_HEADS = 2
A_HEAD_DIM = 64
WINDOW = 128
ROPE_BASE = 10000.0
B_HEADS = 4
B_DK = 64
B_DV = 128
B_GATE_RANK = 16
B_GATE_NORM = 16.0
B_CHUNK = 64
POOL_WINDOWS = (2, 4, 8, 16)
POOL_GROUP = D_MODEL // len(POOL_WINDOWS)
A_Q = A_HEADS * A_HEAD_DIM
A_KV = A_KV_HEADS * A_HEAD_DIM
B_QK = B_HEADS * B_DK
B_V = B_HEADS * B_DV
PROJ_MAIN = A_Q + 2 * A_KV + 2 * B_QK + 2 * B_V
KV_DUP = 4 * A_KV

LOG2E = 1.4426950408889634
LANES = 128
COND_ROWS = 8
VMEM_LIMIT = 56 * 1024 * 1024

TM_FFN = 1024
SUB_FFN = 256
TM_FFN_PROJ = 512
TM_POOL = 1024
TB_GLA = 1024
GLA_GROUP = 4
HALO = 8


def _cparams(*sem):
    return pltpu.CompilerParams(dimension_semantics=sem, vmem_limit_bytes=VMEM_LIMIT)


def _const_spec(shape):
    nd = len(shape)
    return pl.BlockSpec(shape, lambda *_: (0,) * nd, pipeline_mode=pl.Buffered(1))


def _modulate(x, g, shift, scale):
    y = x * lax.rsqrt(jnp.mean(x * x, axis=-1, keepdims=True) + RMS_EPS)
    return (y * g) * (1.0 + scale) + shift


def _silu(a):
    return a * jax.nn.sigmoid(a)


ADALN_MODS_PER_STEP = 3


def _adaln_kernel(c_ref, w_ref, b_ref, o_ref):
    d = c_ref.shape[1]
    s = _silu(c_ref[...]).astype(BF16)
    r = jnp.dot(s, w_ref[...].astype(BF16), preferred_element_type=F32) + b_ref[...]
    for k in range(ADALN_MODS_PER_STEP):
        o_ref[k] = r[:, k * d:(k + 1) * d]


def _adaln(cond, w_mod, b_mod):
    depth, d, n = w_mod.shape
    per = ADALN_MODS_PER_STEP
    tn = per * d
    return pl.pallas_call(
        _adaln_kernel,
        grid=(depth, N_MOD // per),
        in_specs=[
            pl.BlockSpec((COND_ROWS, d), lambda l, j: (0, 0)),
            pl.BlockSpec((None, d, tn), lambda l, j: (l, 0, j)),
            pl.BlockSpec((None, 1, tn), lambda l, j: (l, 0, j)),
        ],
        out_specs=pl.BlockSpec((None, per, COND_ROWS, d), lambda l, j: (l, j, 0, 0)),
        out_shape=jax.ShapeDtypeStruct((depth, N_MOD, COND_ROWS, d), F32),
        compiler_params=_cparams("parallel", "parallel"),
        name="adaln",
    )(cond, w_mod, b_mod.reshape(depth, 1, n))


def _mod_spec(layer, d):
    return pl.BlockSpec((None, N_MOD, COND_ROWS, d), lambda *_: (layer, 0, 0, 0))


def _mod(mod_ref, idx, row):
    return mod_ref[idx, pl.ds(row, 1), :]


FFN_WCHUNK = 256
FFN_NCH = D_FF // FFN_WCHUNK
FFN_CHUNKS_PER_STEP = 2
FFN_NW = 2 * FFN_NCH // FFN_CHUNKS_PER_STEP
FFN_WO_ROWS = D_FF // FFN_NW
FFN_MIX_LEAD = 5


def _ffn_tile(x_ref, o_ref, mod_ref, row, g_ref, wi_s, wo_s, mi, mix, fg_ref, proj):
    sub = min(SUB_FFN, x_ref.shape[0])
    nsub = x_ref.shape[0] // sub

    def residual_in(s):
        rows = slice(s * sub, (s + 1) * sub)
        x = x_ref[rows, :]
        if mix is not None:
            oa_ref, ob_ref, w_ref = mix
            y0 = (jnp.dot(oa_ref[rows, :], w_ref[:A_Q, :], preferred_element_type=F32)
                  + jnp.dot(ob_ref[rows, :], w_ref[A_Q:, :], preferred_element_type=F32))
            x = x + _mod(mod_ref, 5, row) * y0
        return x

    def normed(x):
        return _modulate(x, g_ref[...], _mod(mod_ref, mi, row), _mod(mod_ref, mi + 1, row)).astype(BF16)

    def act_chunk(h, c):
        a = jnp.dot(h, wi_s[c], preferred_element_type=F32)
        u = jnp.dot(h, wi_s[FFN_NCH + c], preferred_element_type=F32)
        return (_silu(a) * u).astype(BF16)

    x_next = residual_in(0)
    h_next = normed(x_next)
    early = []
    pending = []
    for s in range(nsub):
        rows = slice(s * sub, (s + 1) * sub)
        x, h = x_next, h_next
        acts, early = early, []
        for c in range(len(acts), FFN_NCH):
            if s + 1 < nsub:
                if c == FFN_NCH - FFN_MIX_LEAD:
                    x_next = residual_in(s + 1)
                if c == FFN_NCH - 1:
                    h_next = normed(x_next)
                    early = [act_chunk(h_next, 0)]
            acts.append(act_chunk(h, c))
        y = jnp.dot(jnp.concatenate(acts, axis=1), wo_s[...], preferred_element_type=F32)
        o = x + (0.5 * _mod(mod_ref, mi + 2, row)) * y
        if fg_ref is not None:
            o = (o * lax.rsqrt(jnp.mean(o * o, axis=-1, keepdims=True) + RMS_EPS)) * fg_ref[...]
        o_ref[rows, :] = o
        if proj is not None:
            pending.append(
                (_modulate(o, proj[0][...], _mod(mod_ref, 3, row), _mod(mod_ref, 4, row)).astype(BF16), rows))
    for h2, rows in pending:
        for stage in proj[1]:
            stage(h2, rows)


def _ffn_kernel(*refs, mi, nx, tiles_per_batch, ctx_row, has_ctx, has_mix, final, has_proj):
    refs = list(refs)
    x_ref = refs.pop(0)
    c_ref = refs.pop(0) if has_ctx else None
    mod_ref, g_ref, wi_ref, wo_ref = refs[:4]
    refs = refs[4:]
    mix = tuple(refs[:3]) if has_mix else None
    refs = refs[3:] if has_mix else refs
    fg_ref = refs.pop(0) if final else None
    proj = None
    if has_proj:
        g1_ref, wp_ref, w2_ref, b2_ref, cos_ref, sin_ref = refs[:N_PROJ_IN]
        outs = refs[N_PROJ_IN + 1:N_PROJ_IN + 1 + N_PROJ_OUT]
        proj = (g1_ref, _proj_stage_fns(wp_ref, w2_ref, b2_ref, cos_ref, sin_ref, outs))
        refs = refs[N_PROJ_IN:N_PROJ_IN + 1] + refs[N_PROJ_IN + 1 + N_PROJ_OUT:]
    o_ref, wi_s, wo_s = refs
    j = pl.program_id(0)

    @pl.when(j < FFN_NW)
    def _():
        for k in range(FFN_CHUNKS_PER_STEP):
            wi_s[j * FFN_CHUNKS_PER_STEP + k] = wi_ref[:, k * FFN_WCHUNK:(k + 1) * FFN_WCHUNK].astype(BF16)
        wo_s[pl.ds(pl.multiple_of(j * FFN_WO_ROWS, FFN_WO_ROWS), FFN_WO_ROWS), :] = wo_ref[...].astype(BF16)

    @pl.when((j >= FFN_NW) & (j < FFN_NW + nx))
    def _():
        _ffn_tile(x_ref, o_ref, mod_ref, (j - FFN_NW) // tiles_per_batch, g_ref, wi_s, wo_s, mi, mix, fg_ref, proj)

    if has_ctx:
        @pl.when(j >= FFN_NW + nx)
        def _():
            _ffn_tile(c_ref, o_ref, mod_ref, ctx_row, g_ref, wi_s, wo_s, mi, None, fg_ref, proj)


def _ffn(x, n_rows, tok_per_batch, mods, g, wi, wo, layer, mi, *, tm=TM_FFN, ctx=None, ctx_row=None, mix=None,
         final_g=None, proj=None):
    d = x.shape[1]
    nx = n_rows // tm
    nc = 0 if ctx is None else ctx.shape[0] // tm
    tiles_per_batch = tok_per_batch // tm

    def tile(j):
        return jnp.clip(j - FFN_NW, 0, nx + nc - 1)

    def x_tile(j):
        return jnp.minimum(tile(j), nx - 1)

    def wchunk(j):
        return jnp.minimum(j, FFN_NW - 1)

    def pos_blk(j):
        return jnp.where(tile(j) < nx, tile(j) % tiles_per_batch, tiles_per_batch)

    in_specs = [pl.BlockSpec((tm, d), lambda j: (x_tile(j), 0))]
    args = [x]
    if ctx is not None:
        in_specs.append(pl.BlockSpec((tm, d), lambda j: (jnp.clip(j - FFN_NW - nx, 0, nc - 1), 0)))
        args.append(ctx)
    in_specs += [
        _mod_spec(layer, d),
        _const_spec((1, d)),
        pl.BlockSpec((None, d, FFN_WCHUNK * FFN_CHUNKS_PER_STEP), lambda j: (layer, 0, wchunk(j))),
        pl.BlockSpec((None, FFN_WO_ROWS, d), lambda j: (layer, wchunk(j), 0)),
    ]
    args += [mods, g.reshape(1, d), wi, wo]
    if mix is not None:
        oa, ob, w_out = mix
        in_specs += [pl.BlockSpec((tm, A_Q), lambda j: (x_tile(j), 0)),
                     pl.BlockSpec((tm, B_V), lambda j: (x_tile(j), 0)),
                     _const_spec(w_out.shape)]
        args += [oa, ob, w_out]
    if final_g is not None:
        in_specs.append(_const_spec((1, d)))
        args.append(final_g.reshape(1, d))
    n_out_rows = (nx + nc) * tm
    out_specs = [pl.BlockSpec((tm, d), lambda j: (tile(j), 0))]
    out_shape = [jax.ShapeDtypeStruct((n_out_rows, d), F32)]
    if proj is not None:
        g1, w_in, w2, b2, cos, sin = proj
        in_specs += [_const_spec((1, d)), _const_spec(w_in.shape), _const_spec(w2.shape),
                     _const_spec(b2.shape),
                     pl.BlockSpec((tm, LANES), lambda j: (pos_blk(j), 0)),
                     pl.BlockSpec((tm, LANES), lambda j: (pos_blk(j), 0))]
        args += [g1.reshape(1, d), w_in, w2, b2, cos, sin]
        out_specs += [pl.BlockSpec((tm, w), lambda j: (tile(j), 0)) for w, _ in PROJ_OUTS]
        out_shape += [jax.ShapeDtypeStruct((n_out_rows, w), dt) for w, dt in PROJ_OUTS]
    res = pl.pallas_call(
        functools.partial(_ffn_kernel, mi=mi, nx=nx, tiles_per_batch=tiles_per_batch, ctx_row=ctx_row,
                          has_ctx=ctx is not None, has_mix=mix is not None,
                          final=final_g is not None, has_proj=proj is not None),
        grid=(FFN_NW + nx + nc,),
        in_specs=in_specs,
        out_specs=out_specs,
        out_shape=out_shape,
        scratch_shapes=[pltpu.VMEM((2 * FFN_NCH, d, FFN_WCHUNK), BF16), pltpu.VMEM((D_FF, d), BF16)],
        compiler_params=_cparams("arbitrary"),
        name="ffn_proj" if proj is not None else "ffn",
    )(*args)
    return res if proj is not None else res[0]


def _swap16(x):
    lane = lax.broadcasted_iota(jnp.int32, x.shape, 1)
    return jnp.where((lane & 16) == 0, pltpu.roll(x, LANES - 16, 1), pltpu.roll(x, 16, 1))


def _dup_halves(x):
    lane = lax.broadcasted_iota(jnp.int32, x.shape, 1)
    r = pltpu.roll(x, LANES // 2, 1)
    lo = lane < LANES // 2
    return jnp.where(lo, x, r), jnp.where(lo, r, x)


def _halves_with_ones(x):
    lane = lax.broadcasted_iota(jnp.int32, x.shape, 1)
    r = pltpu.roll(x, LANES // 2, 1)
    lo = lane < LANES // 2
    return jnp.where(lo, x, 1.0), jnp.where(lo, r, 1.0)


Z_QA = (A_Q, 0)
Z_KV = (KV_DUP, 1)
Z_VB = (B_V, 2)
Z_RB = (B_V, 3)
Z_QB = (B_QK, 8)
Z_KB = (B_QK, 9)
Z_WIDTH = (Z_KB[1] + 1) * B_QK
PROJ_OUTS = ((Z_WIDTH, BF16), (2 * B_QK, F32))
N_PROJ_OUT = len(PROJ_OUTS)
N_PROJ_IN = 6


def _z_cols(field):
    w, blk = field
    return slice(w * blk, w * (blk + 1))


def _proj_stage_fns(w_ref, w2_ref, b2_ref, cos_ref, sin_ref, outs):
    z_ref, la_ref = outs
    qa0, kv0 = _z_cols(Z_QA).start, _z_cols(Z_KV).start
    scale = A_HEAD_DIM ** -0.5 * LOG2E
    o_b = A_Q + 2 * A_KV

    def cols(h, lo, hi):
        return jnp.dot(h, w_ref[:, lo:hi], preferred_element_type=F32)

    def rope(v, rows):
        return v * cos_ref[rows, :] + _swap16(v) * sin_ref[rows, :]

    def attn_q(h, rows):
        z = cols(h, 0, A_Q)
        for p in range(A_Q // LANES):
            q = z[:, p * LANES:(p + 1) * LANES]
            z_ref[rows, qa0 + p * LANES:qa0 + (p + 1) * LANES] = (rope(q, rows) * scale).astype(BF16)

    def attn_kv(h, rows):
        z = cols(h, A_Q, A_Q + 2 * A_KV)
        k0, k1 = _dup_halves(rope(z[:, :A_KV], rows))
        v0, v1 = _halves_with_ones(z[:, A_KV:])
        for j, piece in enumerate((k0, k1, v0, v1)):
            z_ref[rows, kv0 + j * LANES:kv0 + (j + 1) * LANES] = piece.astype(BF16)

    def gla_qk(h, rows):
        z = cols(h, o_b, o_b + 2 * B_QK)
        z_ref[rows, _z_cols(Z_QB)] = z[:, :B_QK].astype(BF16)
        z_ref[rows, _z_cols(Z_KB)] = z[:, B_QK:].astype(BF16)

    def gla_v(h, rows):
        z_ref[rows, _z_cols(Z_VB)] = cols(h, o_b + 2 * B_QK, o_b + 2 * B_QK + B_V).astype(BF16)

    def gla_r(h, rows):
        z_ref[rows, _z_cols(Z_RB)] = cols(h, o_b + 2 * B_QK + B_V, o_b + 2 * B_QK + 2 * B_V).astype(BF16)

    def gla_gate(h, rows):
        zg = jnp.dot(h, w_ref[:, PROJ_MAIN:], preferred_element_type=F32).astype(BF16)
        pre = jnp.dot(zg, w2_ref[...], preferred_element_type=F32) + b2_ref[...]
        la_ref[rows, :] = (jnp.minimum(pre, 0.0) - jnp.log1p(jnp.exp(-jnp.abs(pre)))) / B_GATE_NORM

    return (gla_gate, attn_q, attn_kv, gla_qk, gla_v, gla_r)


ATTN_QBLOCKS = 8


def _attn_kernel(sink_ref, q_ref, kvp_ref, kvc_ref, kvn_ref, ckv_ref, o_ref, *, nsteps):
    n = pl.program_id(1)
    blk = WINDOW
    row = lax.broadcasted_iota(jnp.int32, (blk, blk), 0)
    col = lax.broadcasted_iota(jnp.int32, (blk, blk), 1)
    ninf = jnp.float32(-jnp.inf)
    lane = lax.broadcasted_iota(jnp.int32, (blk, LANES), 1)
    lo = lane < LANES // 2
    rep = A_HEADS // A_KV_HEADS

    def key_block(i, lanes):
        if i == 0:
            return kvp_ref[:, lanes]
        if i == ATTN_QBLOCKS + 1:
            return kvn_ref[:, lanes]
        return kvc_ref[(i - 1) * blk:i * blk, lanes]

    def band(qb, lanes):
        return jnp.concatenate([key_block(qb + j, lanes) for j in range(3)] + [ckv_ref[:, lanes]], axis=0)

    kd = [[band(qb, slice(g * LANES, (g + 1) * LANES)) for g in range(A_KV_HEADS)] for qb in range(ATTN_QBLOCKS)]
    vd = [[band(qb, slice((A_KV_HEADS + g) * LANES, (A_KV_HEADS + g + 1) * LANES)) for g in range(A_KV_HEADS)]
          for qb in range(ATTN_QBLOCKS)]
    nkb = kd[0][0].shape[0] // blk
    bias_prev = [jnp.where((col >= row) & ((n > 0) if qb == 0 else True), 0.0, ninf) for qb in range(ATTN_QBLOCKS)]
    bias_next = [jnp.where((col <= row) & ((n < nsteps - 1) if qb == ATTN_QBLOCKS - 1 else True), 0.0, ninf)
                 for qb in range(ATTN_QBLOCKS)]

    def scores(qb, p):
        pair = q_ref[qb * blk:(qb + 1) * blk, p * LANES:(p + 1) * LANES]
        zero = jnp.zeros_like(pair)
        lhs = jnp.concatenate([jnp.where(lo, pair, zero), jnp.where(lo, zero, pair)], axis=0)
        return lax.dot_general(lhs, kd[qb][(2 * p) // rep], (((1,), (1,)), ((), ())), preferred_element_type=F32)

    def softmax(qb, p, s):
        ps, sinks = [], []
        for r in range(2):
            sk = sink_ref[2 * p + r] * LOG2E
            cb = [s[r * blk:(r + 1) * blk, j * blk:(j + 1) * blk] for j in range(nkb)]
            cb[0] = cb[0] + bias_prev[qb]
            cb[2] = cb[2] + bias_next[qb]
            mx = cb[0]
            for c in cb[1:]:
                mx = jnp.maximum(mx, c)
            m = jnp.maximum(jnp.max(mx, axis=-1, keepdims=True), sk)
            ps.append(jnp.concatenate([jnp.exp2((c - m).astype(BF16)) for c in cb], axis=1))
            sinks.append(jnp.exp2(sk - m))
        return jnp.concatenate(ps, axis=0), sinks

    def weighted_values(qb, p, probs, sinks):
        o = jnp.dot(probs, vd[qb][(2 * p) // rep], preferred_element_type=F32)
        o_even, o_odd = o[:blk, :], o[blk:, :]
        num = jnp.where(lo, o_even, pltpu.roll(o_odd, LANES // 2, 1))
        den = jnp.where(lo, pltpu.roll(o_even, LANES // 2, 1) + sinks[0], o_odd + sinks[1])
        o_ref[qb * blk:(qb + 1) * blk, p * LANES:(p + 1) * LANES] = (num / den).astype(BF16)

    units = [(qb, p) for qb in range(ATTN_QBLOCKS) for p in range(A_HEADS // 2)]
    s_next = scores(*units[0])
    for i, unit in enumerate(units):
        s_cur = s_next
        if i + 1 < len(units):
            s_next = scores(*units[i + 1])
        probs, sinks = softmax(*unit, s_cur)
        weighted_values(*unit, probs, sinks)


def _attention(sink, z, b, t, lctx):
    nb = t // WINDOW
    tq = ATTN_QBLOCKS * WINDOW
    nsteps = t // tq
    ctx0 = (b * t) // lctx

    def edge(off):
        return pl.BlockSpec((WINDOW, KV_DUP),
                            lambda bi, n: (bi * nb + jnp.clip(n * ATTN_QBLOCKS + off, 0, nb - 1), Z_KV[1]))

    return pl.pallas_call(
        functools.partial(_attn_kernel, nsteps=nsteps),
        grid=(b, nsteps),
        in_specs=[
            pl.BlockSpec(memory_space=pltpu.SMEM),
            pl.BlockSpec((tq, A_Q), lambda bi, n: (bi * nsteps + n, Z_QA[1])),
            edge(-1),
            pl.BlockSpec((tq, KV_DUP), lambda bi, n: (bi * nsteps + n, Z_KV[1])),
            edge(ATTN_QBLOCKS),
            pl.BlockSpec((lctx, KV_DUP), lambda bi, n: (ctx0 + bi, Z_KV[1])),
        ],
        out_specs=pl.BlockSpec((tq, A_Q), lambda bi, n: (bi * nsteps + n, 0)),
        out_shape=jax.ShapeDtypeStruct((b * t, A_Q), BF16),
        compiler_params=_cparams("parallel", "parallel"),
        name="window_attn",
    )(sink, z, z, z, z, z)


B_PAIRS = B_HEADS // 2
PAIR_QK = 2 * B_DK
PAIR_V = 2 * B_DV


class _GlaMasks:
    def __init__(self, reverse):
        c = B_CHUNK
        i = lax.broadcasted_iota(jnp.int32, (c, c), 0)
        j = lax.broadcasted_iota(jnp.int32, (c, c), 1)
        self.tri = jnp.where((j >= i) if reverse else (j <= i), 1.0, 0.0).astype(BF16)
        lane_qk = lax.broadcasted_iota(jnp.int32, (c, B_QK), 1)
        self.k_head = [(lane_qk >= h * B_DK) & (lane_qk < (h + 1) * B_DK) for h in range(B_HEADS)]
        lane_v = lax.broadcasted_iota(jnp.int32, (c, B_V), 1)
        self.v_head = [(lane_v >= h * B_DV) & (lane_v < (h + 1) * B_DV) for h in range(B_HEADS)]
        ai = lax.broadcasted_iota(jnp.int32, (c, B_HEADS * c), 0)
        aj = lax.broadcasted_iota(jnp.int32, (c, B_HEADS * c), 1) % c
        self.causal = (aj >= ai) if reverse else (aj <= ai)
        sr = lax.broadcasted_iota(jnp.int32, (PAIR_V, PAIR_QK), 0) // B_DV
        sl = lax.broadcasted_iota(jnp.int32, (PAIR_V, PAIR_QK), 1) // B_DK
        self.state_diag = sr == sl


def _gla_block(q_ref, k_ref, v_ref, la_ref, st, mk, reverse, want_out):
    c = B_CHUNK
    nc = k_ref.shape[0] // c
    order = list(reversed(range(nc))) if reverse else list(range(nc))
    rows = {ci: slice(ci * c, (ci + 1) * c) for ci in order}
    tb = (((1,), (1,)), ((), ()))
    ta = (((0,), (0,)), ((), ()))

    g = {}
    for ci in order:
        la = la_ref[rows[ci], :]
        hi = la.astype(BF16)
        lo = (la - hi.astype(F32)).astype(BF16)
        g[ci] = jnp.dot(mk.tri, hi, preferred_element_type=F32) + jnp.dot(mk.tri, lo, preferred_element_type=F32)

    o = {}
    for g0 in range(0, nc, GLA_GROUP):
        group = order[g0:g0 + GLA_GROUP]
        dec, k_end, q_t, kstack = {}, {}, {}, {}
        for ci in group:
            k = k_ref[rows[ci], :].astype(F32)
            g_last = g[ci][0:1, :] if reverse else g[ci][c - 1:c, :]
            dec[ci] = jnp.exp(g_last)
            k_end[ci] = (k * jnp.exp(g_last - g[ci])).astype(BF16)
            if want_out:
                q_t[ci] = ((q_ref[rows[ci], :].astype(F32) * (B_DK ** -0.5)) * jnp.exp(g[ci])).astype(BF16)
                k_t = k * jnp.exp(-g[ci])
                kstack[ci] = jnp.concatenate([jnp.where(m, k_t, 0.0) for m in mk.k_head], axis=0).astype(BF16)

        ds = {}
        for ci in group:
            v = v_ref[rows[ci], :]
            ds[ci] = [jnp.where(mk.state_diag,
                                lax.dot_general(v[:, p * PAIR_V:(p + 1) * PAIR_V],
                                                k_end[ci][:, p * PAIR_QK:(p + 1) * PAIR_QK], ta,
                                                preferred_element_type=F32), 0.0) for p in range(B_PAIRS)]

        st_prev = {}
        for ci in group:
            st_prev[ci] = st
            st = [st[p] * dec[ci][:, p * PAIR_QK:(p + 1) * PAIR_QK] + ds[ci][p] for p in range(B_PAIRS)]
        if not want_out:
            continue

        att = {}
        for ci in group:
            a = lax.dot_general(q_t[ci], kstack[ci], tb, preferred_element_type=F32)
            att[ci] = jnp.where(mk.causal, a, 0.0).astype(BF16)
        for ci in group:
            v = v_ref[rows[ci], :]
            vstack = jnp.concatenate([jnp.where(m, v, jnp.zeros_like(v)) for m in mk.v_head], axis=0)
            o[ci] = jnp.dot(att[ci], vstack, preferred_element_type=F32) + jnp.concatenate(
                [lax.dot_general(q_t[ci][:, p * PAIR_QK:(p + 1) * PAIR_QK], st_prev[ci][p].astype(BF16), tb,
                                 preferred_element_type=F32) for p in range(B_PAIRS)], axis=1)
    if not want_out:
        return None, st
    return o, st


def _gla_ctx_kernel(k_ref, v_ref, la_ref, st_ref, *, reverse):
    mk = _GlaMasks(reverse)
    st = [jnp.zeros((PAIR_V, PAIR_QK), F32) for _ in range(B_PAIRS)]
    _, st = _gla_block(None, k_ref, v_ref, la_ref, st, mk, reverse, False)
    for p in range(B_PAIRS):
        st_ref[p] = st[p]


def _gla_ctx_state(z, la, b, t, l, reverse):
    d = 1 if reverse else 0
    ctx0 = (b * t) // l
    return pl.pallas_call(
        functools.partial(_gla_ctx_kernel, reverse=reverse),
        grid=(b,),
        in_specs=[
            pl.BlockSpec((l, B_QK), lambda bi: (ctx0 + bi, Z_KB[1])),
            pl.BlockSpec((l, B_V), lambda bi: (ctx0 + bi, Z_VB[1])),
            pl.BlockSpec((l, B_QK), lambda bi: (ctx0 + bi, d)),
        ],
        out_specs=pl.BlockSpec((None, B_PAIRS, PAIR_V, PAIR_QK), lambda bi: (bi, 0, 0, 0)),
        out_shape=jax.ShapeDtypeStruct((b, B_PAIRS, PAIR_V, PAIR_QK), F32),
        compiler_params=_cparams("parallel"),
        name="gla_ctx_state",
    )(z, z, la)


def _gla_kernel(q_ref, k_ref, v_ref, la_ref, st0_ref, *rest, reverse):
    if reverse:
        of_ref, r_ref, gg_ref, o_ref, st_ref = rest
    else:
        o_ref, st_ref = rest

    @pl.when(pl.program_id(1) == 0)
    def _():
        st_ref[...] = st0_ref[...]

    mk = _GlaMasks(reverse)
    outs_by_chunk, st = _gla_block(q_ref, k_ref, v_ref, la_ref, [st_ref[p] for p in range(B_PAIRS)],
                                   mk, reverse, True)
    for ci, o in outs_by_chunk.items():
        rows = slice(ci * B_CHUNK, (ci + 1) * B_CHUNK)
        if reverse:
            o = o + of_ref[rows, :]
            outs = []
            for h in range(B_HEADS):
                oh = o[:, h * B_DV:(h + 1) * B_DV]
                oh = (oh * lax.rsqrt(jnp.mean(oh * oh, axis=-1, keepdims=True) + RMS_EPS)) * gg_ref[...]
                outs.append(oh * _silu(r_ref[rows, h * B_DV:(h + 1) * B_DV].astype(F32)))
            o_ref[rows, :] = jnp.concatenate(outs, axis=1).astype(BF16)
        else:
            o_ref[rows, :] = o
    for p in range(B_PAIRS):
        st_ref[p] = st[p]


def _gla(z, la, st0, b, t, reverse, o_fwd=None, gla_g=None):
    tb = TB_GLA
    nt = t // tb
    d = 1 if reverse else 0

    def tok(w, lane_blk=0):
        if reverse:
            return pl.BlockSpec((tb, w), lambda bi, j: (bi * nt + nt - 1 - j, lane_blk))
        return pl.BlockSpec((tb, w), lambda bi, j: (bi * nt + j, lane_blk))

    in_specs = [tok(*Z_QB), tok(*Z_KB), tok(*Z_VB), tok(B_QK, d),
                pl.BlockSpec((None, B_PAIRS, PAIR_V, PAIR_QK), lambda bi, j: (bi, 0, 0, 0))]
    args = [z, z, z, la, st0]
    if reverse:
        in_specs += [tok(B_V), tok(*Z_RB), _const_spec((1, B_DV))]
        args += [o_fwd, z, gla_g.reshape(1, B_DV)]
    return pl.pallas_call(
        functools.partial(_gla_kernel, reverse=reverse),
        grid=(b, nt),
        in_specs=in_specs,
        out_specs=tok(B_V),
        out_shape=jax.ShapeDtypeStruct((b * t, B_V), BF16 if reverse else F32),
        scratch_shapes=[pltpu.VMEM((B_PAIRS, PAIR_V, PAIR_QK), F32)],
        compiler_params=_cparams("parallel", "arbitrary"),
        name="gla_bwd" if reverse else "gla_fwd",
    )(*args)


def _shift_rows(e, d):
    n = e.shape[0]
    return pltpu.roll(e, (-d) % n, 0)


def _pool_kernel(x_ref, xp_ref, xn_ref, mod_ref, g_ref, w_ref, ps_ref, o_ref, *, seq):
    i = pl.program_id(1)
    tm = x_ref.shape[0]
    g = g_ref[...]
    row = pl.program_id(0)
    shift = _mod(mod_ref, 3, row)
    gain = g * (1.0 + _mod(mod_ref, 4, row))
    out_gain = _mod(mod_ref, 5, row) * ps_ref[...]

    def normed(v):
        return (v * lax.rsqrt(jnp.mean(v * v, axis=-1, keepdims=True) + RMS_EPS)) * gain + shift

    x = x_ref[...]
    h = normed(x)
    hp = jnp.where(i > 0, normed(xp_ref[...]), 0.0)
    hn = jnp.where(i < pl.num_programs(1) - 1, normed(xn_ref[...]), 0.0)
    groups, totals = [], []
    for gi, w in enumerate(POOL_WINDOWS):
        ls = slice(gi * POOL_GROUP, (gi + 1) * POOL_GROUP)
        groups.append(h[:, ls])
        f = jnp.concatenate([hp[:, ls], h[:, ls], hn[:, ls]], axis=0)
        half = w // 2
        k = 1
        while k < half:
            f = f + _shift_rows(f, k)
            k *= 2
        if half == HALO:
            totals.append(f[:tm, :] + f[HALO:HALO + tm, :])
        else:
            totals.append((_shift_rows(f, -half) + f)[HALO:HALO + tm, :])

    def finish(scales):
        ys = [jnp.dot((totals[gi] * scales[gi] - groups[gi]).astype(BF16), w_ref[gi], preferred_element_type=F32)
              for gi in range(len(POOL_WINDOWS))]
        o_ref[...] = x + out_gain * jnp.concatenate(ys, axis=1)

    clipped = (i == 0) | (i == pl.num_programs(1) - 1)

    @pl.when(clipped)
    def _():
        def edge(t0, w):
            t = t0 + lax.broadcasted_iota(jnp.int32, (HALO, 1), 0)
            return 1.0 / (jnp.minimum(t + w // 2, seq) - jnp.maximum(t - w // 2, 0)).astype(F32)

        finish([jnp.concatenate([edge(i * tm, w), jnp.full((tm - 2 * HALO, 1), 1.0 / w, F32),
                                 edge((i + 1) * tm - HALO, w)], axis=0) for w in POOL_WINDOWS])

    @pl.when(jnp.logical_not(clipped))
    def _():
        finish([1.0 / w for w in POOL_WINDOWS])


def _pool(x, mods, layer, g, w_pool, pool_scale):
    b, t, d = x.shape
    tm = TM_POOL
    hb = tm // HALO
    last = t // HALO - 1
    return pl.pallas_call(
        functools.partial(_pool_kernel, seq=t),
        grid=(b, t // tm),
        in_specs=[
            pl.BlockSpec((None, tm, d), lambda bi, i: (bi, i, 0)),
            pl.BlockSpec((None, HALO, d), lambda bi, i: (bi, jnp.maximum(i * hb - 1, 0), 0)),
            pl.BlockSpec((None, HALO, d), lambda bi, i: (bi, jnp.minimum((i + 1) * hb, last), 0)),
            _mod_spec(layer, d),
            _const_spec((1, d)),
            _const_spec(w_pool.shape),
            _const_spec((1, d)),
        ],
        out_specs=pl.BlockSpec((None, tm, d), lambda bi, i: (bi, i, 0)),
        out_shape=jax.ShapeDtypeStruct((b, t, d), F32),
        compiler_params=_cparams("parallel", "parallel"),
        name="pool_mixer",
    )(x, x, x, mods, g.reshape(1, d), w_pool, pool_scale.reshape(1, d))


def _rope_tables(t, pad_rows):
    n = A_HEAD_DIM // 4
    freqs = (np.float32(ROPE_BASE) ** (-np.arange(n, dtype=np.float32) / np.float32(n))).astype(np.float32)
    pos = np.arange(t)
    ang_r = (pos // GRID_W).astype(np.float32)[:, None] * freqs
    ang_c = (pos % GRID_W).astype(np.float32)[:, None] * freqs
    cos = np.concatenate([np.cos(ang_r), np.cos(ang_r), np.cos(ang_c), np.cos(ang_c)], axis=1)
    sin = np.concatenate([-np.sin(ang_r), np.sin(ang_r), -np.sin(ang_c), np.sin(ang_c)], axis=1)
    cos = np.concatenate([np.tile(cos, (1, 2)), np.ones((pad_rows, LANES), np.float32)], axis=0)
    sin = np.concatenate([np.tile(sin, (1, 2)), np.zeros((pad_rows, LANES), np.float32)], axis=0)
    return jnp.asarray(cos, F32), jnp.asarray(sin, F32)


def kernel(x, c, ctx, c_ctx, w_mod, b_mod, norm_g, ffn1_wi, ffn1_wo, ffn2_wi, ffn2_wo,
           w_in, w_a2_f, b_a_f, w_a2_b, b_a_b, sink, gla_g, w_out, w_pool, pool_scale, final_g):
    b, t, d = x.shape
    lctx = ctx.shape[1]
    n_tok = b * t
    ctx_row = b

    cond = jnp.concatenate([c, c_ctx[None, :], jnp.zeros((COND_ROWS - b - 1, d), F32)], axis=0)
    mods = _adaln(cond, w_mod, b_mod)
    x = x.reshape(n_tok, d)

    zr = jnp.zeros((B_GATE_RANK, B_QK), F32)
    w2 = jnp.concatenate([jnp.concatenate([w_a2_f[0], zr], axis=1),
                          jnp.concatenate([zr, w_a2_b[0]], axis=1)], axis=0).astype(BF16)
    b2 = jnp.concatenate([b_a_f[0], b_a_b[0]]).reshape(1, 2 * B_QK)
    cos, sin = _rope_tables(t, TM_FFN_PROJ)
    xc, z, la = _ffn(
        x, n_tok, t, mods, norm_g[0, 0], ffn1_wi, ffn1_wo, 0, 0, tm=TM_FFN_PROJ,
        ctx=ctx.reshape(b * lctx, d), ctx_row=ctx_row, proj=(norm_g[0, 1], w_in[0].astype(BF16), w2, b2, cos, sin))

    o_a = _attention(sink[0], z, b, t, lctx)
    st_f = _gla_ctx_state(z, la, b, t, lctx, False)
    st_b = _gla_ctx_state(z, la, b, t, lctx, True)
    o_f = _gla(z, la, st_f, b, t, False)
    o_g = _gla(z, la, st_b, b, t, True, o_f, gla_g[0])
    x = _ffn(xc, n_tok, t, mods, norm_g[0, 2], ffn2_wi, ffn2_wo, 0, 6, mix=(o_a, o_g, w_out[0].astype(BF16)))

    x = _ffn(x, n_tok, t, mods, norm_g[1, 0], ffn1_wi, ffn1_wo, 1, 0)
    x = _pool(x.reshape(b, t, d), mods, 1, norm_g[1, 1], w_pool[0].astype(BF16), pool_scale[0])
    x = _ffn(x.reshape(n_tok, d), n_tok, t, mods, norm_g[1, 2], ffn2_wi, ffn2_wo, 1, 6, final_g=final_g)
    return x.reshape(b, t, d)
```

```python
import functools

import jax
import jax.numpy as jnp
import numpy as np
from jax import lax
from jax.experimental import pallas as pl
from jax.experimental.pallas import tpu as pltpu

F32 = jnp.float32
BF16 = jnp.bfloat16

D_MODEL = 1024
N_MOD = 9
D_FF = 2816
RMS_EPS = 1e-6
GRID_W = 64
A_HEADS = 8
A_KV_HEADS = 2
A_HEAD_DIM = 64
WINDOW = 128
ROPE_BASE = 10000.0
B_HEADS = 4
B_DK = 64
B_DV = 128
B_GATE_RANK = 16
B_GATE_NORM = 16.0
B_CHUNK = 64
POOL_WINDOWS = (2, 4, 8, 16)
POOL_GROUP = D_MODEL // len(POOL_WINDOWS)
A_Q = A_HEADS * A_HEAD_DIM
A_KV = A_KV_HEADS * A_HEAD_DIM
B_QK = B_HEADS * B_DK
B_V = B_HEADS * B_DV
PROJ_MAIN = A_Q + 2 * A_KV + 2 * B_QK + 2 * B_V
KV_DUP = 4 * A_KV

LOG2E = 1.4426950408889634
LANES = 128
COND_ROWS = 8
VMEM_LIMIT = 56 * 1024 * 1024

TM_FFN = 1024
SUB_FFN = 256
TM_FFN_PROJ = 512
TM_POOL = 1024
TB_GLA = 1024
GLA_GROUP = 4
HALO = 8


def _cparams(*sem):
    return pltpu.CompilerParams(dimension_semantics=sem, vmem_limit_bytes=VMEM_LIMIT)


def _const_spec(shape):
    nd = len(shape)
    return pl.BlockSpec(shape, lambda *_: (0,) * nd, pipeline_mode=pl.Buffered(1))


def _modulate(x, g, shift, scale):
    y = x * lax.rsqrt(jnp.mean(x * x, axis=-1, keepdims=True) + RMS_EPS)
    return (y * g) * (1.0 + scale) + shift


def _silu(a):
    return a * jax.nn.sigmoid(a)


ADALN_MODS_PER_STEP = 3


def _adaln_kernel(c_ref, w_ref, b_ref, o_ref):
    d = c_ref.shape[1]
    s = _silu(c_ref[...]).astype(BF16)
    bias = b_ref[pl.ds(pl.program_id(0), 1), :]
    r = jnp.dot(s, w_ref[...].astype(BF16), preferred_element_type=F32) + bias
    for k in range(ADALN_MODS_PER_STEP):
        o_ref[k] = r[:, k * d:(k + 1) * d]


def _adaln(cond, w_mod, b_mod):
    depth, d, n = w_mod.shape
    per = ADALN_MODS_PER_STEP
    tn = per * d
    return pl.pallas_call(
        _adaln_kernel,
        grid=(depth, N_MOD // per),
        in_specs=[
            pl.BlockSpec((COND_ROWS, d), lambda l, j: (0, 0)),
            pl.BlockSpec((None, d, tn), lambda l, j: (l, 0, j)),
            pl.BlockSpec((depth, tn), lambda l, j: (0, j)),
        ],
        out_specs=pl.BlockSpec((None, per, COND_ROWS, d), lambda l, j: (l, j, 0, 0)),
        out_shape=jax.ShapeDtypeStruct((depth, N_MOD, COND_ROWS, d), F32),
        compiler_params=_cparams("parallel", "parallel"),
        name="adaln",
    )(cond, w_mod, b_mod)


def _mod_spec(layer, d):
    return pl.BlockSpec((None, N_MOD, COND_ROWS, d), lambda *_: (layer, 0, 0, 0))


def _mod(mod_ref, idx, row):
    return mod_ref[idx, pl.ds(row, 1), :]


FFN_WCHUNK = 256
FFN_NCH = D_FF // FFN_WCHUNK
FFN_CHUNKS_PER_STEP = 2
FFN_NW = 2 * FFN_NCH // FFN_CHUNKS_PER_STEP
FFN_WO_ROWS = D_FF // FFN_NW
FFN_MIX_LEAD = 5


def _ffn_tile(x_ref, o_ref, mod_ref, row, g_ref, wi_s, wo_s, mi, mix, fg_ref, proj):
    sub = min(SUB_FFN, x_ref.shape[0])
    nsub = x_ref.shape[0] // sub

    def residual_in(s):
        rows = slice(s * sub, (s + 1) * sub)
        x = x_ref[rows, :]
        if mix is not None:
            oa_ref, ob_ref, w_ref = mix
            y0 = (jnp.dot(oa_ref[rows, :], w_ref[:A_Q, :], preferred_element_type=F32)
                  + jnp.dot(ob_ref[rows, :], w_ref[A_Q:, :], preferred_element_type=F32))
            x = x + _mod(mod_ref, 5, row) * y0
        return x

    def normed(x):
        return _modulate(x, g_ref[...], _mod(mod_ref, mi, row), _mod(mod_ref, mi + 1, row)).astype(BF16)

    def act_chunk(h, c):
        a = jnp.dot(h, wi_s[c], preferred_element_type=F32)
        u = jnp.dot(h, wi_s[FFN_NCH + c], preferred_element_type=F32)
        return (_silu(a) * u).astype(BF16)

    x_next = residual_in(0)
    h_next = normed(x_next)
    early = []
    pending = []
    for s in range(nsub):
        rows = slice(s * sub, (s + 1) * sub)
        x, h = x_next, h_next
        acts, early = early, []
        for c in range(len(acts), FFN_NCH):
            if s + 1 < nsub:
                if c == FFN_NCH - FFN_MIX_LEAD:
                    x_next = residual_in(s + 1)
                if c == FFN_NCH - 1:
                    h_next = normed(x_next)
                    early = [act_chunk(h_next, 0)]
            acts.append(act_chunk(h, c))
        y = jnp.dot(jnp.concatenate(acts, axis=1), wo_s[...], preferred_element_type=F32)
        o = x + (0.5 * _mod(mod_ref, mi + 2, row)) * y
        if fg_ref is not None:
            o = (o * lax.rsqrt(jnp.mean(o * o, axis=-1, keepdims=True) + RMS_EPS)) * fg_ref[...]
        o_ref[rows, :] = o
        if proj is not None:
            pending.append(
                (_modulate(o, proj[0][...], _mod(mod_ref, 3, row), _mod(mod_ref, 4, row)).astype(BF16), rows))
    for h2, rows in pending:
        for stage in proj[1]:
            stage(h2, rows)


def _ffn_kernel(*refs, mi, nx, tiles_per_batch, ctx_row, has_ctx, has_mix, final, has_proj):
    refs = list(refs)
    x_ref = refs.pop(0)
    c_ref = refs.pop(0) if has_ctx else None
    mod_ref, g_ref, wi_ref, wo_ref = refs[:4]
    refs = refs[4:]
    mix = tuple(refs[:3]) if has_mix else None
    refs = refs[3:] if has_mix else refs
    fg_ref = refs.pop(0) if final else None
    proj = None
    if has_proj:
        g1_ref, wp_ref, w2_ref, b2_ref, cos_ref, sin_ref = refs[:N_PROJ_IN]
        outs = refs[N_PROJ_IN + 1:N_PROJ_IN + 1 + N_PROJ_OUT]
        proj = (g1_ref, _proj_stage_fns(wp_ref, w2_ref, b2_ref, cos_ref, sin_ref, outs))
        refs = refs[N_PROJ_IN:N_PROJ_IN + 1] + refs[N_PROJ_IN + 1 + N_PROJ_OUT:]
    o_ref, wi_s, wo_s = refs
    j = pl.program_id(0)

    @pl.when(j < FFN_NW)
    def _():
        for k in range(FFN_CHUNKS_PER_STEP):
            wi_s[j * FFN_CHUNKS_PER_STEP + k] = wi_ref[:, k * FFN_WCHUNK:(k + 1) * FFN_WCHUNK].astype(BF16)
        wo_s[pl.ds(pl.multiple_of(j * FFN_WO_ROWS, FFN_WO_ROWS), FFN_WO_ROWS), :] = wo_ref[...].astype(BF16)

    @pl.when((j >= FFN_NW) & (j < FFN_NW + nx))
    def _():
        _ffn_tile(x_ref, o_ref, mod_ref, (j - FFN_NW) // tiles_per_batch, g_ref, wi_s, wo_s, mi, mix, fg_ref, proj)

    if has_ctx:
        @pl.when(j >= FFN_NW + nx)
        def _():
            _ffn_tile(c_ref, o_ref, mod_ref, ctx_row, g_ref, wi_s, wo_s, mi, None, fg_ref, proj)


def _ffn(x, n_rows, tok_per_batch, mods, g, wi, wo, layer, mi, *, tm=TM_FFN, ctx=None, ctx_row=None, mix=None,
         final_g=None, proj=None):
    d = x.shape[1]
    nx = n_rows // tm
    nc = 0 if ctx is None else ctx.shape[0] // tm
    tiles_per_batch = tok_per_batch // tm

    def tile(j):
        return jnp.clip(j - FFN_NW, 0, nx + nc - 1)

    def x_tile(j):
        return jnp.minimum(tile(j), nx - 1)

    def wchunk(j):
        return jnp.minimum(j, FFN_NW - 1)

    def pos_blk(j):
        return jnp.where(tile(j) < nx, tile(j) % tiles_per_batch, tiles_per_batch)

    in_specs = [pl.BlockSpec((tm, d), lambda j: (x_tile(j), 0))]
    args = [x]
    if ctx is not None:
        in_specs.append(pl.BlockSpec((tm, d), lambda j: (jnp.clip(j - FFN_NW - nx, 0, nc - 1), 0)))
        args.append(ctx)
    in_specs += [
        _mod_spec(layer, d),
        _const_spec((1, d)),
        pl.BlockSpec((None, d, FFN_WCHUNK * FFN_CHUNKS_PER_STEP), lambda j: (layer, 0, wchunk(j))),
        pl.BlockSpec((None, FFN_WO_ROWS, d), lambda j: (layer, wchunk(j), 0)),
    ]
    args += [mods, g.reshape(1, d), wi, wo]
    if mix is not None:
        oa, ob, w_out = mix
        in_specs += [pl.BlockSpec((tm, A_Q), lambda j: (x_tile(j), 0)),
                     pl.BlockSpec((tm, B_V), lambda j: (x_tile(j), 0)),
                     _const_spec(w_out.shape)]
        args += [oa, ob, w_out]
    if final_g is not None:
        in_specs.append(_const_spec((1, d)))
        args.append(final_g.reshape(1, d))
    n_out_rows = (nx + nc) * tm
    out_specs = [pl.BlockSpec((tm, d), lambda j: (tile(j), 0))]
    out_shape = [jax.ShapeDtypeStruct((n_out_rows, d), F32)]
    if proj is not None:
        g1, w_in, w2, b2, cos, sin = proj
        in_specs += [_const_spec((1, d)), _const_spec(w_in.shape), _const_spec(w2.shape),
                     _const_spec(b2.shape),
                     pl.BlockSpec((tm, LANES), lambda j: (pos_blk(j), 0)),
                     pl.BlockSpec((tm, LANES), lambda j: (pos_blk(j), 0))]
        args += [g1.reshape(1, d), w_in, w2, b2, cos, sin]
        out_specs += [pl.BlockSpec((tm, w), lambda j: (tile(j), 0)) for w, _ in PROJ_OUTS]
        out_shape += [jax.ShapeDtypeStruct((n_out_rows, w), dt) for w, dt in PROJ_OUTS]
    res = pl.pallas_call(
        functools.partial(_ffn_kernel, mi=mi, nx=nx, tiles_per_batch=tiles_per_batch, ctx_row=ctx_row,
                          has_ctx=ctx is not None, has_mix=mix is not None,
                          final=final_g is not None, has_proj=proj is not None),
        grid=(FFN_NW + nx + nc,),
        in_specs=in_specs,
        out_specs=out_specs,
        out_shape=out_shape,
        scratch_shapes=[pltpu.VMEM((2 * FFN_NCH, d, FFN_WCHUNK), BF16), pltpu.VMEM((D_FF, d), BF16)],
        compiler_params=_cparams("arbitrary"),
        name="ffn_proj" if proj is not None else "ffn",
    )(*args)
    return res if proj is not None else res[0]


def _swap16(x):
    lane = lax.broadcasted_iota(jnp.int32, x.shape, 1)
    return jnp.where((lane & 16) == 0, pltpu.roll(x, LANES - 16, 1), pltpu.roll(x, 16, 1))


def _dup_halves(x):
    lane = lax.broadcasted_iota(jnp.int32, x.shape, 1)
    r = pltpu.roll(x, LANES // 2, 1)
    lo = lane < LANES // 2
    return jnp.where(lo, x, r), jnp.where(lo, r, x)


def _halves_with_ones(x):
    lane = lax.broadcasted_iota(jnp.int32, x.shape, 1)
    r = pltpu.roll(x, LANES // 2, 1)
    lo = lane < LANES // 2
    return jnp.where(lo, x, 1.0), jnp.where(lo, r, 1.0)


Z_QA = (A_Q, 0)
Z_KV = (KV_DUP, 1)
Z_VB = (B_V, 2)
Z_RB = (B_V, 3)
Z_QB = (B_QK, 8)
Z_KB = (B_QK, 9)
Z_WIDTH = (Z_KB[1] + 1) * B_QK
PROJ_OUTS = ((Z_WIDTH, BF16), (2 * B_QK, F32))
N_PROJ_OUT = len(PROJ_OUTS)
N_PROJ_IN = 6


def _z_cols(field):
    w, blk = field
    return slice(w * blk, w * (blk + 1))


def _proj_stage_fns(w_ref, w2_ref, b2_ref, cos_ref, sin_ref, outs):
    z_ref, la_ref = outs
    qa0, kv0 = _z_cols(Z_QA).start, _z_cols(Z_KV).start
    scale = A_HEAD_DIM ** -0.5 * LOG2E
    o_b = A_Q + 2 * A_KV

    def cols(h, lo, hi):
        return jnp.dot(h, w_ref[:, lo:hi], preferred_element_type=F32)

    def rope(v, rows):
        return v * cos_ref[rows, :] + _swap16(v) * sin_ref[rows, :]

    def attn_q(h, rows):
        z = cols(h, 0, A_Q)
        for p in range(A_Q // LANES):
            q = z[:, p * LANES:(p + 1) * LANES]
            z_ref[rows, qa0 + p * LANES:qa0 + (p + 1) * LANES] = (rope(q, rows) * scale).astype(BF16)

    def attn_kv(h, rows):
        z = cols(h, A_Q, A_Q + 2 * A_KV)
        k0, k1 = _dup_halves(rope(z[:, :A_KV], rows))
        v0, v1 = _halves_with_ones(z[:, A_KV:])
        for j, piece in enumerate((k0, k1, v0, v1)):
            z_ref[rows, kv0 + j * LANES:kv0 + (j + 1) * LANES] = piece.astype(BF16)

    def gla_qk(h, rows):
        z = cols(h, o_b, o_b + 2 * B_QK)
        z_ref[rows, _z_cols(Z_QB)] = z[:, :B_QK].astype(BF16)
        z_ref[rows, _z_cols(Z_KB)] = z[:, B_QK:].astype(BF16)

    def gla_v(h, rows):
        z_ref[rows, _z_cols(Z_VB)] = cols(h, o_b + 2 * B_QK, o_b + 2 * B_QK + B_V).astype(BF16)

    def gla_r(h, rows):
        z_ref[rows, _z_cols(Z_RB)] = cols(h, o_b + 2 * B_QK + B_V, o_b + 2 * B_QK + 2 * B_V).astype(BF16)

    def gla_gate(h, rows):
        zg = jnp.dot(h, w_ref[:, PROJ_MAIN:], preferred_element_type=F32).astype(BF16)
        pre = jnp.dot(zg, w2_ref[...], preferred_element_type=F32) + b2_ref[...]
        la_ref[rows, :] = (jnp.minimum(pre, 0.0) - jnp.log1p(jnp.exp(-jnp.abs(pre)))) / B_GATE_NORM

    return (gla_gate, attn_q, attn_kv, gla_qk, gla_v, gla_r)


ATTN_QBLOCKS = 8


def _attn_kernel(sink_ref, q_ref, kvp_ref, kvc_ref, kvn_ref, ckv_ref, o_ref, *, nsteps):
    n = pl.program_id(1)
    blk = WINDOW
    row = lax.broadcasted_iota(jnp.int32, (blk, blk), 0)
    col = lax.broadcasted_iota(jnp.int32, (blk, blk), 1)
    ninf = jnp.float32(-jnp.inf)
    lane = lax.broadcasted_iota(jnp.int32, (blk, LANES), 1)
    lo = lane < LANES // 2
    rep = A_HEADS // A_KV_HEADS

    def key_block(i, lanes):
        if i == 0:
            return kvp_ref[:, lanes]
        if i == ATTN_QBLOCKS + 1:
            return kvn_ref[:, lanes]
        return kvc_ref[(i - 1) * blk:i * blk, lanes]

    def band(qb, lanes):
        return jnp.concatenate([key_block(qb + j, lanes) for j in range(3)] + [ckv_ref[:, lanes]], axis=0)

    kd = [[band(qb, slice(g * LANES, (g + 1) * LANES)) for g in range(A_KV_HEADS)] for qb in range(ATTN_QBLOCKS)]
    vd = [[band(qb, slice((A_KV_HEADS + g) * LANES, (A_KV_HEADS + g + 1) * LANES)) for g in range(A_KV_HEADS)]
          for qb in range(ATTN_QBLOCKS)]
    nkb = kd[0][0].shape[0] // blk
    bias_prev = [jnp.where((col >= row) & ((n > 0) if qb == 0 else True), 0.0, ninf) for qb in range(ATTN_QBLOCKS)]
    bias_next = [jnp.where((col <= row) & ((n < nsteps - 1) if qb == ATTN_QBLOCKS - 1 else True), 0.0, ninf)
                 for qb in range(ATTN_QBLOCKS)]

    def scores(qb, p):
        pair = q_ref[qb * blk:(qb + 1) * blk, p * LANES:(p + 1) * LANES]
        zero = jnp.zeros_like(pair)
        lhs = jnp.concatenate([jnp.where(lo, pair, zero), jnp.where(lo, zero, pair)], axis=0)
        return lax.dot_general(lhs, kd[qb][(2 * p) // rep], (((1,), (1,)), ((), ())), preferred_element_type=F32)

    def softmax(qb, p, s):
        ps, sinks = [], []
        for r in range(2):
            sk = sink_ref[2 * p + r] * LOG2E
            cb = [s[r * blk:(r + 1) * blk, j * blk:(j + 1) * blk] for j in range(nkb)]
            cb[0] = cb[0] + bias_prev[qb]
            cb[2] = cb[2] + bias_next[qb]
            mx = cb[0]
            for c in cb[1:]:
                mx = jnp.maximum(mx, c)
            m = jnp.maximum(jnp.max(mx, axis=-1, keepdims=True), sk)
            ps.append(jnp.concatenate([jnp.exp2((c - m).astype(BF16)) for c in cb], axis=1))
            sinks.append(jnp.exp2(sk - m))
        return jnp.concatenate(ps, axis=0), sinks

    def weighted_values(qb, p, probs, sinks):
        o = jnp.dot(probs, vd[qb][(2 * p) // rep], preferred_element_type=F32)
        o_even, o_odd = o[:blk, :], o[blk:, :]
        num = jnp.where(lo, o_even, pltpu.roll(o_odd, LANES // 2, 1))
        den = jnp.where(lo, pltpu.roll(o_even, LANES // 2, 1) + sinks[0], o_odd + sinks[1])
        o_ref[qb * blk:(qb + 1) * blk, p * LANES:(p + 1) * LANES] = (num / den).astype(BF16)

    units = [(qb, p) for qb in range(ATTN_QBLOCKS) for p in range(A_HEADS // 2)]
    s_next = scores(*units[0])
    for i, unit in enumerate(units):
        s_cur = s_next
        if i + 1 < len(units):
            s_next = scores(*units[i + 1])
        probs, sinks = softmax(*unit, s_cur)
        weighted_values(*unit, probs, sinks)


def _attention(sink, z, b, t, lctx):
    nb = t // WINDOW
    tq = ATTN_QBLOCKS * WINDOW
    nsteps = t // tq
    ctx0 = (b * t) // lctx

    def edge(off):
        return pl.BlockSpec((WINDOW, KV_DUP),
                            lambda bi, n: (bi * nb + jnp.clip(n * ATTN_QBLOCKS + off, 0, nb - 1), Z_KV[1]))

    return pl.pallas_call(
        functools.partial(_attn_kernel, nsteps=nsteps),
        grid=(b, nsteps),
        in_specs=[
            pl.BlockSpec(memory_space=pltpu.SMEM),
            pl.BlockSpec((tq, A_Q), lambda bi, n: (bi * nsteps + n, Z_QA[1])),
            edge(-1),
            pl.BlockSpec((tq, KV_DUP), lambda bi, n: (bi * nsteps + n, Z_KV[1])),
            edge(ATTN_QBLOCKS),
            pl.BlockSpec((lctx, KV_DUP), lambda bi, n: (ctx0 + bi, Z_KV[1])),
        ],
        out_specs=pl.BlockSpec((tq, A_Q), lambda bi, n: (bi * nsteps + n, 0)),
        out_shape=jax.ShapeDtypeStruct((b * t, A_Q), BF16),
        compiler_params=_cparams("parallel", "parallel"),
        name="window_attn",
    )(sink, z, z, z, z, z)


B_PAIRS = B_HEADS // 2
PAIR_QK = 2 * B_DK
PAIR_V = 2 * B_DV


class _GlaMasks:
    def __init__(self, reverse):
        c = B_CHUNK
        i = lax.broadcasted_iota(jnp.int32, (c, c), 0)
        j = lax.broadcasted_iota(jnp.int32, (c, c), 1)
        self.tri = jnp.where((j >= i) if reverse else (j <= i), 1.0, 0.0).astype(BF16)
        lane_qk = lax.broadcasted_iota(jnp.int32, (c, B_QK), 1)
        self.k_head = [(lane_qk >= h * B_DK) & (lane_qk < (h + 1) * B_DK) for h in range(B_HEADS)]
        lane_v = lax.broadcasted_iota(jnp.int32, (c, B_V), 1)
        self.v_head = [(lane_v >= h * B_DV) & (lane_v < (h + 1) * B_DV) for h in range(B_HEADS)]
        ai = lax.broadcasted_iota(jnp.int32, (c, B_HEADS * c), 0)
        aj = lax.broadcasted_iota(jnp.int32, (c, B_HEADS * c), 1) % c
        self.causal = (aj >= ai) if reverse else (aj <= ai)
        sr = lax.broadcasted_iota(jnp.int32, (PAIR_V, PAIR_QK), 0) // B_DV
        sl = lax.broadcasted_iota(jnp.int32, (PAIR_V, PAIR_QK), 1) // B_DK
        self.state_diag = sr == sl


def _gla_block(q_ref, k_ref, v_ref, la_ref, st, mk, reverse, want_out):
    c = B_CHUNK
    nc = k_ref.shape[0] // c
    order = list(reversed(range(nc))) if reverse else list(range(nc))
    rows = {ci: slice(ci * c, (ci + 1) * c) for ci in order}
    tb = (((1,), (1,)), ((), ()))
    ta = (((0,), (0,)), ((), ()))

    g = {}
    for ci in order:
        la = la_ref[rows[ci], :]
        hi = la.astype(BF16)
        lo = (la - hi.astype(F32)).astype(BF16)
        g[ci] = jnp.dot(mk.tri, hi, preferred_element_type=F32) + jnp.dot(mk.tri, lo, preferred_element_type=F32)

    o = {}
    for g0 in range(0, nc, GLA_GROUP):
        group = order[g0:g0 + GLA_GROUP]
        dec, k_end, q_t, kstack = {}, {}, {}, {}
        for ci in group:
            k = k_ref[rows[ci], :].astype(F32)
            g_last = g[ci][0:1, :] if reverse else g[ci][c - 1:c, :]
            dec[ci] = jnp.exp(g_last)
            k_end[ci] = (k * jnp.exp(g_last - g[ci])).astype(BF16)
            if want_out:
                q_t[ci] = ((q_ref[rows[ci], :].astype(F32) * (B_DK ** -0.5)) * jnp.exp(g[ci])).astype(BF16)
                k_t = k * jnp.exp(-g[ci])
                kstack[ci] = jnp.concatenate([jnp.where(m, k_t, 0.0) for m in mk.k_head], axis=0).astype(BF16)

        ds = {}
        for ci in group:
            v = v_ref[rows[ci], :]
            ds[ci] = [jnp.where(mk.state_diag,
                                lax.dot_general(v[:, p * PAIR_V:(p + 1) * PAIR_V],
                                                k_end[ci][:, p * PAIR_QK:(p + 1) * PAIR_QK], ta,
                                                preferred_element_type=F32), 0.0) for p in range(B_PAIRS)]

        st_prev = {}
        for ci in group:
            st_prev[ci] = st
            st = [st[p] * dec[ci][:, p * PAIR_QK:(p + 1) * PAIR_QK] + ds[ci][p] for p in range(B_PAIRS)]
        if not want_out:
            continue

        att = {}
        for ci in group:
            a = lax.dot_general(q_t[ci], kstack[ci], tb, preferred_element_type=F32)
            att[ci] = jnp.where(mk.causal, a, 0.0).astype(BF16)
        for ci in group:
            v = v_ref[rows[ci], :]
            vstack = jnp.concatenate([jnp.where(m, v, jnp.zeros_like(v)) for m in mk.v_head], axis=0)
            o[ci] = jnp.dot(att[ci], vstack, preferred_element_type=F32) + jnp.concatenate(
                [lax.dot_general(q_t[ci][:, p * PAIR_QK:(p + 1) * PAIR_QK], st_prev[ci][p].astype(BF16), tb,
                                 preferred_element_type=F32) for p in range(B_PAIRS)], axis=1)
    if not want_out:
        return None, st
    return o, st


def _gla_ctx_kernel(k_ref, v_ref, la_ref, st_ref, *, reverse):
    mk = _GlaMasks(reverse)
    st = [jnp.zeros((PAIR_V, PAIR_QK), F32) for _ in range(B_PAIRS)]
    _, st = _gla_block(None, k_ref, v_ref, la_ref, st, mk, reverse, False)
    for p in range(B_PAIRS):
        st_ref[p] = st[p]


def _gla_ctx_state(z, la, b, t, l, reverse):
    d = 1 if reverse else 0
    ctx0 = (b * t) // l
    return pl.pallas_call(
        functools.partial(_gla_ctx_kernel, reverse=reverse),
        grid=(b,),
        in_specs=[
            pl.BlockSpec((l, B_QK), lambda bi: (ctx0 + bi, Z_KB[1])),
            pl.BlockSpec((l, B_V), lambda bi: (ctx0 + bi, Z_VB[1])),
            pl.BlockSpec((l, B_QK), lambda bi: (ctx0 + bi, d)),
        ],
        out_specs=pl.BlockSpec((None, B_PAIRS, PAIR_V, PAIR_QK), lambda bi: (bi, 0, 0, 0)),
        out_shape=jax.ShapeDtypeStruct((b, B_PAIRS, PAIR_V, PAIR_QK), F32),
        compiler_params=_cparams("parallel"),
        name="gla_ctx_state",
    )(z, z, la)


def _gla_kernel(q_ref, k_ref, v_ref, la_ref, st0_ref, *rest, reverse):
    if reverse:
        of_ref, r_ref, gg_ref, o_ref, st_ref = rest
    else:
        o_ref, st_ref = rest

    @pl.when(pl.program_id(1) == 0)
    def _():
        st_ref[...] = st0_ref[...]

    mk = _GlaMasks(reverse)
    outs_by_chunk, st = _gla_block(q_ref, k_ref, v_ref, la_ref, [st_ref[p] for p in range(B_PAIRS)],
                                   mk, reverse, True)
    for ci, o in outs_by_chunk.items():
        rows = slice(ci * B_CHUNK, (ci + 1) * B_CHUNK)
        if reverse:
            o = o + of_ref[rows, :]
            outs = []
            for h in range(B_HEADS):
                oh = o[:, h * B_DV:(h + 1) * B_DV]
                oh = (oh * lax.rsqrt(jnp.mean(oh * oh, axis=-1, keepdims=True) + RMS_EPS)) * gg_ref[...]
                outs.append(oh * _silu(r_ref[rows, h * B_DV:(h + 1) * B_DV].astype(F32)))
            o_ref[rows, :] = jnp.concatenate(outs, axis=1).astype(BF16)
        else:
            o_ref[rows, :] = o
    for p in range(B_PAIRS):
        st_ref[p] = st[p]


def _gla(z, la, st0, b, t, reverse, o_fwd=None, gla_g=None):
    tb = TB_GLA
    nt = t // tb
    d = 1 if reverse else 0

    def tok(w, lane_blk=0):
        if reverse:
            return pl.BlockSpec((tb, w), lambda bi, j: (bi * nt + nt - 1 - j, lane_blk))
        return pl.BlockSpec((tb, w), lambda bi, j: (bi * nt + j, lane_blk))

    in_specs = [tok(*Z_QB), tok(*Z_KB), tok(*Z_VB), tok(B_QK, d),
                pl.BlockSpec((None, B_PAIRS, PAIR_V, PAIR_QK), lambda bi, j: (bi, 0, 0, 0))]
    args = [z, z, z, la, st0]
    if reverse:
        in_specs += [tok(B_V), tok(*Z_RB), _const_spec((1, B_DV))]
        args += [o_fwd, z, gla_g.reshape(1, B_DV)]
    return pl.pallas_call(
        functools.partial(_gla_kernel, reverse=reverse),
        grid=(b, nt),
        in_specs=in_specs,
        out_specs=tok(B_V),
        out_shape=jax.ShapeDtypeStruct((b * t, B_V), BF16 if reverse else F32),
        scratch_shapes=[pltpu.VMEM((B_PAIRS, PAIR_V, PAIR_QK), F32)],
        compiler_params=_cparams("parallel", "arbitrary"),
        name="gla_bwd" if reverse else "gla_fwd",
    )(*args)


def _shift_rows(e, d):
    n = e.shape[0]
    return pltpu.roll(e, (-d) % n, 0)


def _pool_kernel(x_ref, xp_ref, xn_ref, mod_ref, g_ref, w_ref, ps_ref, o_ref, *, seq):
    i = pl.program_id(1)
    tm = x_ref.shape[0]
    g = g_ref[...]
    row = pl.program_id(0)
    shift = _mod(mod_ref, 3, row)
    gain = g * (1.0 + _mod(mod_ref, 4, row))
    out_gain = _mod(mod_ref, 5, row) * ps_ref[...]

    def normed(v):
        return (v * lax.rsqrt(jnp.mean(v * v, axis=-1, keepdims=True) + RMS_EPS)) * gain + shift

    x = x_ref[...]
    h = normed(x)
    hp = jnp.where(i > 0, normed(xp_ref[...]), 0.0)
    hn = jnp.where(i < pl.num_programs(1) - 1, normed(xn_ref[...]), 0.0)
    groups, totals = [], []
    for gi, w in enumerate(POOL_WINDOWS):
        ls = slice(gi * POOL_GROUP, (gi + 1) * POOL_GROUP)
        groups.append(h[:, ls])
        f = jnp.concatenate([hp[:, ls], h[:, ls], hn[:, ls]], axis=0)
        half = w // 2
        k = 1
        while k < half:
            f = f + _shift_rows(f, k)
            k *= 2
        if half == HALO:
            totals.append(f[:tm, :] + f[HALO:HALO + tm, :])
        else:
            totals.append((_shift_rows(f, -half) + f)[HALO:HALO + tm, :])

    def finish(scales):
        ys = [jnp.dot((totals[gi] * scales[gi] - groups[gi]).astype(BF16), w_ref[gi], preferred_element_type=F32)
              for gi in range(len(POOL_WINDOWS))]
        o_ref[...] = x + out_gain * jnp.concatenate(ys, axis=1)

    clipped = (i == 0) | (i == pl.num_programs(1) - 1)

    @pl.when(clipped)
    def _():
        def edge(t0, w):
            t = t0 + lax.broadcasted_iota(jnp.int32, (HALO, 1), 0)
            return 1.0 / (jnp.minimum(t + w // 2, seq) - jnp.maximum(t - w // 2, 0)).astype(F32)

        finish([jnp.concatenate([edge(i * tm, w), jnp.full((tm - 2 * HALO, 1), 1.0 / w, F32),
                                 edge((i + 1) * tm - HALO, w)], axis=0) for w in POOL_WINDOWS])

    @pl.when(jnp.logical_not(clipped))
    def _():
        finish([1.0 / w for w in POOL_WINDOWS])


def _pool(x, mods, layer, g, w_pool, pool_scale):
    b, t, d = x.shape
    tm = TM_POOL
    hb = tm // HALO
    last = t // HALO - 1
    return pl.pallas_call(
        functools.partial(_pool_kernel, seq=t),
        grid=(b, t // tm),
        in_specs=[
            pl.BlockSpec((None, tm, d), lambda bi, i: (bi, i, 0)),
            pl.BlockSpec((None, HALO, d), lambda bi, i: (bi, jnp.maximum(i * hb - 1, 0), 0)),
            pl.BlockSpec((None, HALO, d), lambda bi, i: (bi, jnp.minimum((i + 1) * hb, last), 0)),
            _mod_spec(layer, d),
            _const_spec((1, d)),
            _const_spec(w_pool.shape),
            _const_spec((1, d)),
        ],
        out_specs=pl.BlockSpec((None, tm, d), lambda bi, i: (bi, i, 0)),
        out_shape=jax.ShapeDtypeStruct((b, t, d), F32),
        compiler_params=_cparams("parallel", "parallel"),
        name="pool_mixer",
    )(x, x, x, mods, g.reshape(1, d), w_pool, pool_scale.reshape(1, d))


def _rope_tables(t, pad_rows):
    n = A_HEAD_DIM // 4
    freqs = (np.float32(ROPE_BASE) ** (-np.arange(n, dtype=np.float32) / np.float32(n))).astype(np.float32)
    pos = np.arange(t)
    ang_r = (pos // GRID_W).astype(np.float32)[:, None] * freqs
    ang_c = (pos % GRID_W).astype(np.float32)[:, None] * freqs
    cos = np.concatenate([np.cos(ang_r), np.cos(ang_r), np.cos(ang_c), np.cos(ang_c)], axis=1)
    sin = np.concatenate([-np.sin(ang_r), np.sin(ang_r), -np.sin(ang_c), np.sin(ang_c)], axis=1)
    cos = np.concatenate([np.tile(cos, (1, 2)), np.ones((pad_rows, LANES), np.float32)], axis=0)
    sin = np.concatenate([np.tile(sin, (1, 2)), np.zeros((pad_rows, LANES), np.float32)], axis=0)
    return jnp.asarray(cos, F32), jnp.asarray(sin, F32)


def kernel(x, c, ctx, c_ctx, w_mod, b_mod, norm_g, ffn1_wi, ffn1_wo, ffn2_wi, ffn2_wo,
           w_in, w_a2_f, b_a_f, w_a2_b, b_a_b, sink, gla_g, w_out, w_pool, pool_scale, final_g):
    b, t, d = x.shape
    lctx = ctx.shape[1]
    n_tok = b * t
    ctx_row = b

    cond = jnp.concatenate([c, c_ctx[None, :], jnp.zeros((COND_ROWS - b - 1, d), F32)], axis=0)
    mods = _adaln(cond, w_mod, b_mod)
    x = x.reshape(n_tok, d)

    zr = jnp.zeros((B_GATE_RANK, B_QK), F32)
    w2 = jnp.concatenate([jnp.concatenate([w_a2_f[0], zr], axis=1),
                          jnp.concatenate([zr, w_a2_b[0]], axis=1)], axis=0).astype(BF16)
    b2 = jnp.concatenate([b_a_f[0], b_a_b[0]]).reshape(1, 2 * B_QK)
    cos, sin = _rope_tables(t, TM_FFN_PROJ)
    xc, z, la = _ffn(
        x, n_tok, t, mods, norm_g[0, 0], ffn1_wi, ffn1_wo, 0, 0, tm=TM_FFN_PROJ,
        ctx=ctx.reshape(b * lctx, d), ctx_row=ctx_row, proj=(norm_g[0, 1], w_in[0].astype(BF16), w2, b2, cos, sin))

    o_a = _attention(sink[0], z, b, t, lctx)
    st_f = _gla_ctx_state(z, la, b, t, lctx, False)
    st_b = _gla_ctx_state(z, la, b, t, lctx, True)
    o_f = _gla(z, la, st_f, b, t, False)
    o_g = _gla(z, la, st_b, b, t, True, o_f, gla_g[0])
    x = _ffn(xc, n_tok, t, mods, norm_g[0, 2], ffn2_wi, ffn2_wo, 0, 6, mix=(o_a, o_g, w_out[0].astype(BF16)))

    x = _ffn(x, n_tok, t, mods, norm_g[1, 0], ffn1_wi, ffn1_wo, 1, 0)
    x = _pool(x.reshape(b, t, d), mods, 1, norm_g[1, 1], w_pool[0].astype(BF16), pool_scale[0])
    x = _ffn(x.reshape(n_tok, d), n_tok, t, mods, norm_g[1, 2], ffn2_wi, ffn2_wo, 1, 6, final_g=final_g)
    return x.reshape(b, t, d)
```

```python
import functools

import jax
import jax.numpy as jnp
import numpy as np
from jax import lax
from jax.experimental import pallas as pl
from jax.experimental.pallas import tpu as pltpu

F32 = jnp.float32
BF16 = jnp.bfloat16

D_MODEL = 1024
N_MOD = 9
D_FF = 2816
RMS_EPS = 1e-6
GRID_W = 64
A_HEADS = 8
A_KV_HEADS = 2
A_HEAD_DIM = 64
WINDOW = 128
ROPE_BASE = 10000.0
B_HEADS = 4
B_DK = 64
B_DV = 128
B_GATE_RANK = 16
B_GATE_NORM = 16.0
B_CHUNK = 64
POOL_WINDOWS = (2, 4, 8, 16)
POOL_GROUP = D_MODEL // len(POOL_WINDOWS)
A_Q = A_HEADS * A_HEAD_DIM
A_KV = A_KV_HEADS * A_HEAD_DIM
B_QK = B_HEADS * B_DK
B_V = B_HEADS * B_DV
PROJ_MAIN = A_Q + 2 * A_KV + 2 * B_QK + 2 * B_V
KV_DUP = 4 * A_KV

LOG2E = 1.4426950408889634
LANES = 128
COND_ROWS = 8
VMEM_LIMIT = 56 * 1024 * 1024

TM_FFN = 1024
SUB_FFN = 256
TM_FFN_PROJ = 512
TM_POOL = 1024
TB_GLA = 1024
GLA_GROUP = 4
HALO = 8


def _cparams(*sem):
    return pltpu.CompilerParams(dimension_semantics=sem, vmem_limit_bytes=VMEM_LIMIT)


def _const_spec(shape):
    nd = len(shape)
    return pl.BlockSpec(shape, lambda *_: (0,) * nd, pipeline_mode=pl.Buffered(1))


def _modulate(x, g, shift, scale):
    y = x * lax.rsqrt(jnp.mean(x * x, axis=-1, keepdims=True) + RMS_EPS)
    return (y * g) * (1.0 + scale) + shift


def _silu(a):
    return a * jax.nn.sigmoid(a)


ADALN_MODS_PER_STEP = 3


def _adaln_kernel(c_ref, w_ref, b_ref, o_ref):
    d = c_ref.shape[1]
    s = _silu(c_ref[...]).astype(BF16)
    bias = b_ref[pl.ds(pl.program_id(0), 1), :]
    r = jnp.dot(s, w_ref[...].astype(BF16), preferred_element_type=F32) + bias
    for k in range(ADALN_MODS_PER_STEP):
        o_ref[k] = r[:, k * d:(k + 1) * d]


def _adaln(cond, w_mod, b_mod):
    depth, d, n = w_mod.shape
    per = ADALN_MODS_PER_STEP
    tn = per * d
    return pl.pallas_call(
        _adaln_kernel,
        grid=(depth, N_MOD // per),
        in_specs=[
            pl.BlockSpec((COND_ROWS, d), lambda l, j: (0, 0)),
            pl.BlockSpec((None, d, tn), lambda l, j: (l, 0, j)),
            pl.BlockSpec((depth, tn), lambda l, j: (0, j)),
        ],
        out_specs=pl.BlockSpec((None, per, COND_ROWS, d), lambda l, j: (l, j, 0, 0)),
        out_shape=jax.ShapeDtypeStruct((depth, N_MOD, COND_ROWS, d), F32),
        compiler_params=_cparams("parallel", "parallel"),
        name="adaln",
    )(cond, w_mod, b_mod)


def _mod_spec(layer, d):
    return pl.BlockSpec((None, N_MOD, COND_ROWS, d), lambda *_: (layer, 0, 0, 0))


def _mod(mod_ref, idx, row):
    return mod_ref[idx, pl.ds(row, 1), :]


FFN_WCHUNK = 256
FFN_NCH = D_FF // FFN_WCHUNK
FFN_CHUNKS_PER_STEP = 2
FFN_NW = 2 * FFN_NCH // FFN_CHUNKS_PER_STEP
FFN_WO_ROWS = D_FF // FFN_NW
FFN_MIX_LEAD = 5


def _ffn_tile(x_ref, o_ref, mod_ref, row, g_ref, wi_s, wo_s, mi, mix, fg_ref, proj):
    sub = min(SUB_FFN, x_ref.shape[0])
    nsub = x_ref.shape[0] // sub

    def residual_in(s):
        rows = slice(s * sub, (s + 1) * sub)
        x = x_ref[rows, :]
        if mix is not None:
            oa_ref, ob_ref, w_ref = mix
            y0 = (jnp.dot(oa_ref[rows, :], w_ref[:A_Q, :], preferred_element_type=F32)
                  + jnp.dot(ob_ref[rows, :], w_ref[A_Q:, :], preferred_element_type=F32))
            x = x + _mod(mod_ref, 5, row) * y0
        return x

    def normed(x):
        return _modulate(x, g_ref[...], _mod(mod_ref, mi, row), _mod(mod_ref, mi + 1, row)).astype(BF16)

    def act_chunk(h, c):
        a = jnp.dot(h, wi_s[c], preferred_element_type=F32)
        u = jnp.dot(h, wi_s[FFN_NCH + c], preferred_element_type=F32)
        return (_silu(a) * u).astype(BF16)

    x_next = residual_in(0)
    h_next = normed(x_next)
    early = []
    pending = []
    for s in range(nsub):
        rows = slice(s * sub, (s + 1) * sub)
        x, h = x_next, h_next
        acts, early = early, []
        for c in range(len(acts), FFN_NCH):
            if s + 1 < nsub:
                if c == FFN_NCH - FFN_MIX_LEAD:
                    x_next = residual_in(s + 1)
                if c == FFN_NCH - 1:
                    h_next = normed(x_next)
                    early = [act_chunk(h_next, 0)]
            acts.append(act_chunk(h, c))
        y = jnp.dot(jnp.concatenate(acts, axis=1), wo_s[...], preferred_element_type=F32)
        o = x + (0.5 * _mod(mod_ref, mi + 2, row)) * y
        if fg_ref is not None:
            o = (o * lax.rsqrt(jnp.mean(o * o, axis=-1, keepdims=True) + RMS_EPS)) * fg_ref[...]
        o_ref[rows, :] = o
        if proj is not None:
            pending.append(
                (_modulate(o, proj[0][...], _mod(mod_ref, 3, row), _mod(mod_ref, 4, row)).astype(BF16), rows))
    for h2, rows in pending:
        for stage in proj[1]:
            stage(h2, rows)


def _ffn_kernel(*refs, mi, nx, tiles_per_batch, ctx_row, has_ctx, has_mix, final, has_proj):
    refs = list(refs)
    x_ref = refs.pop(0)
    c_ref = refs.pop(0) if has_ctx else None
    mod_ref, g_ref, wi_ref, wo_ref = refs[:4]
    refs = refs[4:]
    mix = tuple(refs[:3]) if has_mix else None
    refs = refs[3:] if has_mix else refs
    fg_ref = refs.pop(0) if final else None
    proj = None
    if has_proj:
        g1_ref, wp_ref, w2_ref, b2_ref, cos_ref, sin_ref = refs[:N_PROJ_IN]
        outs = refs[N_PROJ_IN + 1:N_PROJ_IN + 1 + N_PROJ_OUT]
        proj = (g1_ref, _proj_stage_fns(wp_ref, w2_ref, b2_ref, cos_ref, sin_ref, outs))
        refs = refs[N_PROJ_IN:N_PROJ_IN + 1] + refs[N_PROJ_IN + 1 + N_PROJ_OUT:]
    o_ref, wi_s, wo_s = refs
    j = pl.program_id(0)

    @pl.when(j < FFN_NW)
    def _():
        for k in range(FFN_CHUNKS_PER_STEP):
            wi_s[j * FFN_CHUNKS_PER_STEP + k] = wi_ref[:, k * FFN_WCHUNK:(k + 1) * FFN_WCHUNK].astype(BF16)
        wo_s[pl.ds(pl.multiple_of(j * FFN_WO_ROWS, FFN_WO_ROWS), FFN_WO_ROWS), :] = wo_ref[...].astype(BF16)

    @pl.when((j >= FFN_NW) & (j < FFN_NW + nx))
    def _():
        _ffn_tile(x_ref, o_ref, mod_ref, (j - FFN_NW) // tiles_per_batch, g_ref, wi_s, wo_s, mi, mix, fg_ref, proj)

    if has_ctx:
        @pl.when(j >= FFN_NW + nx)
        def _():
            _ffn_tile(c_ref, o_ref, mod_ref, ctx_row, g_ref, wi_s, wo_s, mi, None, fg_ref, proj)


def _ffn(x, n_rows, tok_per_batch, mods, g, wi, wo, layer, mi, *, tm=TM_FFN, ctx=None, ctx_row=None, mix=None,
         final_g=None, proj=None):
    d = x.shape[1]
    nx = n_rows // tm
    nc = 0 if ctx is None else ctx.shape[0] // tm
    tiles_per_batch = tok_per_batch // tm

    def tile(j):
        return jnp.clip(j - FFN_NW, 0, nx + nc - 1)

    def x_tile(j):
        return jnp.minimum(tile(j), nx - 1)

    def wchunk(j):
        return jnp.minimum(j, FFN_NW - 1)

    def pos_blk(j):
        return jnp.where(tile(j) < nx, tile(j) % tiles_per_batch, tiles_per_batch)

    in_specs = [pl.BlockSpec((tm, d), lambda j: (x_tile(j), 0))]
    args = [x]
    if ctx is not None:
        in_specs.append(pl.BlockSpec((tm, d), lambda j: (jnp.clip(j - FFN_NW - nx, 0, nc - 1), 0)))
        args.append(ctx)
    in_specs += [
        _mod_spec(layer, d),
        _const_spec((1, d)),
        pl.BlockSpec((None, d, FFN_WCHUNK * FFN_CHUNKS_PER_STEP), lambda j: (layer, 0, wchunk(j))),
        pl.BlockSpec((None, FFN_WO_ROWS, d), lambda j: (layer, wchunk(j), 0)),
    ]
    args += [mods, g.reshape(1, d), wi, wo]
    if mix is not None:
        oa, ob, w_out = mix
        in_specs += [pl.BlockSpec((tm, A_Q), lambda j: (x_tile(j), 0)),
                     pl.BlockSpec((tm, B_V), lambda j: (x_tile(j), 0)),
                     _const_spec(w_out.shape)]
        args += [oa, ob, w_out]
    if final_g is not None:
        in_specs.append(_const_spec((1, d)))
        args.append(final_g.reshape(1, d))
    n_out_rows = (nx + nc) * tm
    out_specs = [pl.BlockSpec((tm, d), lambda j: (tile(j), 0))]
    out_shape = [jax.ShapeDtypeStruct((n_out_rows, d), F32)]
    if proj is not None:
        g1, w_in, w2, b2, cos, sin = proj
        in_specs += [_const_spec((1, d)), _const_spec(w_in.shape), _const_spec(w2.shape),
                     _const_spec(b2.shape),
                     pl.BlockSpec((tm, LANES), lambda j: (pos_blk(j), 0)),
                     pl.BlockSpec((tm, LANES), lambda j: (pos_blk(j), 0))]
        args += [g1.reshape(1, d), w_in, w2, b2, cos, sin]
        out_specs += [pl.BlockSpec((tm, w), lambda j: (tile(j), 0)) for w, _ in PROJ_OUTS]
        out_shape += [jax.ShapeDtypeStruct((n_out_rows, w), dt) for w, dt in PROJ_OUTS]
    res = pl.pallas_call(
        functools.partial(_ffn_kernel, mi=mi, nx=nx, tiles_per_batch=tiles_per_batch, ctx_row=ctx_row,
                          has_ctx=ctx is not None, has_mix=mix is not None,
                          final=final_g is not None, has_proj=proj is not None),
        grid=(FFN_NW + nx + nc,),
        in_specs=in_specs,
        out_specs=out_specs,
        out_shape=out_shape,
        scratch_shapes=[pltpu.VMEM((2 * FFN_NCH, d, FFN_WCHUNK), BF16), pltpu.VMEM((D_FF, d), BF16)],
        compiler_params=_cparams("arbitrary"),
        name="ffn_proj" if proj is not None else "ffn",
    )(*args)
    return res if proj is not None else res[0]


def _swap16(x):
    lane = lax.broadcasted_iota(jnp.int32, x.shape, 1)
    return jnp.where((lane & 16) == 0, pltpu.roll(x, LANES - 16, 1), pltpu.roll(x, 16, 1))


def _dup_halves(x):
    lane = lax.broadcasted_iota(jnp.int32, x.shape, 1)
    r = pltpu.roll(x, LANES // 2, 1)
    lo = lane < LANES // 2
    return jnp.where(lo, x, r), jnp.where(lo, r, x)


def _halves_with_ones(x):
    lane = lax.broadcasted_iota(jnp.int32, x.shape, 1)
    r = pltpu.roll(x, LANES // 2, 1)
    lo = lane < LANES // 2
    return jnp.where(lo, x, 1.0), jnp.where(lo, r, 1.0)


Z_QA = (A_Q, 0)
Z_KV = (KV_DUP, 1)
Z_VB = (B_V, 2)
Z_RB = (B_V, 3)
Z_QB = (B_QK, 8)
Z_KB = (B_QK, 9)
Z_WIDTH = (Z_KB[1] + 1) * B_QK
PROJ_OUTS = ((Z_WIDTH, BF16), (2 * B_QK, F32))
N_PROJ_OUT = len(PROJ_OUTS)
N_PROJ_IN = 6


def _z_cols(field):
    w, blk = field
    return slice(w * blk, w * (blk + 1))


def _proj_stage_fns(w_ref, w2_ref, b2_ref, cos_ref, sin_ref, outs):
    z_ref, la_ref = outs
    qa0, kv0 = _z_cols(Z_QA).start, _z_cols(Z_KV).start
    scale = A_HEAD_DIM ** -0.5 * LOG2E
    o_b = A_Q + 2 * A_KV

    def cols(h, lo, hi):
        return jnp.dot(h, w_ref[:, lo:hi], preferred_element_type=F32)

    def rope(v, rows):
        return v * cos_ref[rows, :] + _swap16(v) * sin_ref[rows, :]

    def attn_q(h, rows):
        z = cols(h, 0, A_Q)
        for p in range(A_Q // LANES):
            q = z[:, p * LANES:(p + 1) * LANES]
            z_ref[rows, qa0 + p * LANES:qa0 + (p + 1) * LANES] = (rope(q, rows) * scale).astype(BF16)

    def attn_kv(h, rows):
        z = cols(h, A_Q, A_Q + 2 * A_KV)
        k0, k1 = _dup_halves(rope(z[:, :A_KV], rows))
        v0, v1 = _halves_with_ones(z[:, A_KV:])
        for j, piece in enumerate((k0, k1, v0, v1)):
            z_ref[rows, kv0 + j * LANES:kv0 + (j + 1) * LANES] = piece.astype(BF16)

    def gla_qk(h, rows):
        z = cols(h, o_b, o_b + 2 * B_QK)
        z_ref[rows, _z_cols(Z_QB)] = z[:, :B_QK].astype(BF16)
        z_ref[rows, _z_cols(Z_KB)] = z[:, B_QK:].astype(BF16)

    def gla_v(h, rows):
        z_ref[rows, _z_cols(Z_VB)] = cols(h, o_b + 2 * B_QK, o_b + 2 * B_QK + B_V).astype(BF16)

    def gla_r(h, rows):
        z_ref[rows, _z_cols(Z_RB)] = cols(h, o_b + 2 * B_QK + B_V, o_b + 2 * B_QK + 2 * B_V).astype(BF16)

    def gla_gate(h, rows):
        zg = jnp.dot(h, w_ref[:, PROJ_MAIN:], preferred_element_type=F32).astype(BF16)
        pre = jnp.dot(zg, w2_ref[...], preferred_element_type=F32) + b2_ref[...]
        la_ref[rows, :] = (jnp.minimum(pre, 0.0) - jnp.log1p(jnp.exp(-jnp.abs(pre)))) / B_GATE_NORM

    return (gla_gate, attn_q, attn_kv, gla_qk, gla_v, gla_r)


ATTN_QBLOCKS = 16


def _attn_kernel(sink_ref, q_ref, kvp_ref, kvc_ref, kvn_ref, ckv_ref, o_ref, *, nsteps):
    n = pl.program_id(1)
    blk = WINDOW
    row = lax.broadcasted_iota(jnp.int32, (blk, blk), 0)
    col = lax.broadcasted_iota(jnp.int32, (blk, blk), 1)
    ninf = jnp.float32(-jnp.inf)
    lane = lax.broadcasted_iota(jnp.int32, (blk, LANES), 1)
    lo = lane < LANES // 2
    rep = A_HEADS // A_KV_HEADS

    def key_block(i, lanes):
        if i == 0:
            return kvp_ref[:, lanes]
        if i == ATTN_QBLOCKS + 1:
            return kvn_ref[:, lanes]
        return kvc_ref[(i - 1) * blk:i * blk, lanes]

    def band(qb, lanes):
        return jnp.concatenate([key_block(qb + j, lanes) for j in range(3)] + [ckv_ref[:, lanes]], axis=0)

    kd = [[band(qb, slice(g * LANES, (g + 1) * LANES)) for g in range(A_KV_HEADS)] for qb in range(ATTN_QBLOCKS)]
    vd = [[band(qb, slice((A_KV_HEADS + g) * LANES, (A_KV_HEADS + g + 1) * LANES)) for g in range(A_KV_HEADS)]
          for qb in range(ATTN_QBLOCKS)]
    nkb = kd[0][0].shape[0] // blk
    bias_prev = [jnp.where((col >= row) & ((n > 0) if qb == 0 else True), 0.0, ninf) for qb in range(ATTN_QBLOCKS)]
    bias_next = [jnp.where((col <= row) & ((n < nsteps - 1) if qb == ATTN_QBLOCKS - 1 else True), 0.0, ninf)
                 for qb in range(ATTN_QBLOCKS)]

    def scores(qb, p):
        pair = q_ref[qb * blk:(qb + 1) * blk, p * LANES:(p + 1) * LANES]
        zero = jnp.zeros_like(pair)
        lhs = jnp.concatenate([jnp.where(lo, pair, zero), jnp.where(lo, zero, pair)], axis=0)
        return lax.dot_general(lhs, kd[qb][(2 * p) // rep], (((1,), (1,)), ((), ())), preferred_element_type=F32)

    def softmax(qb, p, s):
        ps, sinks = [], []
        for r in range(2):
            sk = sink_ref[2 * p + r] * LOG2E
            cb = [s[r * blk:(r + 1) * blk, j * blk:(j + 1) * blk] for j in range(nkb)]
            cb[0] = cb[0] + bias_prev[qb]
            cb[2] = cb[2] + bias_next[qb]
            mx = cb[0]
            for c in cb[1:]:
                mx = jnp.maximum(mx, c)
            m = jnp.maximum(jnp.max(mx, axis=-1, keepdims=True), sk)
            ps.append(jnp.concatenate([jnp.exp2((c - m).astype(BF16)) for c in cb], axis=1))
            sinks.append(jnp.exp2(sk - m))
        return jnp.concatenate(ps, axis=0), sinks

    def weighted_values(qb, p, probs, sinks):
        o = jnp.dot(probs, vd[qb][(2 * p) // rep], preferred_element_type=F32)
        o_even, o_odd = o[:blk, :], o[blk:, :]
        num = jnp.where(lo, o_even, pltpu.roll(o_odd, LANES // 2, 1))
        den = jnp.where(lo, pltpu.roll(o_even, LANES // 2, 1) + sinks[0], o_odd + sinks[1])
        o_ref[qb * blk:(qb + 1) * blk, p * LANES:(p + 1) * LANES] = (num / den).astype(BF16)

    units = [(qb, p) for qb in range(ATTN_QBLOCKS) for p in range(A_HEADS // 2)]
    s_next = scores(*units[0])
    for i, unit in enumerate(units):
        s_cur = s_next
        if i + 1 < len(units):
            s_next = scores(*units[i + 1])
        probs, sinks = softmax(*unit, s_cur)
        weighted_values(*unit, probs, sinks)


def _attention(sink, z, b, t, lctx):
    nb = t // WINDOW
    tq = ATTN_QBLOCKS * WINDOW
    nsteps = t // tq
    ctx0 = (b * t) // lctx

    def edge(off):
        return pl.BlockSpec((WINDOW, KV_DUP),
                            lambda bi, n: (bi * nb + jnp.clip(n * ATTN_QBLOCKS + off, 0, nb - 1), Z_KV[1]))

    return pl.pallas_call(
        functools.partial(_attn_kernel, nsteps=nsteps),
        grid=(b, nsteps),
        in_specs=[
            pl.BlockSpec(memory_space=pltpu.SMEM),
            pl.BlockSpec((tq, A_Q), lambda bi, n: (bi * nsteps + n, Z_QA[1])),
            edge(-1),
            pl.BlockSpec((tq, KV_DUP), lambda bi, n: (bi * nsteps + n, Z_KV[1])),
            edge(ATTN_QBLOCKS),
            pl.BlockSpec((lctx, KV_DUP), lambda bi, n: (ctx0 + bi, Z_KV[1])),
        ],
        out_specs=pl.BlockSpec((tq, A_Q), lambda bi, n: (bi * nsteps + n, 0)),
        out_shape=jax.ShapeDtypeStruct((b * t, A_Q), BF16),
        compiler_params=_cparams("parallel", "parallel"),
        name="window_attn",
    )(sink, z, z, z, z, z)


B_PAIRS = B_HEADS // 2
PAIR_QK = 2 * B_DK
PAIR_V = 2 * B_DV


class _GlaMasks:
    def __init__(self, reverse):
        c = B_CHUNK
        i = lax.broadcasted_iota(jnp.int32, (c, c), 0)
        j = lax.broadcasted_iota(jnp.int32, (c, c), 1)
        self.tri = jnp.where((j >= i) if reverse else (j <= i), 1.0, 0.0).astype(BF16)
        lane_qk = lax.broadcasted_iota(jnp.int32, (c, B_QK), 1)
        self.k_head = [(lane_qk >= h * B_DK) & (lane_qk < (h + 1) * B_DK) for h in range(B_HEADS)]
        lane_v = lax.broadcasted_iota(jnp.int32, (c, B_V), 1)
        self.v_head = [(lane_v >= h * B_DV) & (lane_v < (h + 1) * B_DV) for h in range(B_HEADS)]
        ai = lax.broadcasted_iota(jnp.int32, (c, B_HEADS * c), 0)
        aj = lax.broadcasted_iota(jnp.int32, (c, B_HEADS * c), 1) % c
        self.causal = (aj >= ai) if reverse else (aj <= ai)
        sr = lax.broadcasted_iota(jnp.int32, (PAIR_V, PAIR_QK), 0) // B_DV
        sl = lax.broadcasted_iota(jnp.int32, (PAIR_V, PAIR_QK), 1) // B_DK
        self.state_diag = sr == sl


def _gla_block(q_ref, k_ref, v_ref, la_ref, st, mk, reverse, want_out):
    c = B_CHUNK
    nc = k_ref.shape[0] // c
    order = list(reversed(range(nc))) if reverse else list(range(nc))
    rows = {ci: slice(ci * c, (ci + 1) * c) for ci in order}
    tb = (((1,), (1,)), ((), ()))
    ta = (((0,), (0,)), ((), ()))

    g = {}
    for ci in order:
        la = la_ref[rows[ci], :]
        hi = la.astype(BF16)
        lo = (la - hi.astype(F32)).astype(BF16)
        g[ci] = jnp.dot(mk.tri, hi, preferred_element_type=F32) + jnp.dot(mk.tri, lo, preferred_element_type=F32)

    o = {}
    for g0 in range(0, nc, GLA_GROUP):
        group = order[g0:g0 + GLA_GROUP]
        dec, k_end, q_t, kstack = {}, {}, {}, {}
        for ci in group:
            k = k_ref[rows[ci], :].astype(F32)
            g_last = g[ci][0:1, :] if reverse else g[ci][c - 1:c, :]
            dec[ci] = jnp.exp(g_last)
            k_end[ci] = (k * jnp.exp(g_last - g[ci])).astype(BF16)
            if want_out:
                q_t[ci] = ((q_ref[rows[ci], :].astype(F32) * (B_DK ** -0.5)) * jnp.exp(g[ci])).astype(BF16)
                k_t = k * jnp.exp(-g[ci])
                kstack[ci] = jnp.concatenate([jnp.where(m, k_t, 0.0) for m in mk.k_head], axis=0).astype(BF16)

        ds = {}
        for ci in group:
            v = v_ref[rows[ci], :]
            ds[ci] = [jnp.where(mk.state_diag,
                                lax.dot_general(v[:, p * PAIR_V:(p + 1) * PAIR_V],
                                                k_end[ci][:, p * PAIR_QK:(p + 1) * PAIR_QK], ta,
                                                preferred_element_type=F32), 0.0) for p in range(B_PAIRS)]

        st_prev = {}
        for ci in group:
            st_prev[ci] = st
            st = [st[p] * dec[ci][:, p * PAIR_QK:(p + 1) * PAIR_QK] + ds[ci][p] for p in range(B_PAIRS)]
        if not want_out:
            continue

        att = {}
        for ci in group:
            a = lax.dot_general(q_t[ci], kstack[ci], tb, preferred_element_type=F32)
            att[ci] = jnp.where(mk.causal, a, 0.0).astype(BF16)
        for ci in group:
            v = v_ref[rows[ci], :]
            vstack = jnp.concatenate([jnp.where(m, v, jnp.zeros_like(v)) for m in mk.v_head], axis=0)
            o[ci] = jnp.dot(att[ci], vstack, preferred_element_type=F32) + jnp.concatenate(
                [lax.dot_general(q_t[ci][:, p * PAIR_QK:(p + 1) * PAIR_QK], st_prev[ci][p].astype(BF16), tb,
                                 preferred_element_type=F32) for p in range(B_PAIRS)], axis=1)
    if not want_out:
        return None, st
    return o, st


def _gla_ctx_kernel(k_ref, v_ref, la_ref, st_ref, *, reverse):
    mk = _GlaMasks(reverse)
    st = [jnp.zeros((PAIR_V, PAIR_QK), F32) for _ in range(B_PAIRS)]
    _, st = _gla_block(None, k_ref, v_ref, la_ref, st, mk, reverse, False)
    for p in range(B_PAIRS):
        st_ref[p] = st[p]


def _gla_ctx_state(z, la, b, t, l, reverse):
    d = 1 if reverse else 0
    ctx0 = (b * t) // l
    return pl.pallas_call(
        functools.partial(_gla_ctx_kernel, reverse=reverse),
        grid=(b,),
        in_specs=[
            pl.BlockSpec((l, B_QK), lambda bi: (ctx0 + bi, Z_KB[1])),
            pl.BlockSpec((l, B_V), lambda bi: (ctx0 + bi, Z_VB[1])),
            pl.BlockSpec((l, B_QK), lambda bi: (ctx0 + bi, d)),
        ],
        out_specs=pl.BlockSpec((None, B_PAIRS, PAIR_V, PAIR_QK), lambda bi: (bi, 0, 0, 0)),
        out_shape=jax.ShapeDtypeStruct((b, B_PAIRS, PAIR_V, PAIR_QK), F32),
        compiler_params=_cparams("parallel"),
        name="gla_ctx_state",
    )(z, z, la)


def _gla_kernel(q_ref, k_ref, v_ref, la_ref, st0_ref, *rest, reverse):
    if reverse:
        of_ref, r_ref, gg_ref, o_ref, st_ref = rest
    else:
        o_ref, st_ref = rest

    @pl.when(pl.program_id(1) == 0)
    def _():
        st_ref[...] = st0_ref[...]

    mk = _GlaMasks(reverse)
    outs_by_chunk, st = _gla_block(q_ref, k_ref, v_ref, la_ref, [st_ref[p] for p in range(B_PAIRS)],
                                   mk, reverse, True)
    for ci, o in outs_by_chunk.items():
        rows = slice(ci * B_CHUNK, (ci + 1) * B_CHUNK)
        if reverse:
            o = o + of_ref[rows, :]
            outs = []
            for h in range(B_HEADS):
                oh = o[:, h * B_DV:(h + 1) * B_DV]
                oh = (oh * lax.rsqrt(jnp.mean(oh * oh, axis=-1, keepdims=True) + RMS_EPS)) * gg_ref[...]
                outs.append(oh * _silu(r_ref[rows, h * B_DV:(h + 1) * B_DV].astype(F32)))
            o_ref[rows, :] = jnp.concatenate(outs, axis=1).astype(BF16)
        else:
            o_ref[rows, :] = o
    for p in range(B_PAIRS):
        st_ref[p] = st[p]


def _gla(z, la, st0, b, t, reverse, o_fwd=None, gla_g=None):
    tb = TB_GLA
    nt = t // tb
    d = 1 if reverse else 0

    def tok(w, lane_blk=0):
        if reverse:
            return pl.BlockSpec((tb, w), lambda bi, j: (bi * nt + nt - 1 - j, lane_blk))
        return pl.BlockSpec((tb, w), lambda bi, j: (bi * nt + j, lane_blk))

    in_specs = [tok(*Z_QB), tok(*Z_KB), tok(*Z_VB), tok(B_QK, d),
                pl.BlockSpec((None, B_PAIRS, PAIR_V, PAIR_QK), lambda bi, j: (bi, 0, 0, 0))]
    args = [z, z, z, la, st0]
    if reverse:
        in_specs += [tok(B_V), tok(*Z_RB), _const_spec((1, B_DV))]
        args += [o_fwd, z, gla_g.reshape(1, B_DV)]
    return pl.pallas_call(
        functools.partial(_gla_kernel, reverse=reverse),
        grid=(b, nt),
        in_specs=in_specs,
        out_specs=tok(B_V),
        out_shape=jax.ShapeDtypeStruct((b * t, B_V), BF16 if reverse else F32),
        scratch_shapes=[pltpu.VMEM((B_PAIRS, PAIR_V, PAIR_QK), F32)],
        compiler_params=_cparams("parallel", "arbitrary"),
        name="gla_bwd" if reverse else "gla_fwd",
    )(*args)


def _shift_rows(e, d):
    n = e.shape[0]
    return pltpu.roll(e, (-d) % n, 0)


def _pool_kernel(x_ref, xp_ref, xn_ref, mod_ref, g_ref, w_ref, ps_ref, o_ref, *, seq):
    i = pl.program_id(1)
    tm = x_ref.shape[0]
    g = g_ref[...]
    row = pl.program_id(0)
    shift = _mod(mod_ref, 3, row)
    gain = g * (1.0 + _mod(mod_ref, 4, row))
    out_gain = _mod(mod_ref, 5, row) * ps_ref[...]

    def normed(v):
        return (v * lax.rsqrt(jnp.mean(v * v, axis=-1, keepdims=True) + RMS_EPS)) * gain + shift

    x = x_ref[...]
    h = normed(x)
    hp = jnp.where(i > 0, normed(xp_ref[...]), 0.0)
    hn = jnp.where(i < pl.num_programs(1) - 1, normed(xn_ref[...]), 0.0)
    groups, totals = [], []
    for gi, w in enumerate(POOL_WINDOWS):
        ls = slice(gi * POOL_GROUP, (gi + 1) * POOL_GROUP)
        groups.append(h[:, ls])
        f = jnp.concatenate([hp[:, ls], h[:, ls], hn[:, ls]], axis=0)
        half = w // 2
        k = 1
        while k < half:
            f = f + _shift_rows(f, k)
            k *= 2
        if half == HALO:
            totals.append(f[:tm, :] + f[HALO:HALO + tm, :])
        else:
            totals.append((_shift_rows(f, -half) + f)[HALO:HALO + tm, :])

    def finish(scales):
        ys = [jnp.dot((totals[gi] * scales[gi] - groups[gi]).astype(BF16), w_ref[gi], preferred_element_type=F32)
              for gi in range(len(POOL_WINDOWS))]
        o_ref[...] = x + out_gain * jnp.concatenate(ys, axis=1)

    clipped = (i == 0) | (i == pl.num_programs(1) - 1)

    @pl.when(clipped)
    def _():
        def edge(t0, w):
            t = t0 + lax.broadcasted_iota(jnp.int32, (HALO, 1), 0)
            return 1.0 / (jnp.minimum(t + w // 2, seq) - jnp.maximum(t - w // 2, 0)).astype(F32)

        finish([jnp.concatenate([edge(i * tm, w), jnp.full((tm - 2 * HALO, 1), 1.0 / w, F32),
                                 edge((i + 1) * tm - HALO, w)], axis=0) for w in POOL_WINDOWS])

    @pl.when(jnp.logical_not(clipped))
    def _():
        finish([1.0 / w for w in POOL_WINDOWS])


def _pool(x, mods, layer, g, w_pool, pool_scale):
    b, t, d = x.shape
    tm = TM_POOL
    hb = tm // HALO
    last = t // HALO - 1
    return pl.pallas_call(
        functools.partial(_pool_kernel, seq=t),
        grid=(b, t // tm),
        in_specs=[
            pl.BlockSpec((None, tm, d), lambda bi, i: (bi, i, 0)),
            pl.BlockSpec((None, HALO, d), lambda bi, i: (bi, jnp.maximum(i * hb - 1, 0), 0)),
            pl.BlockSpec((None, HALO, d), lambda bi, i: (bi, jnp.minimum((i + 1) * hb, last), 0)),
            _mod_spec(layer, d),
            _const_spec((1, d)),
            _const_spec(w_pool.shape),
            _const_spec((1, d)),
        ],
        out_specs=pl.BlockSpec((None, tm, d), lambda bi, i: (bi, i, 0)),
        out_shape=jax.ShapeDtypeStruct((b, t, d), F32),
        compiler_params=_cparams("parallel", "parallel"),
        name="pool_mixer",
    )(x, x, x, mods, g.reshape(1, d), w_pool, pool_scale.reshape(1, d))


def _rope_tables(t, pad_rows):
    n = A_HEAD_DIM // 4
    freqs = (np.float32(ROPE_BASE) ** (-np.arange(n, dtype=np.float32) / np.float32(n))).astype(np.float32)
    pos = np.arange(t)
    ang_r = (pos // GRID_W).astype(np.float32)[:, None] * freqs
    ang_c = (pos % GRID_W).astype(np.float32)[:, None] * freqs
    cos = np.concatenate([np.cos(ang_r), np.cos(ang_r), np.cos(ang_c), np.cos(ang_c)], axis=1)
    sin = np.concatenate([-np.sin(ang_r), np.sin(ang_r), -np.sin(ang_c), np.sin(ang_c)], axis=1)
    cos = np.concatenate([np.tile(cos, (1, 2)), np.ones((pad_rows, LANES), np.float32)], axis=0)
    sin = np.concatenate([np.tile(sin, (1, 2)), np.zeros((pad_rows, LANES), np.float32)], axis=0)
    return jnp.asarray(cos, F32), jnp.asarray(sin, F32)


def kernel(x, c, ctx, c_ctx, w_mod, b_mod, norm_g, ffn1_wi, ffn1_wo, ffn2_wi, ffn2_wo,
           w_in, w_a2_f, b_a_f, w_a2_b, b_a_b, sink, gla_g, w_out, w_pool, pool_scale, final_g):
    b, t, d = x.shape
    lctx = ctx.shape[1]
    n_tok = b * t
    ctx_row = b

    cond = jnp.concatenate([c, c_ctx[None, :], jnp.zeros((COND_ROWS - b - 1, d), F32)], axis=0)
    mods = _adaln(cond, w_mod, b_mod)
    x = x.reshape(n_tok, d)

    zr = jnp.zeros((B_GATE_RANK, B_QK), F32)
    w2 = jnp.concatenate([jnp.concatenate([w_a2_f[0], zr], axis=1),
                          jnp.concatenate([zr, w_a2_b[0]], axis=1)], axis=0).astype(BF16)
    b2 = jnp.concatenate([b_a_f[0], b_a_b[0]]).reshape(1, 2 * B_QK)
    cos, sin = _rope_tables(t, TM_FFN_PROJ)
    xc, z, la = _ffn(
        x, n_tok, t, mods, norm_g[0, 0], ffn1_wi, ffn1_wo, 0, 0, tm=TM_FFN_PROJ,
        ctx=ctx.reshape(b * lctx, d), ctx_row=ctx_row, proj=(norm_g[0, 1], w_in[0].astype(BF16), w2, b2, cos, sin))

    o_a = _attention(sink[0], z, b, t, lctx)
    st_f = _gla_ctx_state(z, la, b, t, lctx, False)
    st_b = _gla_ctx_state(z, la, b, t, lctx, True)
    o_f = _gla(z, la, st_f, b, t, False)
    o_g = _gla(z, la, st_b, b, t, True, o_f, gla_g[0])
    x = _ffn(xc, n_tok, t, mods, norm_g[0, 2], ffn2_wi, ffn2_wo, 0, 6, mix=(o_a, o_g, w_out[0].astype(BF16)))

    x = _ffn(x, n_tok, t, mods, norm_g[1, 0], ffn1_wi, ffn1_wo, 1, 0)
    x = _pool(x.reshape(b, t, d), mods, 1, norm_g[1, 1], w_pool[0].astype(BF16), pool_scale[0])
    x = _ffn(x.reshape(n_tok, d), n_tok, t, mods, norm_g[1, 2], ffn2_wi, ffn2_wo, 1, 6, final_g=final_g)
    return x.reshape(b, t, d)
```
